```python
import math
import jax, jax.numpy as jnp
from jax import lax
import numpy as np

D_MODEL = 2048
BATCH = 1
SEQ = 8192
DEPTH = 4

CHUNK = 64
N_MIXERS = 2
EXPAND = 2
D_INNER = EXPAND * D_MODEL

GLA_HEADS = 4
GLA_DK = (D_MODEL // 2) // GLA_HEADS
GLA_DV = D_INNER // GLA_HEADS
GLA_QK = GLA_HEADS * GLA_DK
GLA_GATE_RANK = 16
GLA_TAU = 16.0
GLA_IN_WIDTH = 2 * GLA_QK + 2 * D_INNER + GLA_GATE_RANK

S5_GROUP = 16
S5_STATE = 64
S5_GROUPS = D_INNER // S5_GROUP
S5_GROUPS_PER_BLOCK = 16
S5_BLOCKS = S5_GROUPS // S5_GROUPS_PER_BLOCK
S5_DT_MIN = 1e-3
S5_DT_MAX = 1e-1

N_GLA_LAYERS = (DEPTH + 1) // 2
N_S5_LAYERS = DEPTH // 2

DEEPNORM_ALPHA = (2 * DEPTH) ** 0.25
DEEPNORM_BETA = (8 * DEPTH) ** -0.25
LN_EPS = 1e-5
RMS_EPS = 1e-6

kernel_name = "hybrid_gla_s5_deepnorm_adaln"


def layer_norm(x, g, b):
    xf = x.astype(jnp.float32)
    mu = jnp.mean(xf, axis=-1, keepdims=True)
    var = jnp.mean(jnp.square(xf - mu), axis=-1, keepdims=True)
    y = (xf - mu) * lax.rsqrt(var + LN_EPS) * g.astype(jnp.float32) + b.astype(jnp.float32)
    return y.astype(x.dtype)


def gla_mixer(u, w_in, gate_w2, gate_b, norm_g, w_out):
    bsz, seq_len, _ = u.shape
    n_chunks = seq_len // CHUNK
    proj = u @ w_in
    q, k, v, z, g_lr = jnp.split(
        proj, [GLA_QK, 2 * GLA_QK, 2 * GLA_QK + D_INNER, 2 * GLA_QK + 2 * D_INNER], axis=-1)
    log_alpha = jax.nn.log_sigmoid((g_lr @ gate_w2 + gate_b).astype(jnp.float32)) / GLA_TAU

    def to_chunks(t, d):
        return t.astype(jnp.float32).reshape(bsz, n_chunks, CHUNK, GLA_HEADS, d).transpose(1, 0, 3, 2, 4)

    qc = to_chunks(q, GLA_DK) * (GLA_DK ** -0.5)
    kc = to_chunks(k, GLA_DK)
    vc = to_chunks(v, GLA_DV)
    gc = to_chunks(log_alpha, GLA_DK)

    def step(state, inp):
        q_c, k_c, v_c, g_c = inp
        cum = jnp.cumsum(g_c, axis=2)
        tot = cum[:, :, -1:, :]
        k_dec = k_c * jnp.exp(tot - cum)
        state = jnp.exp(tot[:, :, 0, :, None]) * state + jnp.einsum('bhck,bhcv->bhkv', k_dec, v_c)
        o = jnp.einsum('bhck,bhkv->bhcv', q_c, state)
        return state, o

    s0 = jnp.zeros((bsz, GLA_HEADS, GLA_DK, GLA_DV), jnp.float32)
    _, o = lax.scan(step, s0, (qc, kc, vc, gc))
    o = o.transpose(1, 0, 3, 2, 4).reshape(bsz, seq_len, GLA_HEADS, GLA_DV)
    o = o * lax.rsqrt(jnp.mean(jnp.square(o), axis=-1, keepdims=True) + RMS_EPS)
    o = o.reshape(bsz, seq_len, D_INNER) * norm_g.astype(jnp.float32)
    y = o.astype(u.dtype) * jax.nn.silu(z)
    return y @ w_out


def _linear_recurrence_combine(e1, e2):
    a1r, a1i, b1r, b1i = e1
    a2r, a2i, b2r, b2i = e2
    ar = a2r * a1r - a2i * a1i
    ai = a2r * a1i + a2i * a1r
    br = a2r * b1r - a2i * b1i + b2r
    bi = a2r * b1i + a2i * b1r + b2i
    return (ar, ai, br, bi)


def s5_mixer(u_in, w_in, a_re, a_im, log_dt, b_re, b_im, c_re, c_im, d_skip, w_glu, b_glu, w_out):
    bsz, seq_len, _ = u_in.shape
    proj = u_in @ w_in
    u, z = jnp.split(proj, 2, axis=-1)
    f32 = jnp.float32
    a_re = a_re.astype(f32); a_im = a_im.astype(f32)
    b_re = b_re.astype(f32); b_im = b_im.astype(f32)
    dt = jnp.exp(log_dt.astype(f32))[:, None]
    mag = jnp.exp(a_re * dt)
    lb_re = mag * jnp.cos(a_im * dt)
    lb_im = mag * jnp.sin(a_im * dt)
    nr = lb_re - 1.0
    den = jnp.square(a_re) + jnp.square(a_im)
    coef_re = (nr * a_re + lb_im * a_im) / den
    coef_im = (lb_im * a_re - nr * a_im) / den
    bb_re = coef_re[..., None] * b_re - coef_im[..., None] * b_im
    bb_im = coef_re[..., None] * b_im + coef_im[..., None] * b_re

    uf = u.astype(f32)
    u_blocks = uf.reshape(bsz, seq_len, S5_BLOCKS, S5_GROUPS_PER_BLOCK, S5_GROUP).transpose(2, 0, 1, 3, 4)

    def blk(args):
        u_b, lr_b, li_b, br_b, bi_b, cr_b, ci_b = args
        bu_r = jnp.einsum('blgi,gpi->blgp', u_b, br_b)
        bu_i = jnp.einsum('blgi,gpi->blgp', u_b, bi_b)
        a_r = jnp.broadcast_to(lr_b, bu_r.shape)
        a_i = jnp.broadcast_to(li_b, bu_i.shape)
        _, _, x_r, x_i = lax.associative_scan(_linear_recurrence_combine, (a_r, a_i, bu_r, bu_i), axis=1)
        return jnp.einsum('blgp,gip->blgi', x_r, cr_b) - jnp.einsum('blgp,gip->blgi', x_i, ci_b)

    nb, gpb = S5_BLOCKS, S5_GROUPS_PER_BLOCK
    y = lax.map(blk, (
        u_blocks,
        lb_re.reshape(nb, gpb, S5_STATE), lb_im.reshape(nb, gpb, S5_STATE),
        bb_re.reshape(nb, gpb, S5_STATE, S5_GROUP), bb_im.reshape(nb, gpb, S5_STATE, S5_GROUP),
        c_re.astype(f32).reshape(nb, gpb, S5_GROUP, S5_STATE),
        c_im.astype(f32).reshape(nb, gpb, S5_GROUP, S5_STATE)))
    y = y.transpose(1, 2, 0, 3, 4).reshape(bsz, seq_len, D_INNER) + d_skip.astype(f32) * uf
    y = jax.nn.gelu(y)
    y = y * jax.nn.sigmoid(y @ w_glu.astype(f32) + b_glu.astype(f32))
    y = y.astype(u_in.dtype) * jax.nn.silu(z)
    return y @ w_out


def setup_inputs(seed: int = 0) -> dict:
    key = jax.random.key(seed)
    ks = jax.random.split(key, 24)
    nrm = jax.random.normal
    d, e = D_MODEL, D_INNER
    x = nrm(ks[0], (BATCH, SEQ, d), jnp.float32)
    c = nrm(ks[1], (BATCH, d), jnp.float32)
    ln_g = 1.0 + 0.01 * nrm(ks[2], (DEPTH, d), jnp.float32)
    ln_b = 0.01 * nrm(ks[3], (DEPTH, d), jnp.float32)
    ada_w = 0.2 * d ** -0.5 * nrm(ks[4], (DEPTH, d, 3 * d), jnp.float32)
    ada_b = 0.01 * nrm(ks[5], (DEPTH, 3 * d), jnp.float32)
    gla_w_in = d ** -0.5 * nrm(ks[6], (N_GLA_LAYERS, d, GLA_IN_WIDTH), jnp.float32)
    gla_gate_w2 = GLA_GATE_RANK ** -0.5 * nrm(ks[7], (N_GLA_LAYERS, GLA_GATE_RANK, GLA_QK), jnp.float32)
    gla_gate_b = 0.01 * nrm(ks[8], (N_GLA_LAYERS, GLA_QK), jnp.float32)
    gla_norm_g = 1.0 + 0.01 * nrm(ks[9], (N_GLA_LAYERS, e), jnp.float32)
    gla_w_out = DEEPNORM_BETA * e ** -0.5 * nrm(ks[10], (N_GLA_LAYERS, e, d), jnp.float32)
    s5_w_in = d ** -0.5 * nrm(ks[11], (N_S5_LAYERS, d, 2 * e), jnp.float32)
    s5_a_re = -0.5 + 0.01 * nrm(ks[12], (N_S5_LAYERS, S5_GROUPS, S5_STATE), jnp.float32)
    s5_a_im = jnp.broadcast_to(jnp.pi * jnp.arange(S5_STATE, dtype=jnp.float32),
                               (N_S5_LAYERS, S5_GROUPS, S5_STATE))
    s5_log_dt = jax.random.uniform(ks[13], (N_S5_LAYERS, S5_GROUPS), jnp.float32,
                                   math.log(S5_DT_MIN), math.log(S5_DT_MAX))
    bscale = (2 * S5_GROUP) ** -0.5
    s5_b_re = bscale * nrm(ks[14], (N_S5_LAYERS, S5_GROUPS, S5_STATE, S5_GROUP), jnp.float32)
    s5_b_im = bscale * nrm(ks[15], (N_S5_LAYERS, S5_GROUPS, S5_STATE, S5_GROUP), jnp.float32)
    cscale = S5_STATE ** -0.5
    s5_c_re = cscale * nrm(ks[16], (N_S5_LAYERS, S5_GROUPS, S5_GROUP, S5_STATE), jnp.float32)
    s5_c_im = cscale * nrm(ks[17], (N_S5_LAYERS, S5_GROUPS, S5_GROUP, S5_STATE), jnp.float32)
    s5_d = nrm(ks[18], (N_S5_LAYERS, e), jnp.float32)
    s5_w_glu = e ** -0.5 * nrm(ks[19], (N_S5_LAYERS, e, e), jnp.float32)
    s5_b_glu = 0.01 * nrm(ks[20], (N_S5_LAYERS, e), jnp.float32)
    s5_w_out = DEEPNORM_BETA * e ** -0.5 * nrm(ks[21], (N_S5_LAYERS, e, d), jnp.float32)
    return {"x": x, "c": c, "ln_g": ln_g, "ln_b": ln_b, "ada_w": ada_w, "ada_b": ada_b,
            "gla_w_in": gla_w_in, "gla_gate_w2": gla_gate_w2, "gla_gate_b": gla_gate_b,
            "gla_norm_g": gla_norm_g, "gla_w_out": gla_w_out,
            "s5_w_in": s5_w_in, "s5_a_re": s5_a_re, "s5_a_im": s5_a_im, "s5_log_dt": s5_log_dt,
            "s5_b_re": s5_b_re, "s5_b_im": s5_b_im, "s5_c_re": s5_c_re, "s5_c_im": s5_c_im,
            "s5_d": s5_d, "s5_w_glu": s5_w_glu, "s5_b_glu": s5_b_glu, "s5_w_out": s5_w_out}


def reference(x, c, ln_g, ln_b, ada_w, ada_b, gla_w_in, gla_gate_w2, gla_gate_b, gla_norm_g, gla_w_out,
              s5_w_in, s5_a_re, s5_a_im, s5_log_dt, s5_b_re, s5_b_im, s5_c_re, s5_c_im, s5_d,
              s5_w_glu, s5_b_glu, s5_w_out):
    mod = jnp.einsum('bd,lde->lbe', jax.nn.silu(c), ada_w) + ada_b[:, None, :]
    for i in range(DEPTH):
        shift, scale, gate = jnp.split(mod[i], 3, axis=-1)
        u = x * (1.0 + scale[:, None, :]) + shift[:, None, :]
        j = i // N_MIXERS
        if i % N_MIXERS == 0:
            h = gla_mixer(u, gla_w_in[j], gla_gate_w2[j], gla_gate_b[j], gla_norm_g[j], gla_w_out[j])
        else:
            h = s5_mixer(u, s5_w_in[j], s5_a_re[j], s5_a_im[j], s5_log_dt[j], s5_b_re[j], s5_b_im[j],
                         s5_c_re[j], s5_c_im[j], s5_d[j], s5_w_glu[j], s5_b_glu[j], s5_w_out[j])
        x = layer_norm(DEEPNORM_ALPHA * x + (1.0 + gate[:, None, :]) * h, ln_g[i], ln_b[i])
    return x
```

```python
import functools
import math

import jax
import jax.numpy as jnp
from jax import lax
from jax.experimental import pallas as pl
from jax.experimental.pallas import tpu as pltpu

F32 = jnp.float32
BF16 = jnp.bfloat16
HIGHEST = lax.Precision.HIGHEST

CHUNK = 64
GLA_HEADS = 4
GLA_GATE_RANK = 16
GLA_TAU = 16.0
S5_GROUP = 16
S5_STATE = 64
LN_EPS = 1e-5
RMS_EPS = 1e-6

S5_T = 16
S5_OCT = 16
LANES = 128
SUBLANES = 8
V7X_SCOPED_VMEM_CAP = 60000 * 1024


def _cparams(semantics, vmem_bytes):
    limit = min(int(vmem_bytes) + (6 << 20), V7X_SCOPED_VMEM_CAP)
    return pltpu.CompilerParams(dimension_semantics=semantics, vmem_limit_bytes=limit)


def _sigmoid(x):
    return 1.0 / (1.0 + jnp.exp(-x))


def _silu(x):
    return x * _sigmoid(x)


def _adaln_kernel(c_ref, w_ref, b_ref, o_ref):
    c = c_ref[...]
    o_ref[...] = jnp.sum(_silu(c) * w_ref[...], axis=0, keepdims=True) + b_ref[...]


def _adaln(c, ada_w, ada_b, tn=1024):
    depth, d, n3 = ada_w.shape
    assert c.shape == (1, d), "batch 1 only"
    c_col = c.reshape(d, 1)
    out = pl.pallas_call(
        _adaln_kernel,
        grid=(depth, n3 // tn),
        in_specs=[
            pl.BlockSpec((d, 1), lambda l, j: (0, 0)),
            pl.BlockSpec((None, d, tn), lambda l, j: (l, 0, j)),
            pl.BlockSpec((None, 1, tn), lambda l, j: (l, 0, j)),
        ],
        out_specs=pl.BlockSpec((None, 1, tn), lambda l, j: (l, 0, j)),
        out_shape=jax.ShapeDtypeStruct((depth, 1, n3), F32),
        compiler_params=_cparams(("arbitrary", "arbitrary"), 2 * d * tn * 4 + d * LANES * 4),
        name="adaln_mod",
    )(c_col, ada_w, ada_b.reshape(depth, 1, n3))
    return out


def _proj_kernel(x_ref, sc_ref, sh_ref, w_ref, o_ref, u_scr):
    @pl.when(pl.program_id(1) == 0)
    def _():
        u_scr[...] = (x_ref[...] * (1.0 + sc_ref[...]) + sh_ref[...]).astype(BF16)

    o_ref[...] = jnp.dot(u_scr[...], w_ref[...], preferred_element_type=F32).astype(o_ref.dtype)


def _proj_gate_kernel(x_ref, sc_ref, sh_ref, w_ref, wg_ref, o_ref, g_ref, u_scr):
    @pl.when(pl.program_id(1) == 0)
    def _():
        u = (x_ref[...] * (1.0 + sc_ref[...]) + sh_ref[...]).astype(BF16)
        u_scr[...] = u
        g_ref[...] = jnp.dot(u, wg_ref[...], preferred_element_type=F32)

    o_ref[...] = jnp.dot(u_scr[...], w_ref[...], preferred_element_type=F32).astype(o_ref.dtype)


def _in_proj(x, scale, shift, w, wg=None, tm=1024, tn=1024):
    l, d = x.shape
    n = w.shape[1]
    tm = min(tm, l)
    grid = (l // tm, n // tn)
    in_specs = [
        pl.BlockSpec((tm, d), lambda i, j: (i, 0)),
        pl.BlockSpec((1, d), lambda i, j: (0, 0)),
        pl.BlockSpec((1, d), lambda i, j: (0, 0)),
        pl.BlockSpec((d, tn), lambda i, j: (0, j)),
    ]
    vmem = 2 * tm * d * 4 + 2 * d * tn * 2 + 2 * tm * tn * 2 + tm * d * 2
    if wg is None:
        return pl.pallas_call(
            _proj_kernel, grid=grid, in_specs=in_specs,
            out_specs=pl.BlockSpec((tm, tn), lambda i, j: (i, j)),
            out_shape=jax.ShapeDtypeStruct((l, n), BF16),
            scratch_shapes=[pltpu.VMEM((tm, d), BF16)],
            compiler_params=_cparams(("arbitrary", "arbitrary"), vmem),
            name="in_proj",
        )(x, scale, shift, w)
    ng = wg.shape[1]
    in_specs.append(pl.BlockSpec((d, ng), lambda i, j: (0, 0)))
    return pl.pallas_call(
        _proj_gate_kernel, grid=grid, in_specs=in_specs,
        out_specs=[pl.BlockSpec((tm, tn), lambda i, j: (i, j)),
                   pl.BlockSpec((tm, ng), lambda i, j: (i, 0))],
        out_shape=[jax.ShapeDtypeStruct((l, n), BF16), jax.ShapeDtypeStruct((l, ng), F32)],
        scratch_shapes=[pltpu.VMEM((tm, d), BF16)],
        compiler_params=_cparams(("arbitrary", "arbitrary"), vmem + 2 * d * ng * 2 + 2 * tm * ng * 4),
        name="in_proj_gate",
    )(x, scale, shift, w, wg)


def _gla_core_kernel(v_ref, z_ref, q_ref, k_ref, g_ref, w2_ref, gb_ref, ng_ref, o_ref, st_ref, *, dk, dv):
    @pl.when(pl.program_id(0) == 0)
    def _():
        st_ref[...] = jnp.zeros_like(st_ref)

    c = q_ref.shape[0]
    pre = jnp.dot(g_ref[...], w2_ref[...], precision=HIGHEST, preferred_element_type=F32) + gb_ref[...]
    la = (jnp.minimum(pre, 0.0) - jnp.log1p(jnp.exp(-jnp.abs(pre)))) * (1.0 / GLA_TAU)
    row = lax.broadcasted_iota(jnp.int32, (c, c), 0)
    col = lax.broadcasted_iota(jnp.int32, (c, c), 1)
    tri = (col <= row).astype(F32)
    cum = jnp.dot(tri, la, precision=HIGHEST, preferred_element_type=F32)
    tot = cum[c - 1:c, :]
    kdec = (k_ref[...].astype(F32) * jnp.exp(tot - cum)).astype(BF16)
    dec = jnp.exp(tot)
    for h in range(GLA_HEADS):
        ks = slice(h * dk, (h + 1) * dk)
        vs = slice(h * dv, (h + 1) * dv)
        upd = lax.dot_general(v_ref[:, vs], kdec[:, ks], (((0,), (0,)), ((), ())),
                              preferred_element_type=F32)
        st = dec[:, ks] * st_ref[h] + upd
        st_ref[h] = st
        o = lax.dot_general(q_ref[:, ks], st.astype(BF16), (((1,), (1,)), ((), ())),
                            preferred_element_type=F32) * (dk ** -0.5)
        o = o * lax.rsqrt(jnp.mean(o * o, axis=-1, keepdims=True) + RMS_EPS)
        y = o * ng_ref[:, vs] * _silu(z_ref[:, vs].astype(F32))
        o_ref[:, vs] = y.astype(o_ref.dtype)


def _gla_core(proj, glr, w2p, gate_b, norm_g, e, qk):
    l = proj.shape[0]
    dk, dv = qk // GLA_HEADS, e // GLA_HEADS
    ng = glr.shape[1]
    c = CHUNK
    kern = functools.partial(_gla_core_kernel, dk=dk, dv=dv)
    vmem = 2 * (2 * c * e * 2 + 2 * c * qk * 2 + c * ng * 4 + ng * qk * 4 + c * e * 2) + GLA_HEADS * dv * dk * 4
    return pl.pallas_call(
        kern,
        grid=(l // c,),
        in_specs=[
            pl.BlockSpec((c, e), lambda n: (n, 0)),
            pl.BlockSpec((c, e), lambda n: (n, 1)),
            pl.BlockSpec((c, qk), lambda n: (n, 2 * e // qk)),
            pl.BlockSpec((c, qk), lambda n: (n, 2 * e // qk + 1)),
            pl.BlockSpec((c, ng), lambda n: (n, 0)),
            pl.BlockSpec((ng, qk), lambda n: (0, 0)),
            pl.BlockSpec((1, qk), lambda n: (0, 0)),
            pl.BlockSpec((1, e), lambda n: (0, 0)),
        ],
        out_specs=pl.BlockSpec((c, e), lambda n: (n, 0)),
        out_shape=jax.ShapeDtypeStruct((l, e), BF16),
        scratch_shapes=[pltpu.VMEM((GLA_HEADS, dv, dk), F32)],
        compiler_params=_cparams(("arbitrary",), vmem + (8 << 20)),
        name="gla_core",
    )(proj, proj, proj, proj, glr, w2p, gate_b, norm_g)


def _out_ln_kernel(y_ref, w_ref, x_ref, gate_ref, g_ref, b_ref, o_ref, acc_ref, *, alpha):
    k = pl.program_id(1)

    @pl.when(k == 0)
    def _():
        acc_ref[...] = jnp.zeros_like(acc_ref)

    acc_ref[...] += jnp.dot(y_ref[...], w_ref[...], preferred_element_type=F32)

    @pl.when(k == pl.num_programs(1) - 1)
    def _():
        r = alpha * x_ref[...] + (1.0 + gate_ref[...]) * acc_ref[...]
        mu = jnp.mean(r, axis=-1, keepdims=True)
        rc = r - mu
        var = jnp.mean(rc * rc, axis=-1, keepdims=True)
        o_ref[...] = rc * lax.rsqrt(var + LN_EPS) * g_ref[...] + b_ref[...]


def _out_ln(y, w, x, gate, ln_g, ln_b, alpha, tm=512, tk=1024):
    l, e = y.shape
    d = w.shape[1]
    tm = min(tm, l)
    kern = functools.partial(_out_ln_kernel, alpha=alpha)
    vmem = 2 * tm * tk * 2 + 2 * tk * d * 2 + 4 * tm * d * 4 + tm * d * 4
    return pl.pallas_call(
        kern,
        grid=(l // tm, e // tk),
        in_specs=[
            pl.BlockSpec((tm, tk), lambda i, k: (i, k)),
            pl.BlockSpec((tk, d), lambda i, k: (k, 0)),
            pl.BlockSpec((tm, d), lambda i, k: (i, 0)),
            pl.BlockSpec((1, d), lambda i, k: (0, 0)),
            pl.BlockSpec((1, d), lambda i, k: (0, 0)),
            pl.BlockSpec((1, d), lambda i, k: (0, 0)),
        ],
        out_specs=pl.BlockSpec((tm, d), lambda i, k: (i, 0)),
        out_shape=jax.ShapeDtypeStruct((l, d), F32),
        scratch_shapes=[pltpu.VMEM((tm, d), F32)],
        compiler_params=_cparams(("arbitrary", "arbitrary"), vmem),
        name="out_proj_ln",
    )(y, w, x, gate, ln_g, ln_b)


def _ssm_kernel(a_ref, w_ref, p_ref, q_ref, lr_ref, li_ref, o_ref, vre, vim, sre, sim):
    nc = a_ref.shape[1]
    npair = S5_OCT // 2
    for p in range(npair):
        v = (jnp.dot(a_ref[2 * p], p_ref[2 * p], preferred_element_type=F32)
             + jnp.dot(a_ref[2 * p + 1], p_ref[2 * p + 1], preferred_element_type=F32))
        vre[pl.ds(p, nc, stride=npair), :] = v[:, :LANES]
        vim[pl.ds(p, nc, stride=npair), :] = v[:, LANES:]

    ar = lr_ref[...]
    ai = li_ref[...]

    def step(c, carry):
        xr, xi = carry
        rows = pl.ds(pl.multiple_of(c * npair, npair), npair)
        sre[rows, :] = xr
        sim[rows, :] = xi
        nxr = ar * xr - ai * xi + vre[rows, :]
        nxi = ar * xi + ai * xr + vim[rows, :]
        return nxr, nxi

    zero = jnp.zeros((npair, LANES), F32)
    lax.fori_loop(0, nc, step, (zero, zero), unroll=8)

    for p in range(npair):
        s = jnp.concatenate([sre[pl.ds(p, nc, stride=npair), :], sim[pl.ds(p, nc, stride=npair), :]],
                            axis=1).astype(BF16)
        for g in (2 * p, 2 * p + 1):
            o_ref[g] = (jnp.dot(a_ref[g], w_ref[g], preferred_element_type=F32)
                        + jnp.dot(s, q_ref[g], preferred_element_type=F32))


def _ssm_core(a, wt, pm, qm, lam_r, lam_i):
    g, nc, kk = a.shape
    noct = g // S5_OCT
    npair = S5_OCT // 2
    blk = lambda i: (i, 0, 0)
    vmem = 2 * (S5_OCT * nc * kk * 2 + 3 * S5_OCT * kk * kk * 2 + S5_OCT * nc * kk * 4) + 4 * nc * npair * LANES * 4
    return pl.pallas_call(
        _ssm_kernel,
        grid=(noct,),
        in_specs=[
            pl.BlockSpec((S5_OCT, nc, kk), blk),
            pl.BlockSpec((S5_OCT, kk, kk), blk),
            pl.BlockSpec((S5_OCT, kk, kk), blk),
            pl.BlockSpec((S5_OCT, kk, kk), blk),
            pl.BlockSpec((None, npair, LANES), blk),
            pl.BlockSpec((None, npair, LANES), blk),
        ],
        out_specs=pl.BlockSpec((S5_OCT, nc, kk), blk),
        out_shape=jax.ShapeDtypeStruct((g, nc, kk), F32),
        scratch_shapes=[pltpu.VMEM((nc * npair, LANES), F32) for _ in range(4)],
        compiler_params=_cparams(("arbitrary",), vmem),
        name="s5_ssm",
    )(a, wt, pm, qm, lam_r, lam_i)


def _s5_operators(a_re, a_im, log_dt, b_re, b_im, c_re, c_im):
    g, p = a_re.shape
    t = S5_T
    gi = S5_GROUP
    dt = jnp.exp(log_dt)[:, None]
    mag = jnp.exp(a_re * dt)
    lr = mag * jnp.cos(a_im * dt)
    li = mag * jnp.sin(a_im * dt)
    nr = lr - 1.0
    den = jnp.square(a_re) + jnp.square(a_im)
    coef_re = (nr * a_re + li * a_im) / den
    coef_im = (li * a_re - nr * a_im) / den
    bb_re = coef_re[..., None] * b_re - coef_im[..., None] * b_im
    bb_im = coef_re[..., None] * b_im + coef_im[..., None] * b_re
    pr, pi = [jnp.ones_like(lr)], [jnp.zeros_like(lr)]
    for _ in range(t):
        pr.append(pr[-1] * lr - pi[-1] * li)
        pi.append(pr[-2] * li + pi[-1] * lr)
    pw_r = jnp.stack(pr)
    pw_i = jnp.stack(pi)
    cl_r = c_re[None] * pw_r[:, :, None, :] - c_im[None] * pw_i[:, :, None, :]
    cl_i = c_re[None] * pw_i[:, :, None, :] + c_im[None] * pw_r[:, :, None, :]
    kern = (jnp.einsum('tgin,gnj->tgij', cl_r[:t], bb_re, precision=HIGHEST)
            - jnp.einsum('tgin,gnj->tgij', cl_i[:t], bb_im, precision=HIGHEST))
    s_in = jnp.arange(t)[:, None]
    s_out = jnp.arange(t)[None, :]
    lag = s_out - s_in
    kt = kern[jnp.clip(lag, 0, t - 1)] * (lag >= 0)[:, :, None, None, None].astype(F32)
    wt = kt.transpose(2, 0, 4, 1, 3).reshape(g, t * gi, t * gi)
    rev_r = pw_r[t - 1::-1][:t]
    rev_i = pw_i[t - 1::-1][:t]
    pb_r = rev_r[..., None] * bb_re[None] - rev_i[..., None] * bb_im[None]
    pb_i = rev_r[..., None] * bb_im[None] + rev_i[..., None] * bb_re[None]
    pb_r = pb_r.transpose(1, 0, 3, 2).reshape(g, t * gi, p)
    pb_i = pb_i.transpose(1, 0, 3, 2).reshape(g, t * gi, p)
    zc = jnp.zeros_like(pb_r)
    odd = (jnp.arange(g) % 2 == 1)[:, None, None]
    pm = jnp.where(odd, jnp.concatenate([zc, pb_r, zc, pb_i], axis=2),
                   jnp.concatenate([pb_r, zc, pb_i, zc], axis=2))
    q_r = cl_r[1:].transpose(1, 3, 0, 2).reshape(g, p, t * gi)
    q_i = -cl_i[1:].transpose(1, 3, 0, 2).reshape(g, p, t * gi)
    zr = jnp.zeros_like(q_r)
    qm = jnp.where(odd, jnp.concatenate([zr, q_r, zr, q_i], axis=1),
                   jnp.concatenate([q_r, zr, q_i, zr], axis=1))
    npair = S5_OCT // 2
    lam_r = pw_r[t].reshape(g // S5_OCT, npair, 2 * p)
    lam_i = pw_i[t].reshape(g // S5_OCT, npair, 2 * p)
    return wt.astype(BF16), pm.astype(BF16), qm.astype(BF16), lam_r, lam_i


def _gelu_kernel(y_ref, u_ref, d_ref, o_ref):
    y = y_ref[...] + d_ref[...] * u_ref[...].astype(F32)
    cdf = 0.5 * (1.0 + jnp.tanh(math.sqrt(2.0 / math.pi) * (y + 0.044715 * (y * y * y))))
    o_ref[...] = (y * cdf).astype(o_ref.dtype)


def _gelu_skip(yssm, uz, d_skip, tm=512):
    l, e = yssm.shape
    tm = min(tm, l)
    return pl.pallas_call(
        _gelu_kernel,
        grid=(l // tm,),
        in_specs=[pl.BlockSpec((tm, e), lambda i: (i, 0)),
                  pl.BlockSpec((tm, e), lambda i: (i, 0)),
                  pl.BlockSpec((1, e), lambda i: (0, 0))],
        out_specs=pl.BlockSpec((tm, e), lambda i: (i, 0)),
        out_shape=jax.ShapeDtypeStruct((l, e), BF16),
        compiler_params=_cparams(("arbitrary",), 2 * tm * e * (4 + 2 + 2)),
        name="s5_gelu",
    )(yssm, uz, d_skip)


def _glu_kernel(ya_ref, w_ref, b_ref, yc_ref, z_ref, o_ref):
    acc = jnp.dot(ya_ref[...], w_ref[...], preferred_element_type=F32) + b_ref[...]
    o_ref[...] = (yc_ref[...].astype(F32) * _sigmoid(acc) * _silu(z_ref[...].astype(F32))).astype(o_ref.dtype)


def _glu(yact, w_glu, b_glu, uz, tm=1024, tn=512):
    l, e = yact.shape
    tm = min(tm, l)
    zoff = e // tn
    vmem = 2 * (tm * e * 2 + e * tn * 2 + 3 * tm * tn * 2)
    return pl.pallas_call(
        _glu_kernel,
        grid=(l // tm, e // tn),
        in_specs=[pl.BlockSpec((tm, e), lambda i, j: (i, 0)),
                  pl.BlockSpec((e, tn), lambda i, j: (0, j)),
                  pl.BlockSpec((1, tn), lambda i, j: (0, j)),
                  pl.BlockSpec((tm, tn), lambda i, j: (i, j)),
                  pl.BlockSpec((tm, tn), lambda i, j: (i, zoff + j))],
        out_specs=pl.BlockSpec((tm, tn), lambda i, j: (i, j)),
        out_shape=jax.ShapeDtypeStruct((l, e), BF16),
        compiler_params=_cparams(("arbitrary", "arbitrary"), vmem),
        name="s5_glu",
    )(yact, w_glu, b_glu, yact, uz)


def _gla_layer(x, scale, shift, gate, ln_g, ln_b, alpha, w_in, gate_w2, gate_b, norm_g, w_out):
    d = x.shape[1]
    e = w_out.shape[0]
    qk = gate_w2.shape[1]
    w_main = jnp.concatenate([w_in[:, 2 * qk:2 * qk + 2 * e], w_in[:, :2 * qk]], axis=1).astype(BF16)
    wg = jnp.pad(w_in[:, 2 * qk + 2 * e:], ((0, 0), (0, LANES - GLA_GATE_RANK))).astype(BF16)
    w2p = jnp.pad(gate_w2, ((0, LANES - GLA_GATE_RANK), (0, 0)))
    proj, glr = _in_proj(x, scale, shift, w_main, wg)
    y = _gla_core(proj, glr, w2p, gate_b.reshape(1, qk), norm_g.reshape(1, e), e, qk)
    return _out_ln(y, w_out.astype(BF16), x, gate, ln_g.reshape(1, d), ln_b.reshape(1, d), alpha)


def _s5_layer(x, scale, shift, gate, ln_g, ln_b, alpha, w_in, a_re, a_im, log_dt, b_re, b_im, c_re, c_im,
              d_skip, w_glu, b_glu, w_out):
    l, d = x.shape
    e = w_out.shape[0]
    g = a_re.shape[0]
    t = S5_T
    uz = _in_proj(x, scale, shift, w_in.astype(BF16))
    wt, pm, qm, lam_r, lam_i = _s5_operators(a_re, a_im, log_dt, b_re, b_im, c_re, c_im)
    a = uz[:, :e].reshape(l // t, t, g, S5_GROUP).transpose(2, 0, 1, 3).reshape(g, l // t, t * S5_GROUP)
    yg = _ssm_core(a, wt, pm, qm, lam_r, lam_i)
    yssm = yg.reshape(g, l // t, t, S5_GROUP).transpose(1, 2, 0, 3).reshape(l, e)
    yact = _gelu_skip(yssm, uz, d_skip.reshape(1, e))
    yglu = _glu(yact, w_glu.astype(BF16), b_glu.reshape(1, e), uz)
    return _out_ln(yglu, w_out.astype(BF16), x, gate, ln_g.reshape(1, d), ln_b.reshape(1, d), alpha)


def kernel(x, c, ln_g, ln_b, ada_w, ada_b, gla_w_in, gla_gate_w2, gla_gate_b, gla_norm_g, gla_w_out,
           s5_w_in, s5_a_re, s5_a_im, s5_log_dt, s5_b_re, s5_b_im, s5_c_re, s5_c_im, s5_d,
           s5_w_glu, s5_b_glu, s5_w_out):
    bsz, l, d = x.shape
    assert bsz == 1, "batch 1 only"
    depth = ln_g.shape[0]
    alpha = (2 * depth) ** 0.25
    mod = _adaln(c, ada_w, ada_b)
    h = x.reshape(l, d)
    for i in range(depth):
        shift, scale, gate = mod[i, :, :d], mod[i, :, d:2 * d], mod[i, :, 2 * d:]
        j = i // 2
        if i % 2 == 0:
            h = _gla_layer(h, scale, shift, gate, ln_g[i], ln_b[i], alpha, gla_w_in[j], gla_gate_w2[j],
                           gla_gate_b[j], gla_norm_g[j], gla_w_out[j])
        else:
            h = _s5_layer(h, scale, shift, gate, ln_g[i], ln_b[i], alpha, s5_w_in[j], s5_a_re[j], s5_a_im[j],
                          s5_log_dt[j], s5_b_re[j], s5_b_im[j], s5_c_re[j], s5_c_im[j], s5_d[j],
                          s5_w_glu[j], s5_b_glu[j], s5_w_out[j])
    return h.reshape(bsz, l, d)
```

```python
import functools
import math

import jax
import jax.numpy as jnp
from jax import lax
from jax.experimental import pallas as pl
from jax.experimental.pallas import tpu as pltpu

F32 = jnp.float32
BF16 = jnp.bfloat16
U32 = jnp.uint32
HIGHEST = lax.Precision.HIGHEST

CHUNK = 64
GLA_HEADS = 4
GLA_GATE_RANK = 16
GLA_TAU = 16.0
S5_GROUP = 16
S5_STATE = 64
LN_EPS = 1e-5
RMS_EPS = 1e-6

S5_T = 16
S5_OCT = 16
LANES = 128
V7X_SCOPED_VMEM_CAP = 60000 * 1024


def _cparams(semantics, vmem_bytes):
    limit = min(int(vmem_bytes) + (6 << 20), V7X_SCOPED_VMEM_CAP)
    return pltpu.CompilerParams(dimension_semantics=semantics, vmem_limit_bytes=limit)


def _sigmoid(x):
    return 1.0 / (1.0 + jnp.exp(-x))


def _silu(x):
    return x * _sigmoid(x)


def _gelu_tanh(y):
    cdf = 0.5 * (1.0 + jnp.tanh(math.sqrt(2.0 / math.pi) * (y + 0.044715 * (y * y * y))))
    return y * cdf


def _adaln_kernel(c_ref, w_ref, b_ref, o_ref):
    c = c_ref[...]
    o_ref[...] = jnp.sum(_silu(c) * w_ref[...], axis=0, keepdims=True) + b_ref[...]


def _adaln(c, ada_w, ada_b, tn=1024):
    depth, d, n3 = ada_w.shape
    assert c.shape == (1, d), "batch 1 only"
    c_col = c.reshape(d, 1)
    out = pl.pallas_call(
        _adaln_kernel,
        grid=(depth, n3 // tn),
        in_specs=[
            pl.BlockSpec((d, 1), lambda l, j: (0, 0)),
            pl.BlockSpec((None, d, tn), lambda l, j: (l, 0, j)),
            pl.BlockSpec((None, 1, tn), lambda l, j: (l, 0, j)),
        ],
        out_specs=pl.BlockSpec((None, 1, tn), lambda l, j: (l, 0, j)),
        out_shape=jax.ShapeDtypeStruct((depth, 1, n3), F32),
        compiler_params=_cparams(("arbitrary", "arbitrary"), 2 * d * tn * 4 + d * LANES * 4),
        name="adaln_mod",
    )(c_col, ada_w, ada_b.reshape(depth, 1, n3))
    return out


def _modulate(x_ref, sc_ref, sh_ref):
    return (x_ref[...] * (1.0 + sc_ref[...]) + sh_ref[...]).astype(BF16)


def _proj_gate_kernel(x_ref, sc_ref, sh_ref, w_ref, wg_ref, o_ref, g_ref, u_scr):
    @pl.when(pl.program_id(1) == 0)
    def _():
        u = _modulate(x_ref, sc_ref, sh_ref)
        u_scr[...] = u
        g_ref[...] = jnp.dot(u, wg_ref[...].astype(BF16), preferred_element_type=F32)

    o_ref[...] = jnp.dot(u_scr[...], w_ref[...].astype(BF16), preferred_element_type=F32).astype(o_ref.dtype)


def _proj_perm_kernel(xa_ref, xb_ref, sc_ref, sh_ref, w_ref, o_ref, u_scr):
    nc = xa_ref.shape[0]

    @pl.when(pl.program_id(1) == 0)
    def _():
        u_scr[:nc] = _modulate(xa_ref, sc_ref, sh_ref)
        u_scr[nc:] = _modulate(xb_ref, sc_ref, sh_ref)

    o_ref[...] = jnp.dot(u_scr[...], w_ref[...].astype(BF16), preferred_element_type=F32).astype(o_ref.dtype)


def _in_proj_gate(x, scale, shift, w, layer, wg, tm=1024, tn=1024):
    l, d = x.shape
    n = (w.shape[2] // tn) * tn
    ng = wg.shape[1]
    tm = min(tm, l)
    vmem = 2 * tm * d * 4 + 2 * d * tn * 4 + 2 * tm * tn * 2 + tm * d * 2 + 2 * d * ng * 4 + 2 * tm * ng * 4
    return pl.pallas_call(
        _proj_gate_kernel,
        grid=(l // tm, n // tn),
        in_specs=[
            pl.BlockSpec((tm, d), lambda i, j: (i, 0)),
            pl.BlockSpec((1, d), lambda i, j: (0, 0)),
            pl.BlockSpec((1, d), lambda i, j: (0, 0)),
            pl.BlockSpec((None, d, tn), lambda i, j: (layer, 0, j)),
            pl.BlockSpec((d, ng), lambda i, j: (0, 0)),
        ],
        out_specs=[pl.BlockSpec((tm, tn), lambda i, j: (i, j)),
                   pl.BlockSpec((tm, ng), lambda i, j: (i, 0))],
        out_shape=[jax.ShapeDtypeStruct((l, n), BF16), jax.ShapeDtypeStruct((l, ng), F32)],
        scratch_shapes=[pltpu.VMEM((tm, d), BF16)],
        compiler_params=_cparams(("arbitrary", "arbitrary"), vmem),
        name="in_proj_gate",
    )(x, scale, shift, w, wg)


def _in_proj_perm(x2, t, scale, shift, w, layer, tn=1024):
    nc = x2.shape[0]
    d = x2.shape[1] // t
    n = w.shape[2]
    tm = 2 * nc
    vmem = 2 * tm * d * 4 + 2 * d * tn * 4 + 2 * tm * tn * 2 + tm * d * 2
    return pl.pallas_call(
        _proj_perm_kernel,
        grid=(t // 2, n // tn),
        in_specs=[
            pl.BlockSpec((nc, d), lambda i, j: (0, 2 * i)),
            pl.BlockSpec((nc, d), lambda i, j: (0, 2 * i + 1)),
            pl.BlockSpec((1, d), lambda i, j: (0, 0)),
            pl.BlockSpec((1, d), lambda i, j: (0, 0)),
            pl.BlockSpec((None, d, tn), lambda i, j: (layer, 0, j)),
        ],
        out_specs=pl.BlockSpec((tm, tn), lambda i, j: (i, j)),
        out_shape=jax.ShapeDtypeStruct((nc * t, n), BF16),
        scratch_shapes=[pltpu.VMEM((tm, d), BF16)],
        compiler_params=_cparams(("arbitrary", "arbitrary"), vmem),
        name="in_proj_perm",
    )(x2, x2, scale, shift, w)


def _gla_core_kernel(*refs, dk, dv):
    nh = GLA_HEADS
    q_ref, k_ref = refs[0], refs[1]
    v_refs = refs[2:2 + nh]
    z_refs = refs[2 + nh:2 + 2 * nh]
    g_ref, w2_ref, gb_ref, ng_ref, o_ref, st_ref = refs[2 + 2 * nh:]

    @pl.when(pl.program_id(0) == 0)
    def _():
        st_ref[...] = jnp.zeros_like(st_ref)

    c = q_ref.shape[0]
    pre = jnp.dot(g_ref[...], w2_ref[...], precision=HIGHEST, preferred_element_type=F32) + gb_ref[...]
    la = (jnp.minimum(pre, 0.0) - jnp.log1p(jnp.exp(-jnp.abs(pre)))) * (1.0 / GLA_TAU)
    row = lax.broadcasted_iota(jnp.int32, (c, c), 0)
    col = lax.broadcasted_iota(jnp.int32, (c, c), 1)
    tri = (col <= row).astype(F32)
    cum = jnp.dot(tri, la, precision=HIGHEST, preferred_element_type=F32)
    tot = cum[c - 1:c, :]
    kdec = (k_ref[...].astype(F32) * jnp.exp(tot - cum)).astype(BF16)
    dec = jnp.exp(tot)
    for h in range(nh):
        ks = slice(h * dk, (h + 1) * dk)
        vs = slice(h * dv, (h + 1) * dv)
        upd = lax.dot_general(v_refs[h][...], kdec[:, ks], (((0,), (0,)), ((), ())),
                              preferred_element_type=F32)
        st = dec[:, ks] * st_ref[h] + upd
        st_ref[h] = st
        o = lax.dot_general(q_ref[:, ks], st.astype(BF16), (((1,), (1,)), ((), ())),
                            preferred_element_type=F32) * (dk ** -0.5)
        o = o * lax.rsqrt(jnp.mean(o * o, axis=-1, keepdims=True) + RMS_EPS)
        y = o * ng_ref[:, vs] * _silu(z_refs[h][...].astype(F32))
        o_ref[:, vs] = y.astype(o_ref.dtype)


def _gla_core(proj, glr, w2p, gate_b, norm_g, e, qk):
    l = proj.shape[0]
    nh = GLA_HEADS
    dk, dv = qk // nh, e // nh
    assert (2 * qk) % dv == 0
    v0 = 2 * qk // dv
    ng = glr.shape[1]
    c = CHUNK
    kern = functools.partial(_gla_core_kernel, dk=dk, dv=dv)
    vmem = 2 * (2 * c * e * 2 + 2 * c * qk * 2 + c * ng * 4 + ng * qk * 4 + c * e * 2) + nh * dv * dk * 4
    head_specs = [pl.BlockSpec((c, dv), functools.partial(lambda n, b: (n, b), b=v0 + h)) for h in range(2 * nh)]
    return pl.pallas_call(
        kern,
        grid=(l // c,),
        in_specs=[
            pl.BlockSpec((c, qk), lambda n: (n, 0)),
            pl.BlockSpec((c, qk), lambda n: (n, 1)),
            *head_specs,
            pl.BlockSpec((c, ng), lambda n: (n, 0)),
            pl.BlockSpec((ng, qk), lambda n: (0, 0)),
            pl.BlockSpec((1, qk), lambda n: (0, 0)),
            pl.BlockSpec((1, e), lambda n: (0, 0)),
        ],
        out_specs=pl.BlockSpec((c, e), lambda n: (n, 0)),
        out_shape=jax.ShapeDtypeStruct((l, e), BF16),
        scratch_shapes=[pltpu.VMEM((nh, dv, dk), F32)],
        compiler_params=_cparams(("arbitrary",), vmem + (8 << 20)),
        name="gla_core",
    )(*([proj] * (2 + 2 * nh)), glr, w2p, gate_b, norm_g)


def _out_ln_kernel(y_ref, w_ref, x_ref, gate_ref, g_ref, b_ref, o_ref, acc_ref, *, alpha):
    k = pl.program_id(1)

    @pl.when(k == 0)
    def _():
        acc_ref[...] = jnp.zeros_like(acc_ref)

    acc_ref[...] += jnp.dot(y_ref[...], w_ref[...], preferred_element_type=F32)

    @pl.when(k == pl.num_programs(1) - 1)
    def _():
        r = alpha * x_ref[...] + (1.0 + gate_ref[...]) * acc_ref[...]
        mu = jnp.mean(r, axis=-1, keepdims=True)
        rc = r - mu
        var = jnp.mean(rc * rc, axis=-1, keepdims=True)
        o_ref[...] = rc * lax.rsqrt(var + LN_EPS) * g_ref[...] + b_ref[...]


def _out_ln(y, w, x, gate, ln_g, ln_b, alpha, step_major_t=None, tm=512, tk=1024):
    l, e = y.shape
    d = w.shape[1]
    kern = functools.partial(_out_ln_kernel, alpha=alpha)
    if step_major_t is None:
        tm = min(tm, l)
        x_in, out_shape = x, (l, d)
        x_spec = pl.BlockSpec((tm, d), lambda i, k: (i, 0))
    else:
        t = step_major_t
        tm = l // t
        x_in, out_shape = x.reshape(tm, t * d), (tm, t * d)
        x_spec = pl.BlockSpec((tm, d), lambda i, k: (0, i))
    vmem = 2 * tm * tk * 2 + 2 * tk * d * 2 + 4 * tm * d * 4 + tm * d * 4
    out = pl.pallas_call(
        kern,
        grid=(l // tm, e // tk),
        in_specs=[
            pl.BlockSpec((tm, tk), lambda i, k: (i, k)),
            pl.BlockSpec((tk, d), lambda i, k: (k, 0)),
            x_spec,
            pl.BlockSpec((1, d), lambda i, k: (0, 0)),
            pl.BlockSpec((1, d), lambda i, k: (0, 0)),
            pl.BlockSpec((1, d), lambda i, k: (0, 0)),
        ],
        out_specs=x_spec,
        out_shape=jax.ShapeDtypeStruct(out_shape, F32),
        scratch_shapes=[pltpu.VMEM((tm, d), F32)],
        compiler_params=_cparams(("arbitrary", "arbitrary"), vmem),
        name="out_proj_ln",
    )(y, w, x_in, gate, ln_g, ln_b)
    return out.reshape(l, d)


def _lane_block_transpose8(cols):
    blk = lax.shift_right_logical(lax.broadcasted_iota(jnp.int32, cols[0].shape, 1), 4)
    for b in range(3):
        dist = 1 << b
        sel = (blk & dist) != 0
        new = list(cols)
        for j in range(8):
            if not j & dist:
                lo, hi = cols[j], cols[j + dist]
                new[j] = jnp.where(sel, pltpu.roll(hi, S5_GROUP * dist, 1), lo)
                new[j + dist] = jnp.where(sel, hi, pltpu.roll(lo, LANES - S5_GROUP * dist, 1))
        cols = new
    return cols


def _ssm_kernel(x_ref, w_ref, p_ref, q_ref, lr_ref, li_ref, d_ref, o_ref, a_scr, y_scr, vre, vim, sre, sim):
    nc = x_ref.shape[1]
    npair = S5_OCT // 2

    rb = min(64, nc)

    def gather_steps(it, carry):
        r = pl.ds(pl.multiple_of(it * rb, rb), rb)
        for s_hi in range(2):
            for g_hi in range(2):
                cols = [pltpu.bitcast(x_ref[8 * s_hi + m, r, g_hi * LANES:(g_hi + 1) * LANES], U32)
                        for m in range(8)]
                res = _lane_block_transpose8(cols)
                for m in range(8):
                    a_scr[8 * g_hi + m, r, s_hi * LANES:(s_hi + 1) * LANES] = pltpu.bitcast(res[m], BF16)
        return carry

    lax.fori_loop(0, nc // rb, gather_steps, 0)
    a_bf = a_scr

    for p in range(npair):
        v = (jnp.dot(a_bf[2 * p], p_ref[2 * p], preferred_element_type=F32)
             + jnp.dot(a_bf[2 * p + 1], p_ref[2 * p + 1], preferred_element_type=F32))
        vre[pl.ds(p, nc, stride=npair), :] = v[:, :LANES]
        vim[pl.ds(p, nc, stride=npair), :] = v[:, LANES:]

    ar = lr_ref[...]
    ai = li_ref[...]

    def step(c, carry):
        xr, xi = carry
        rows = pl.ds(pl.multiple_of(c * npair, npair), npair)
        sre[rows, :] = xr
        sim[rows, :] = xi
        nxr = ar * xr - ai * xi + vre[rows, :]
        nxi = ar * xi + ai * xr + vim[rows, :]
        return nxr, nxi

    zero = jnp.zeros((npair, LANES), F32)
    lax.fori_loop(0, nc, step, (zero, zero), unroll=8)

    for p in range(npair):
        s = jnp.concatenate([sre[pl.ds(p, nc, stride=npair), :], sim[pl.ds(p, nc, stride=npair), :]],
                            axis=1).astype(BF16)
        for g in (2 * p, 2 * p + 1):
            y_scr[g] = (jnp.dot(a_bf[g], w_ref[g], preferred_element_type=F32)
                        + jnp.dot(s, q_ref[g], preferred_element_type=F32))

    rbo = min(32, nc)

    def scatter_steps(it, carry):
        r = pl.ds(pl.multiple_of(it * rbo, rbo), rbo)
        for s_hi in range(2):
            for g_hi in range(2):
                cs = slice(g_hi * LANES, (g_hi + 1) * LANES)
                cols = [y_scr[8 * g_hi + m, r, s_hi * LANES:(s_hi + 1) * LANES] for m in range(8)]
                res = _lane_block_transpose8(cols)
                dsk = d_ref[:, cs]
                for m in range(8):
                    s = 8 * s_hi + m
                    y = res[m] + dsk * x_ref[s, r, cs].astype(F32)
                    o_ref[s, r, cs] = _gelu_tanh(y).astype(o_ref.dtype)
        return carry

    lax.fori_loop(0, nc // rbo, scatter_steps, 0)


def _ssm_core(uz, wt, pm, qm, lam_r, lam_i, d_skip, e):
    t = S5_T
    nc = uz.shape[0] // t
    kk = t * S5_GROUP
    assert kk == 2 * LANES and S5_OCT * S5_GROUP == kk
    noct = e // kk
    npair = S5_OCT // 2
    x3 = uz.reshape(t, nc, uz.shape[1])
    vmem = (2 * (2 * t * nc * kk * 2 + 3 * S5_OCT * kk * kk * 2) + S5_OCT * nc * kk * (2 + 4)
            + 4 * nc * npair * LANES * 4)
    out = pl.pallas_call(
        _ssm_kernel,
        grid=(noct,),
        in_specs=[
            pl.BlockSpec((t, nc, kk), lambda i: (0, 0, i)),
            pl.BlockSpec((S5_OCT, kk, kk), lambda i: (i, 0, 0)),
            pl.BlockSpec((S5_OCT, kk, kk), lambda i: (i, 0, 0)),
            pl.BlockSpec((S5_OCT, kk, kk), lambda i: (i, 0, 0)),
            pl.BlockSpec((None, npair, LANES), lambda i: (i, 0, 0)),
            pl.BlockSpec((None, npair, LANES), lambda i: (i, 0, 0)),
            pl.BlockSpec((1, kk), lambda i: (0, i)),
        ],
        out_specs=pl.BlockSpec((t, nc, kk), lambda i: (0, 0, i)),
        out_shape=jax.ShapeDtypeStruct((t, nc, e), BF16),
        scratch_shapes=[pltpu.VMEM((S5_OCT, nc, kk), BF16), pltpu.VMEM((S5_OCT, nc, kk), F32)]
        + [pltpu.VMEM((nc * npair, LANES), F32) for _ in range(4)],
        compiler_params=_cparams(("arbitrary",), vmem),
        name="s5_ssm",
    )(x3, wt, pm, qm, lam_r, lam_i, d_skip)
    return out.reshape(t * nc, e)


def _s5_operators(a_re, a_im, log_dt, b_re, b_im, c_re, c_im):
    g, p = a_re.shape
    t = S5_T
    gi = S5_GROUP
    dt = jnp.exp(log_dt)[:, None]
    mag = jnp.exp(a_re * dt)
    lr = mag * jnp.cos(a_im * dt)
    li = mag * jnp.sin(a_im * dt)
    nr = lr - 1.0
    den = jnp.square(a_re) + jnp.square(a_im)
    coef_re = (nr * a_re + li * a_im) / den
    coef_im = (li * a_re - nr * a_im) / den
    bb_re = coef_re[..., None] * b_re - coef_im[..., None] * b_im
    bb_im = coef_re[..., None] * b_im + coef_im[..., None] * b_re
    pr, pi = [jnp.ones_like(lr)], [jnp.zeros_like(lr)]
    for _ in range(t):
        pr.append(pr[-1] * lr - pi[-1] * li)
        pi.append(pr[-2] * li + pi[-1] * lr)
    pw_r = jnp.stack(pr)
    pw_i = jnp.stack(pi)
    cl_r = c_re[None] * pw_r[:, :, None, :] - c_im[None] * pw_i[:, :, None, :]
    cl_i = c_re[None] * pw_i[:, :, None, :] + c_im[None] * pw_r[:, :, None, :]
    kern = (jnp.einsum('tgin,gnj->tgij', cl_r[:t], bb_re, precision=HIGHEST)
            - jnp.einsum('tgin,gnj->tgij', cl_i[:t], bb_im, precision=HIGHEST))
    s_in = jnp.arange(t)[:, None]
    s_out = jnp.arange(t)[None, :]
    lag = s_out - s_in
    kt = kern[jnp.clip(lag, 0, t - 1)] * (lag >= 0)[:, :, None, None, None].astype(F32)
    wt = kt.transpose(2, 0, 4, 1, 3).reshape(g, t * gi, t * gi)
    rev_r = pw_r[t - 1::-1][:t]
    rev_i = pw_i[t - 1::-1][:t]
    pb_r = rev_r[..., None] * bb_re[None] - rev_i[..., None] * bb_im[None]
    pb_i = rev_r[..., None] * bb_im[None] + rev_i[..., None] * bb_re[None]
    pb_r = pb_r.transpose(1, 0, 3, 2).reshape(g, t * gi, p)
    pb_i = pb_i.transpose(1, 0, 3, 2).reshape(g, t * gi, p)
    zc = jnp.zeros_like(pb_r)
    odd = (jnp.arange(g) % 2 == 1)[:, None, None]
    pm = jnp.where(odd, jnp.concatenate([zc, pb_r, zc, pb_i], axis=2),
                   jnp.concatenate([pb_r, zc, pb_i, zc], axis=2))
    q_r = cl_r[1:].transpose(1, 3, 0, 2).reshape(g, p, t * gi)
    q_i = -cl_i[1:].transpose(1, 3, 0, 2).reshape(g, p, t * gi)
    zr = jnp.zeros_like(q_r)
    qm = jnp.where(odd, jnp.concatenate([zr, q_r, zr, q_i], axis=1),
                   jnp.concatenate([q_r, zr, q_i, zr], axis=1))
    npair = S5_OCT // 2
    lam_r = pw_r[t].reshape(g // S5_OCT, npair, 2 * p)
    lam_i = pw_i[t].reshape(g // S5_OCT, npair, 2 * p)
    return wt.astype(BF16), pm.astype(BF16), qm.astype(BF16), lam_r, lam_i


def _glu_kernel(ya_ref, w_ref, b_ref, yc_ref, z_ref, o_ref):
    acc = jnp.dot(ya_ref[...], w_ref[...].astype(BF16), preferred_element_type=F32) + b_ref[...]
    o_ref[...] = (yc_ref[...].astype(F32) * _sigmoid(acc) * _silu(z_ref[...].astype(F32))).astype(o_ref.dtype)


def _glu(yact, w_glu, layer, b_glu, uz, tm=1024, tn=512):
    l, e = yact.shape
    tm = min(tm, l)
    zoff = e // tn
    vmem = 2 * (tm * e * 2 + e * tn * 4 + 3 * tm * tn * 2)
    return pl.pallas_call(
        _glu_kernel,
        grid=(l // tm, e // tn),
        in_specs=[pl.BlockSpec((tm, e), lambda i, j: (i, 0)),
                  pl.BlockSpec((None, e, tn), lambda i, j: (layer, 0, j)),
                  pl.BlockSpec((1, tn), lambda i, j: (0, j)),
                  pl.BlockSpec((tm, tn), lambda i, j: (i, j)),
                  pl.BlockSpec((tm, tn), lambda i, j: (i, zoff + j))],
        out_specs=pl.BlockSpec((tm, tn), lambda i, j: (i, j)),
        out_shape=jax.ShapeDtypeStruct((l, e), BF16),
        compiler_params=_cparams(("arbitrary", "arbitrary"), vmem),
        name="s5_glu",
    )(yact, w_glu, b_glu, yact, uz)


def _gla_layer(x, scale, shift, gate, ln_g, ln_b, alpha, w_in, layer, gate_w2, gate_b, norm_g, w_out):
    d = x.shape[1]
    e = w_out.shape[0]
    qk = gate_w2.shape[1]
    wg = jnp.pad(w_in[layer, :, 2 * qk + 2 * e:], ((0, 0), (0, LANES - GLA_GATE_RANK)))
    w2p = jnp.pad(gate_w2, ((0, LANES - GLA_GATE_RANK), (0, 0)))
    proj, glr = _in_proj_gate(x, scale, shift, w_in, layer, wg)
    y = _gla_core(proj, glr, w2p, gate_b.reshape(1, qk), norm_g.reshape(1, e), e, qk)
    return _out_ln(y, w_out.astype(BF16), x, gate, ln_g.reshape(1, d), ln_b.reshape(1, d), alpha)


def _s5_layer(x, scale, shift, gate, ln_g, ln_b, alpha, w_in, layer, a_re, a_im, log_dt, b_re, b_im, c_re, c_im,
              d_skip, w_glu, b_glu, w_out):
    l, d = x.shape
    e = w_out.shape[0]
    t = S5_T
    uz = _in_proj_perm(x.reshape(l // t, t * d), t, scale, shift, w_in, layer)
    wt, pm, qm, lam_r, lam_i = _s5_operators(a_re, a_im, log_dt, b_re, b_im, c_re, c_im)
    yact = _ssm_core(uz, wt, pm, qm, lam_r, lam_i, d_skip.reshape(1, e), e)
    yglu = _glu(yact, w_glu, layer, b_glu.reshape(1, e), uz)
    return _out_ln(yglu, w_out.astype(BF16), x, gate, ln_g.reshape(1, d), ln_b.reshape(1, d), alpha,
                   step_major_t=t)


def kernel(x, c, ln_g, ln_b, ada_w, ada_b, gla_w_in, gla_gate_w2, gla_gate_b, gla_norm_g, gla_w_out,
           s5_w_in, s5_a_re, s5_a_im, s5_log_dt, s5_b_re, s5_b_im, s5_c_re, s5_c_im, s5_d,
           s5_w_glu, s5_b_glu, s5_w_out):
    bsz, l, d = x.shape
    assert bsz == 1, "batch 1 only"
    depth = ln_g.shape[0]
    alpha = (2 * depth) ** 0.25
    mod = _adaln(c, ada_w, ada_b)
    h = x.reshape(l, d)
    for i in range(depth):
        shift, scale, gate = mod[i, :, :d], mod[i, :, d:2 * d], mod[i, :, 2 * d:]
        j = i // 2
        if i % 2 == 0:
            h = _gla_layer(h, scale, shift, gate, ln_g[i], ln_b[i], alpha, gla_w_in, j, gla_gate_w2[j],
                           gla_gate_b[j], gla_norm_g[j], gla_w_out[j])
        else:
            h = _s5_layer(h, scale, shift, gate, ln_g[i], ln_b[i], alpha, s5_w_in, j, s5_a_re[j], s5_a_im[j],
                          s5_log_dt[j], s5_b_re[j], s5_b_im[j], s5_c_re[j], s5_c_im[j], s5_d[j],
                          s5_w_glu, s5_b_glu[j], s5_w_out[j])
    return h.reshape(bsz, l, d)
```

```python
import functools
import math

import jax
import jax.numpy as jnp
from jax import lax
from jax.experimental import pallas as pl
from jax.experimental.pallas import tpu as pltpu

F32 = jnp.float32
BF16 = jnp.bfloat16
U32 = jnp.uint32
HIGHEST = lax.Precision.HIGHEST

CHUNK = 64
GLA_HEADS = 4
GLA_GATE_RANK = 16
GLA_TAU = 16.0
S5_GROUP = 16
S5_STATE = 64
LN_EPS = 1e-5
RMS_EPS = 1e-6

S5_T = 16
S5_OCT = 16
LANES = 128
V7X_SCOPED_VMEM_CAP = 60000 * 1024


def _cparams(semantics, vmem_bytes):
    limit = min(int(vmem_bytes) + (6 << 20), V7X_SCOPED_VMEM_CAP)
    return pltpu.CompilerParams(dimension_semantics=semantics, vmem_limit_bytes=limit)


def _sigmoid(x):
    return 1.0 / (1.0 + jnp.exp(-x))


def _silu(x):
    return x * _sigmoid(x)


def _gelu_tanh(y):
    cdf = 0.5 * (1.0 + jnp.tanh(math.sqrt(2.0 / math.pi) * (y + 0.044715 * (y * y * y))))
    return y * cdf


def _adaln_kernel(c_ref, w_ref, b_ref, o_ref):
    c = c_ref[...]
    o_ref[...] = jnp.sum(_silu(c) * w_ref[...], axis=0, keepdims=True) + b_ref[...]


def _adaln(c, ada_w, ada_b, tn=1024):
    depth, d, n3 = ada_w.shape
    assert c.shape == (1, d), "batch 1 only"
    c_col = c.reshape(d, 1)
    out = pl.pallas_call(
        _adaln_kernel,
        grid=(depth, n3 // tn),
        in_specs=[
            pl.BlockSpec((d, 1), lambda l, j: (0, 0)),
            pl.BlockSpec((None, d, tn), lambda l, j: (l, 0, j)),
            pl.BlockSpec((None, 1, tn), lambda l, j: (l, 0, j)),
        ],
        out_specs=pl.BlockSpec((None, 1, tn), lambda l, j: (l, 0, j)),
        out_shape=jax.ShapeDtypeStruct((depth, 1, n3), F32),
        compiler_params=_cparams(("arbitrary", "arbitrary"), 2 * d * tn * 4 + d * LANES * 4),
        name="adaln_mod",
    )(c_col, ada_w, ada_b.reshape(depth, 1, n3))
    return out


def _modulate(x_ref, sc_ref, sh_ref):
    return (x_ref[...] * (1.0 + sc_ref[...]) + sh_ref[...]).astype(BF16)


def _proj_gate_kernel(x_ref, sc_ref, sh_ref, w_ref, wg_ref, o_ref, g_ref, u_scr):
    @pl.when(pl.program_id(1) == 0)
    def _():
        u = _modulate(x_ref, sc_ref, sh_ref)
        u_scr[...] = u
        g_ref[...] = jnp.dot(u, wg_ref[...].astype(BF16), preferred_element_type=F32)

    o_ref[...] = jnp.dot(u_scr[...], w_ref[...].astype(BF16), preferred_element_type=F32).astype(o_ref.dtype)


def _proj_perm_kernel(xa_ref, xb_ref, sc_ref, sh_ref, w_ref, o_ref, u_scr):
    nc = xa_ref.shape[0]

    @pl.when(pl.program_id(1) == 0)
    def _():
        u_scr[:nc] = _modulate(xa_ref, sc_ref, sh_ref)
        u_scr[nc:] = _modulate(xb_ref, sc_ref, sh_ref)

    o_ref[...] = jnp.dot(u_scr[...], w_ref[...].astype(BF16), preferred_element_type=F32).astype(o_ref.dtype)


def _in_proj_gate(x, scale, shift, w, layer, wg, tm=1024, tn=1024):
    l, d = x.shape
    n = (w.shape[2] // tn) * tn
    ng = wg.shape[1]
    tm = min(tm, l)
    vmem = 2 * tm * d * 4 + 2 * d * tn * 4 + 2 * tm * tn * 2 + tm * d * 2 + 2 * d * ng * 4 + 2 * tm * ng * 4
    return pl.pallas_call(
        _proj_gate_kernel,
        grid=(l // tm, n // tn),
        in_specs=[
            pl.BlockSpec((tm, d), lambda i, j: (i, 0)),
            pl.BlockSpec((1, d), lambda i, j: (0, 0)),
            pl.BlockSpec((1, d), lambda i, j: (0, 0)),
            pl.BlockSpec((None, d, tn), lambda i, j: (layer, 0, j)),
            pl.BlockSpec((d, ng), lambda i, j: (0, 0)),
        ],
        out_specs=[pl.BlockSpec((tm, tn), lambda i, j: (i, j)),
                   pl.BlockSpec((tm, ng), lambda i, j: (i, 0))],
        out_shape=[jax.ShapeDtypeStruct((l, n), BF16), jax.ShapeDtypeStruct((l, ng), F32)],
        scratch_shapes=[pltpu.VMEM((tm, d), BF16)],
        compiler_params=_cparams(("arbitrary", "arbitrary"), vmem),
        name="in_proj_gate",
    )(x, scale, shift, w, wg)


def _in_proj_perm(x2, t, scale, shift, w, layer, tn=1024):
    nc = x2.shape[0]
    d = x2.shape[1] // t
    n = w.shape[2]
    tm = 2 * nc
    vmem = 2 * tm * d * 4 + 2 * d * tn * 4 + 2 * tm * tn * 2 + tm * d * 2
    return pl.pallas_call(
        _proj_perm_kernel,
        grid=(t // 2, n // tn),
        in_specs=[
            pl.BlockSpec((nc, d), lambda i, j: (0, 2 * i)),
            pl.BlockSpec((nc, d), lambda i, j: (0, 2 * i + 1)),
            pl.BlockSpec((1, d), lambda i, j: (0, 0)),
            pl.BlockSpec((1, d), lambda i, j: (0, 0)),
            pl.BlockSpec((None, d, tn), lambda i, j: (layer, 0, j)),
        ],
        out_specs=pl.BlockSpec((tm, tn), lambda i, j: (i, j)),
        out_shape=jax.ShapeDtypeStruct((nc * t, n), BF16),
        scratch_shapes=[pltpu.VMEM((tm, d), BF16)],
        compiler_params=_cparams(("arbitrary", "arbitrary"), vmem),
        name="in_proj_perm",
    )(x2, x2, scale, shift, w)


def _gla_core_kernel(*refs, dk, dv):
    nh = GLA_HEADS
    q_ref, k_ref = refs[0], refs[1]
    v_refs = refs[2:2 + nh]
    z_refs = refs[2 + nh:2 + 2 * nh]
    g_ref, w2_ref, gb_ref, ng_ref, o_ref, st_ref = refs[2 + 2 * nh:]

    @pl.when(pl.program_id(0) == 0)
    def _():
        st_ref[...] = jnp.zeros_like(st_ref)

    c = q_ref.shape[0]
    pre = jnp.dot(g_ref[...], w2_ref[...], precision=HIGHEST, preferred_element_type=F32) + gb_ref[...]
    la = (jnp.minimum(pre, 0.0) - jnp.log1p(jnp.exp(-jnp.abs(pre)))) * (1.0 / GLA_TAU)
    row = lax.broadcasted_iota(jnp.int32, (c, c), 0)
    col = lax.broadcasted_iota(jnp.int32, (c, c), 1)
    tri = (col <= row).astype(F32)
    cum = jnp.dot(tri, la, precision=HIGHEST, preferred_element_type=F32)
    tot = cum[c - 1:c, :]
    kdec = (k_ref[...].astype(F32) * jnp.exp(tot - cum)).astype(BF16)
    dec = jnp.exp(tot)
    for h in range(nh):
        ks = slice(h * dk, (h + 1) * dk)
        vs = slice(h * dv, (h + 1) * dv)
        upd = lax.dot_general(v_refs[h][...], kdec[:, ks], (((0,), (0,)), ((), ())),
                              preferred_element_type=F32)
        st = dec[:, ks] * st_ref[h] + upd
        st_ref[h] = st
        o = lax.dot_general(q_ref[:, ks], st.astype(BF16), (((1,), (1,)), ((), ())),
                            preferred_element_type=F32) * (dk ** -0.5)
        o = o * lax.rsqrt(jnp.mean(o * o, axis=-1, keepdims=True) + RMS_EPS)
        y = o * ng_ref[:, vs] * _silu(z_refs[h][...].astype(F32))
        o_ref[:, vs] = y.astype(o_ref.dtype)


def _gla_core(proj, glr, w2p, gate_b, norm_g, e, qk):
    l = proj.shape[0]
    nh = GLA_HEADS
    dk, dv = qk // nh, e // nh
    assert (2 * qk) % dv == 0
    v0 = 2 * qk // dv
    ng = glr.shape[1]
    c = CHUNK
    kern = functools.partial(_gla_core_kernel, dk=dk, dv=dv)
    vmem = 2 * (2 * c * e * 2 + 2 * c * qk * 2 + c * ng * 4 + ng * qk * 4 + c * e * 2) + nh * dv * dk * 4
    head_specs = [pl.BlockSpec((c, dv), functools.partial(lambda n, b: (n, b), b=v0 + h)) for h in range(2 * nh)]
    return pl.pallas_call(
        kern,
        grid=(l // c,),
        in_specs=[
            pl.BlockSpec((c, qk), lambda n: (n, 0)),
            pl.BlockSpec((c, qk), lambda n: (n, 1)),
            *head_specs,
            pl.BlockSpec((c, ng), lambda n: (n, 0)),
            pl.BlockSpec((ng, qk), lambda n: (0, 0)),
            pl.BlockSpec((1, qk), lambda n: (0, 0)),
            pl.BlockSpec((1, e), lambda n: (0, 0)),
        ],
        out_specs=pl.BlockSpec((c, e), lambda n: (n, 0)),
        out_shape=jax.ShapeDtypeStruct((l, e), BF16),
        scratch_shapes=[pltpu.VMEM((nh, dv, dk), F32)],
        compiler_params=_cparams(("arbitrary",), vmem + (8 << 20)),
        name="gla_core",
    )(*([proj] * (2 + 2 * nh)), glr, w2p, gate_b, norm_g)


def _out_ln_kernel(y_ref, w_ref, x_ref, gate_ref, g_ref, b_ref, o_ref, acc_ref, *, alpha):
    k = pl.program_id(1)

    @pl.when(k == 0)
    def _():
        acc_ref[...] = jnp.zeros_like(acc_ref)

    acc_ref[...] += jnp.dot(y_ref[...], w_ref[...], preferred_element_type=F32)

    @pl.when(k == pl.num_programs(1) - 1)
    def _():
        r = alpha * x_ref[...] + (1.0 + gate_ref[...]) * acc_ref[...]
        mu = jnp.mean(r, axis=-1, keepdims=True)
        rc = r - mu
        var = jnp.mean(rc * rc, axis=-1, keepdims=True)
        o_ref[...] = rc * lax.rsqrt(var + LN_EPS) * g_ref[...] + b_ref[...]


def _out_ln(y, w, x, gate, ln_g, ln_b, alpha, step_major_t=None, tm=512, tk=1024):
    l, e = y.shape
    d = w.shape[1]
    kern = functools.partial(_out_ln_kernel, alpha=alpha)
    if step_major_t is None:
        tm = min(tm, l)
        x_in, out_shape = x, (l, d)
        x_spec = pl.BlockSpec((tm, d), lambda i, k: (i, 0))
    else:
        t = step_major_t
        tm = l // t
        x_in, out_shape = x.reshape(tm, t * d), (tm, t * d)
        x_spec = pl.BlockSpec((tm, d), lambda i, k: (0, i))
    vmem = 2 * tm * tk * 2 + 2 * tk * d * 2 + 4 * tm * d * 4 + tm * d * 4
    out = pl.pallas_call(
        kern,
        grid=(l // tm, e // tk),
        in_specs=[
            pl.BlockSpec((tm, tk), lambda i, k: (i, k)),
            pl.BlockSpec((tk, d), lambda i, k: (k, 0)),
            x_spec,
            pl.BlockSpec((1, d), lambda i, k: (0, 0)),
            pl.BlockSpec((1, d), lambda i, k: (0, 0)),
            pl.BlockSpec((1, d), lambda i, k: (0, 0)),
        ],
        out_specs=x_spec,
        out_shape=jax.ShapeDtypeStruct(out_shape, F32),
        scratch_shapes=[pltpu.VMEM((tm, d), F32)],
        compiler_params=_cparams(("arbitrary", "arbitrary"), vmem),
        name="out_proj_ln",
    )(y, w, x_in, gate, ln_g, ln_b)
    return out.reshape(l, d)


def _lane_block_transpose8(cols):
    blk = lax.shift_right_logical(lax.broadcasted_iota(jnp.int32, cols[0].shape, 1), 4)
    for b in range(3):
        dist = 1 << b
        sel = (blk & dist) != 0
        new = list(cols)
        for j in range(8):
            if not j & dist:
                lo, hi = cols[j], cols[j + dist]
                new[j] = jnp.where(sel, pltpu.roll(hi, S5_GROUP * dist, 1), lo)
                new[j + dist] = jnp.where(sel, hi, pltpu.roll(lo, LANES - S5_GROUP * dist, 1))
        cols = new
    return cols


def _ssm_kernel(x_ref, w_ref, p_ref, q_ref, lr_ref, li_ref, d_ref, o_ref, a_scr, y_scr, vre, vim, sre, sim):
    nc = x_ref.shape[1]
    npair = S5_OCT // 2

    rb = min(64, nc)

    def gather_steps(it, carry):
        r = pl.ds(pl.multiple_of(it * rb, rb), rb)
        for s_hi in range(2):
            for g_hi in range(2):
                cols = [pltpu.bitcast(x_ref[8 * s_hi + m, r, g_hi * LANES:(g_hi + 1) * LANES], U32)
                        for m in range(8)]
                res = _lane_block_transpose8(cols)
                for m in range(8):
                    a_scr[8 * g_hi + m, r, s_hi * LANES:(s_hi + 1) * LANES] = pltpu.bitcast(res[m], BF16)
        return carry

    lax.fori_loop(0, nc // rb, gather_steps, 0)
    a_bf = a_scr

    nt = (((1,), (1,)), ((), ()))
    for p in range(npair):
        v = (lax.dot_general(a_bf[2 * p], p_ref[2 * p], nt, preferred_element_type=F32)
             + lax.dot_general(a_bf[2 * p + 1], p_ref[2 * p + 1], nt, preferred_element_type=F32))
        vre[pl.ds(p, nc, stride=npair), :] = v[:, :LANES]
        vim[pl.ds(p, nc, stride=npair), :] = v[:, LANES:]

    ar = lr_ref[...]
    ai = li_ref[...]

    def step(c, carry):
        xr, xi = carry
        rows = pl.ds(pl.multiple_of(c * npair, npair), npair)
        sre[rows, :] = xr
        sim[rows, :] = xi
        nxr = ar * xr - ai * xi + vre[rows, :]
        nxi = ar * xi + ai * xr + vim[rows, :]
        return nxr, nxi

    zero = jnp.zeros((npair, LANES), F32)
    lax.fori_loop(0, nc, step, (zero, zero), unroll=8)

    for p in range(npair):
        s = jnp.concatenate([sre[pl.ds(p, nc, stride=npair), :], sim[pl.ds(p, nc, stride=npair), :]],
                            axis=1).astype(BF16)
        for g in (2 * p, 2 * p + 1):
            y_scr[g] = (jnp.dot(a_bf[g], w_ref[g], preferred_element_type=F32)
                        + jnp.dot(s, q_ref[g], preferred_element_type=F32))

    rbo = min(32, nc)

    def scatter_steps(it, carry):
        r = pl.ds(pl.multiple_of(it * rbo, rbo), rbo)
        for s_hi in range(2):
            for g_hi in range(2):
                cs = slice(g_hi * LANES, (g_hi + 1) * LANES)
                cols = [y_scr[8 * g_hi + m, r, s_hi * LANES:(s_hi + 1) * LANES] for m in range(8)]
                res = _lane_block_transpose8(cols)
                dsk = d_ref[:, cs]
                for m in range(8):
                    s = 8 * s_hi + m
                    y = res[m] + dsk * x_ref[s, r, cs].astype(F32)
                    o_ref[s, r, cs] = _gelu_tanh(y).astype(o_ref.dtype)
        return carry

    lax.fori_loop(0, nc // rbo, scatter_steps, 0)


def _ssm_core(uz, wt, pm, qm, lam_r, lam_i, d_skip, e):
    t = S5_T
    nc = uz.shape[0] // t
    kk = t * S5_GROUP
    assert kk == 2 * LANES and S5_OCT * S5_GROUP == kk
    noct = e // kk
    npair = S5_OCT // 2
    x3 = uz.reshape(t, nc, uz.shape[1])
    vmem = (2 * (2 * t * nc * kk * 2 + 3 * S5_OCT * kk * kk * 2) + S5_OCT * nc * kk * (2 + 4)
            + 4 * nc * npair * LANES * 4)
    out = pl.pallas_call(
        _ssm_kernel,
        grid=(noct,),
        in_specs=[
            pl.BlockSpec((t, nc, kk), lambda i: (0, 0, i)),
            pl.BlockSpec((S5_OCT, kk, kk), lambda i: (i, 0, 0)),
            pl.BlockSpec((S5_OCT, kk, kk), lambda i: (i, 0, 0)),
            pl.BlockSpec((S5_OCT, kk, kk), lambda i: (i, 0, 0)),
            pl.BlockSpec((None, npair, LANES), lambda i: (i, 0, 0)),
            pl.BlockSpec((None, npair, LANES), lambda i: (i, 0, 0)),
            pl.BlockSpec((1, kk), lambda i: (0, i)),
        ],
        out_specs=pl.BlockSpec((t, nc, kk), lambda i: (0, 0, i)),
        out_shape=jax.ShapeDtypeStruct((t, nc, e), BF16),
        scratch_shapes=[pltpu.VMEM((S5_OCT, nc, kk), BF16), pltpu.VMEM((S5_OCT, nc, kk), F32)]
        + [pltpu.VMEM((nc * npair, LANES), F32) for _ in range(4)],
        compiler_params=_cparams(("arbitrary",), vmem),
        name="s5_ssm",
    )(x3, wt, pm, qm, lam_r, lam_i, d_skip)
    return out.reshape(t * nc, e)


def _cmul(ar, ai, br, bi):
    return ar * br - ai * bi, ar * bi + ai * br


def _zoh(a_re, a_im, dt):
    mag = jnp.exp(a_re * dt)
    return mag * jnp.cos(a_im * dt), mag * jnp.sin(a_im * dt)


def _s5_ops_kernel(are_ref, aim_ref, ldt_ref, bre_ref, bim_ref, cre_ref, cim_ref, arp_ref, aip_ref, ldtp_ref,
                   w_ref, pt_ref, q_ref, lamr_ref, lami_ref, rows_scr):
    t, gi = S5_T, S5_GROUP
    ns = are_ref.shape[1]
    kk = t * gi

    pr, pi = _zoh(arp_ref[...], aip_ref[...], jnp.exp(ldtp_ref[...]))
    for _ in range(t.bit_length() - 1):
        pr, pi = _cmul(pr, pi, pr, pi)
    lamr_ref[...] = pr
    lami_ref[...] = pi

    ar = are_ref[...]
    ai = aim_ref[...]
    l1r, l1i = _zoh(ar, ai, jnp.exp(ldt_ref[...]))
    nr = l1r - 1.0
    den = ar * ar + ai * ai
    quantities = [l1r, l1i]
    for _ in range(3):
        quantities += list(_cmul(quantities[-2], quantities[-1], quantities[-2], quantities[-1]))
    quantities += [(nr * ar + l1i * ai) / den, (l1i * ar - nr * ai) / den]
    for k, val in enumerate(quantities):
        rows_scr[k] = val

    eye = lax.broadcasted_iota(jnp.int32, (ns, ns), 0) == lax.broadcasted_iota(jnp.int32, (ns, ns), 1)
    lane = lax.broadcasted_iota(jnp.int32, (ns, kk), 1)
    tau = lax.shift_right_logical(lane, 4)
    bits = [(lax.shift_right_logical(tau, b) & 1) == 1 for b in range(4)]
    expand = (lax.broadcasted_iota(jnp.int32, (gi, kk), 0)
              == (lax.broadcasted_iota(jnp.int32, (gi, kk), 1) & (gi - 1))).astype(F32)
    lane_w = lax.broadcasted_iota(jnp.int32, (gi, LANES), 1)
    tn_dims = (((0,), (0,)), ((), ()))

    def column(k, g):
        row = rows_scr[k, pl.ds(g, 1), :]
        return jnp.sum(jnp.where(eye, row, 0.0), axis=1, keepdims=True)

    def lane_powers(cols, flip):
        pr = pi = None
        for b in range(4):
            on = jnp.logical_not(bits[b]) if flip else bits[b]
            fr = jnp.where(on, cols[2 * b], 1.0)
            fi = jnp.where(on, cols[2 * b + 1], 0.0)
            pr, pi = (fr, fi) if pr is None else _cmul(pr, pi, fr, fi)
        return pr, pi

    def build(g, odd):
        cols = [column(k, g) for k in range(10)]
        cr, ci = cols[8], cols[9]
        bbr = cr * bre_ref[g] - ci * bim_ref[g]
        bbi = cr * bim_ref[g] + ci * bre_ref[g]
        pwr, pwi = lane_powers(cols, False)
        rvr, rvi = lane_powers(cols, True)
        p1r, p1i = _cmul(pwr, pwi, cols[0], cols[1])
        ctr = lax.dot_general(cre_ref[g], expand, tn_dims, precision=HIGHEST, preferred_element_type=F32)
        cti = lax.dot_general(cim_ref[g], expand, tn_dims, precision=HIGHEST, preferred_element_type=F32)
        bbtr = jnp.dot(bbr, expand, precision=HIGHEST, preferred_element_type=F32)
        bbti = jnp.dot(bbi, expand, precision=HIGHEST, preferred_element_type=F32)
        clr, cli = _cmul(ctr, cti, pwr, pwi)
        kt = (lax.dot_general(bbr, clr, tn_dims, precision=HIGHEST, preferred_element_type=F32)
              - lax.dot_general(bbi, cli, tn_dims, precision=HIGHEST, preferred_element_type=F32))
        k0, k1 = kt[:, :LANES], kt[:, LANES:]
        for s in range(t):
            sh = (gi * s) % LANES
            r0 = pltpu.roll(k0, sh, 1) if sh else k0
            if gi * s < LANES:
                r1 = pltpu.roll(k1, sh, 1) if sh else k1
                lo = jnp.where(lane_w >= sh, r0, 0.0)
                hi = jnp.where(lane_w >= sh, r1, r0)
            else:
                lo = jnp.zeros_like(k0)
                hi = jnp.where(lane_w >= sh, r0, 0.0)
            w_ref[g, gi * s:gi * (s + 1), :] = jnp.concatenate([lo, hi], axis=1).astype(w_ref.dtype)
        ptr, pti = _cmul(rvr, rvi, bbtr, bbti)
        qr, qi = _cmul(ctr, cti, p1r, p1i)
        zero = jnp.zeros((ns, kk), pt_ref.dtype)
        for ref, re, im in ((pt_ref, ptr, pti), (q_ref, qr, -qi)):
            ref[g, odd * ns:(odd + 1) * ns, :] = re.astype(ref.dtype)
            ref[g, (1 - odd) * ns:(2 - odd) * ns, :] = zero
            ref[g, (2 + odd) * ns:(3 + odd) * ns, :] = im.astype(ref.dtype)
            ref[g, (3 - odd) * ns:(4 - odd) * ns, :] = zero

    def pair(p, carry):
        build(2 * p, 0)
        build(2 * p + 1, 1)
        return carry

    lax.fori_loop(0, S5_OCT // 2, pair, 0)


def _s5_operators(a_re, a_im, log_dt, b_re, b_im, c_re, c_im):
    g, ns = a_re.shape
    t, gi = S5_T, S5_GROUP
    kk = t * gi
    assert t == 16 and 4 * ns == kk and 2 * ns == LANES
    noct = g // S5_OCT
    npair = S5_OCT // 2
    a_re_p = a_re.reshape(g // 2, 2 * ns)
    a_im_p = a_im.reshape(g // 2, 2 * ns)
    ldt_p = jnp.broadcast_to(log_dt[:, None], (g, ns)).reshape(g // 2, 2 * ns)
    oct3 = lambda i: (i, 0, 0)
    ops_shape = jax.ShapeDtypeStruct((g, kk, kk), BF16)
    lam_shape = jax.ShapeDtypeStruct((noct, npair, LANES), F32)
    return pl.pallas_call(
        _s5_ops_kernel,
        grid=(noct,),
        in_specs=[
            pl.BlockSpec((S5_OCT, ns), lambda i: (i, 0)),
            pl.BlockSpec((S5_OCT, ns), lambda i: (i, 0)),
            pl.BlockSpec((S5_OCT, 1), lambda i: (i, 0)),
            pl.BlockSpec((S5_OCT, ns, gi), oct3),
            pl.BlockSpec((S5_OCT, ns, gi), oct3),
            pl.BlockSpec((S5_OCT, gi, ns), oct3),
            pl.BlockSpec((S5_OCT, gi, ns), oct3),
            pl.BlockSpec((npair, LANES), lambda i: (i, 0)),
            pl.BlockSpec((npair, LANES), lambda i: (i, 0)),
            pl.BlockSpec((npair, LANES), lambda i: (i, 0)),
        ],
        out_specs=[pl.BlockSpec((S5_OCT, kk, kk), oct3)] * 3 + [pl.BlockSpec((None, npair, LANES), oct3)] * 2,
        out_shape=[ops_shape] * 3 + [lam_shape] * 2,
        scratch_shapes=[pltpu.VMEM((10, S5_OCT, ns), F32)],
        compiler_params=_cparams(("arbitrary",), 2 * 3 * S5_OCT * kk * kk * 2 + 4 * S5_OCT * ns * LANES * 4 * 2),
        name="s5_ops",
    )(a_re, a_im, log_dt.reshape(g, 1), b_re, b_im, c_re, c_im, a_re_p, a_im_p, ldt_p)


def _glu_kernel(ya_ref, w_ref, b_ref, yc_ref, z_ref, o_ref):
    acc = jnp.dot(ya_ref[...], w_ref[...].astype(BF16), preferred_element_type=F32) + b_ref[...]
    o_ref[...] = (yc_ref[...].astype(F32) * _sigmoid(acc) * _silu(z_ref[...].astype(F32))).astype(o_ref.dtype)


def _glu(yact, w_glu, layer, b_glu, uz, tm=1024, tn=512):
    l, e = yact.shape
    tm = min(tm, l)
    zoff = e // tn
    vmem = 2 * (tm * e * 2 + e * tn * 4 + 3 * tm * tn * 2)
    return pl.pallas_call(
        _glu_kernel,
        grid=(l // tm, e // tn),
        in_specs=[pl.BlockSpec((tm, e), lambda i, j: (i, 0)),
                  pl.BlockSpec((None, e, tn), lambda i, j: (layer, 0, j)),
                  pl.BlockSpec((1, tn), lambda i, j: (0, j)),
                  pl.BlockSpec((tm, tn), lambda i, j: (i, j)),
                  pl.BlockSpec((tm, tn), lambda i, j: (i, zoff + j))],
        out_specs=pl.BlockSpec((tm, tn), lambda i, j: (i, j)),
        out_shape=jax.ShapeDtypeStruct((l, e), BF16),
        compiler_params=_cparams(("arbitrary", "arbitrary"), vmem),
        name="s5_glu",
    )(yact, w_glu, b_glu, yact, uz)


def _gla_layer(x, scale, shift, gate, ln_g, ln_b, alpha, w_in, layer, gate_w2, gate_b, norm_g, w_out):
    d = x.shape[1]
    e = w_out.shape[0]
    qk = gate_w2.shape[1]
    wg = jnp.pad(w_in[layer, :, 2 * qk + 2 * e:], ((0, 0), (0, LANES - GLA_GATE_RANK)))
    w2p = jnp.pad(gate_w2, ((0, LANES - GLA_GATE_RANK), (0, 0)))
    proj, glr = _in_proj_gate(x, scale, shift, w_in, layer, wg)
    y = _gla_core(proj, glr, w2p, gate_b.reshape(1, qk), norm_g.reshape(1, e), e, qk)
    return _out_ln(y, w_out.astype(BF16), x, gate, ln_g.reshape(1, d), ln_b.reshape(1, d), alpha)


def _s5_layer(x, scale, shift, gate, ln_g, ln_b, alpha, w_in, layer, a_re, a_im, log_dt, b_re, b_im, c_re, c_im,
              d_skip, w_glu, b_glu, w_out):
    l, d = x.shape
    e = w_out.shape[0]
    t = S5_T
    uz = _in_proj_perm(x.reshape(l // t, t * d), t, scale, shift, w_in, layer)
    wt, pm, qm, lam_r, lam_i = _s5_operators(a_re, a_im, log_dt, b_re, b_im, c_re, c_im)
    yact = _ssm_core(uz, wt, pm, qm, lam_r, lam_i, d_skip.reshape(1, e), e)
    yglu = _glu(yact, w_glu, layer, b_glu.reshape(1, e), uz)
    return _out_ln(yglu, w_out.astype(BF16), x, gate, ln_g.reshape(1, d), ln_b.reshape(1, d), alpha,
                   step_major_t=t)


def kernel(x, c, ln_g, ln_b, ada_w, ada_b, gla_w_in, gla_gate_w2, gla_gate_b, gla_norm_g, gla_w_out,
           s5_w_in, s5_a_re, s5_a_im, s5_log_dt, s5_b_re, s5_b_im, s5_c_re, s5_c_im, s5_d,
           s5_w_glu, s5_b_glu, s5_w_out):
    bsz, l, d = x.shape
    assert bsz == 1, "batch 1 only"
    depth = ln_g.shape[0]
    alpha = (2 * depth) ** 0.25
    mod = _adaln(c, ada_w, ada_b)
    h = x.reshape(l, d)
    for i in range(depth):
        shift, scale, gate = mod[i, :, :d], mod[i, :, d:2 * d], mod[i, :, 2 * d:]
        j = i // 2
        if i % 2 == 0:
            h = _gla_layer(h, scale, shift, gate, ln_g[i], ln_b[i], alpha, gla_w_in, j, gla_gate_w2[j],
                           gla_gate_b[j], gla_norm_g[j], gla_w_out[j])
        else:
            h = _s5_layer(h, scale, shift, gate, ln_g[i], ln_b[i], alpha, s5_w_in, j, s5_a_re[j], s5_a_im[j],
                          s5_log_dt[j], s5_b_re[j], s5_b_im[j], s5_c_re[j], s5_c_im[j], s5_d[j],
                          s5_w_glu, s5_b_glu[j], s5_w_out[j])
    return h.reshape(bsz, l, d)
```

```python
import functools
import math

import jax
import jax.numpy as jnp
from jax import lax
from jax.experimental import pallas as pl
from jax.experimental.pallas import tpu as pltpu

F32 = jnp.float32
BF16 = jnp.bfloat16
U32 = jnp.uint32
HIGHEST = lax.Precision.HIGHEST

CHUNK = 64
GLA_HEADS = 4
GLA_GATE_RANK = 16
GLA_TAU = 16.0
S5_GROUP = 16
S5_STATE = 64
LN_EPS = 1e-5
RMS_EPS = 1e-6

S5_T = 16
S5_OCT = 16
LANES = 128
V7X_SCOPED_VMEM_CAP = 60000 * 1024


def _cparams(semantics, vmem_bytes):
    limit = min(int(vmem_bytes) + (6 << 20), V7X_SCOPED_VMEM_CAP)
    return pltpu.CompilerParams(dimension_semantics=semantics, vmem_limit_bytes=limit)


def _sigmoid(x):
    return 1.0 / (1.0 + jnp.exp(-x))


def _silu(x):
    return x * _sigmoid(x)


def _gelu_tanh(y):
    cdf = 0.5 * (1.0 + jnp.tanh(math.sqrt(2.0 / math.pi) * (y + 0.044715 * (y * y * y))))
    return y * cdf


def _adaln_kernel(c_ref, w_ref, b_ref, o_ref):
    c = c_ref[...]
    o_ref[...] = jnp.sum(_silu(c) * w_ref[...], axis=0, keepdims=True) + b_ref[...]


def _adaln(c, ada_w, ada_b, tn=1024):
    depth, d, n3 = ada_w.shape
    assert c.shape == (1, d), "batch 1 only"
    c_col = c.reshape(d, 1)
    out = pl.pallas_call(
        _adaln_kernel,
        grid=(depth, n3 // tn),
        in_specs=[
            pl.BlockSpec((d, 1), lambda l, j: (0, 0)),
            pl.BlockSpec((None, d, tn), lambda l, j: (l, 0, j)),
            pl.BlockSpec((None, 1, tn), lambda l, j: (l, 0, j)),
        ],
        out_specs=pl.BlockSpec((None, 1, tn), lambda l, j: (l, 0, j)),
        out_shape=jax.ShapeDtypeStruct((depth, 1, n3), F32),
        compiler_params=_cparams(("arbitrary", "arbitrary"), 2 * d * tn * 4 + d * LANES * 4),
        name="adaln_mod",
    )(c_col, ada_w, ada_b.reshape(depth, 1, n3))
    return out


def _modulate(x_ref, sc_ref, sh_ref):
    return (x_ref[...] * (1.0 + sc_ref[...]) + sh_ref[...]).astype(BF16)


def _proj_gate_kernel(x_ref, sc_ref, sh_ref, w_ref, wg_ref, o_ref, g_ref, u_scr):
    @pl.when(pl.program_id(1) == 0)
    def _():
        u = _modulate(x_ref, sc_ref, sh_ref)
        u_scr[...] = u
        g_ref[...] = jnp.dot(u, wg_ref[...].astype(BF16), preferred_element_type=F32)

    o_ref[...] = lax.dot_general(u_scr[...], w_ref[...].astype(BF16), (((1,), (1,)), ((), ())),
                                 preferred_element_type=F32).astype(o_ref.dtype)


def _proj_perm_kernel(xa_ref, xb_ref, sc_ref, sh_ref, w_ref, o_ref, u_scr):
    nc = xa_ref.shape[0]

    @pl.when(pl.program_id(1) == 0)
    def _():
        u_scr[:nc] = _modulate(xa_ref, sc_ref, sh_ref)
        u_scr[nc:] = _modulate(xb_ref, sc_ref, sh_ref)

    o_ref[...] = jnp.dot(u_scr[...], w_ref[...].astype(BF16), preferred_element_type=F32).astype(o_ref.dtype)


def _in_proj_gate(x, scale, shift, wt, layer, wg, tm=1024, tn=1024):
    l, d = x.shape
    n = (wt.shape[1] // tn) * tn
    ng = wg.shape[1]
    tm = min(tm, l)
    vmem = 2 * tm * d * 4 + 2 * d * tn * 4 + 2 * tm * tn * 2 + tm * d * 2 + 2 * d * ng * 4 + 2 * tm * ng * 4
    return pl.pallas_call(
        _proj_gate_kernel,
        grid=(l // tm, n // tn),
        in_specs=[
            pl.BlockSpec((tm, d), lambda i, j: (i, 0)),
            pl.BlockSpec((1, d), lambda i, j: (0, 0)),
            pl.BlockSpec((1, d), lambda i, j: (0, 0)),
            pl.BlockSpec((None, tn, d), lambda i, j: (layer, j, 0)),
            pl.BlockSpec((d, ng), lambda i, j: (0, 0)),
        ],
        out_specs=[pl.BlockSpec((tm, tn), lambda i, j: (i, j)),
                   pl.BlockSpec((tm, ng), lambda i, j: (i, 0))],
        out_shape=[jax.ShapeDtypeStruct((l, n), BF16), jax.ShapeDtypeStruct((l, ng), F32)],
        scratch_shapes=[pltpu.VMEM((tm, d), BF16)],
        compiler_params=_cparams(("arbitrary", "arbitrary"), vmem),
        name="in_proj_gate",
    )(x, scale, shift, wt, wg)


def _in_proj_perm(x2, t, scale, shift, w, layer, tn=1024):
    nc = x2.shape[0]
    d = x2.shape[1] // t
    n = w.shape[2]
    tm = 2 * nc
    vmem = 2 * tm * d * 4 + 2 * d * tn * 4 + 2 * tm * tn * 2 + tm * d * 2
    return pl.pallas_call(
        _proj_perm_kernel,
        grid=(t // 2, n // tn),
        in_specs=[
            pl.BlockSpec((nc, d), lambda i, j: (0, 2 * i)),
            pl.BlockSpec((nc, d), lambda i, j: (0, 2 * i + 1)),
            pl.BlockSpec((1, d), lambda i, j: (0, 0)),
            pl.BlockSpec((1, d), lambda i, j: (0, 0)),
            pl.BlockSpec((None, d, tn), lambda i, j: (layer, 0, j)),
        ],
        out_specs=pl.BlockSpec((tm, tn), lambda i, j: (i, j)),
        out_shape=jax.ShapeDtypeStruct((nc * t, n), BF16),
        scratch_shapes=[pltpu.VMEM((tm, d), BF16)],
        compiler_params=_cparams(("arbitrary", "arbitrary"), vmem),
        name="in_proj_perm",
    )(x2, x2, scale, shift, w)


def _gla_core_kernel(*refs, dk, dv):
    nh = GLA_HEADS
    q_ref, k_ref = refs[0], refs[1]
    v_refs = refs[2:2 + nh]
    z_refs = refs[2 + nh:2 + 2 * nh]
    g_ref, w2_ref, gb_ref, ng_ref, o_ref, st_ref = refs[2 + 2 * nh:]

    @pl.when(pl.program_id(0) == 0)
    def _():
        st_ref[...] = jnp.zeros_like(st_ref)

    c = q_ref.shape[0]
    pre = jnp.dot(g_ref[...], w2_ref[...], precision=HIGHEST, preferred_element_type=F32) + gb_ref[...]
    la = (jnp.minimum(pre, 0.0) - jnp.log1p(jnp.exp(-jnp.abs(pre)))) * (1.0 / GLA_TAU)
    row = lax.broadcasted_iota(jnp.int32, (c, c), 0)
    col = lax.broadcasted_iota(jnp.int32, (c, c), 1)
    tri = (col <= row).astype(F32)
    cum = jnp.dot(tri, la, precision=HIGHEST, preferred_element_type=F32)
    tot = cum[c - 1:c, :]
    kdec = (k_ref[...].astype(F32) * jnp.exp(tot - cum)).astype(BF16)
    dec = jnp.exp(tot)
    for h in range(nh):
        ks = slice(h * dk, (h + 1) * dk)
        vs = slice(h * dv, (h + 1) * dv)
        upd = lax.dot_general(v_refs[h][...], kdec[:, ks], (((0,), (0,)), ((), ())),
                              preferred_element_type=F32)
        st = dec[:, ks] * st_ref[h] + upd
        st_ref[h] = st
        o = lax.dot_general(q_ref[:, ks], st.astype(BF16), (((1,), (1,)), ((), ())),
                            preferred_element_type=F32) * (dk ** -0.5)
        o = o * lax.rsqrt(jnp.mean(o * o, axis=-1, keepdims=True) + RMS_EPS)
        y = o * ng_ref[:, vs] * _silu(z_refs[h][...].astype(F32))
        o_ref[:, vs] = y.astype(o_ref.dtype)


def _gla_core(proj, glr, w2p, gate_b, norm_g, e, qk):
    l = proj.shape[0]
    nh = GLA_HEADS
    dk, dv = qk // nh, e // nh
    assert (2 * qk) % dv == 0
    v0 = 2 * qk // dv
    ng = glr.shape[1]
    c = CHUNK
    kern = functools.partial(_gla_core_kernel, dk=dk, dv=dv)
    vmem = 2 * (2 * c * e * 2 + 2 * c * qk * 2 + c * ng * 4 + ng * qk * 4 + c * e * 2) + nh * dv * dk * 4
    head_specs = [pl.BlockSpec((c, dv), functools.partial(lambda n, b: (n, b), b=v0 + h)) for h in range(2 * nh)]
    return pl.pallas_call(
        kern,
        grid=(l // c,),
        in_specs=[
            pl.BlockSpec((c, qk), lambda n: (n, 0)),
            pl.BlockSpec((c, qk), lambda n: (n, 1)),
            *head_specs,
            pl.BlockSpec((c, ng), lambda n: (n, 0)),
            pl.BlockSpec((ng, qk), lambda n: (0, 0)),
            pl.BlockSpec((1, qk), lambda n: (0, 0)),
            pl.BlockSpec((1, e), lambda n: (0, 0)),
        ],
        out_specs=pl.BlockSpec((c, e), lambda n: (n, 0)),
        out_shape=jax.ShapeDtypeStruct((l, e), BF16),
        scratch_shapes=[pltpu.VMEM((nh, dv, dk), F32)],
        compiler_params=_cparams(("arbitrary",), vmem + (8 << 20)),
        name="gla_core",
    )(*([proj] * (2 + 2 * nh)), glr, w2p, gate_b, norm_g)


def _out_ln_kernel(y_ref, w_ref, x_ref, gate_ref, g_ref, b_ref, o_ref, acc_ref, *, alpha):
    k = pl.program_id(1)

    @pl.when(k == 0)
    def _():
        acc_ref[...] = jnp.zeros_like(acc_ref)

    acc_ref[...] += jnp.dot(y_ref[...], w_ref[...], preferred_element_type=F32)

    @pl.when(k == pl.num_programs(1) - 1)
    def _():
        r = alpha * x_ref[...] + (1.0 + gate_ref[...]) * acc_ref[...]
        mu = jnp.mean(r, axis=-1, keepdims=True)
        rc = r - mu
        var = jnp.mean(rc * rc, axis=-1, keepdims=True)
        o_ref[...] = rc * lax.rsqrt(var + LN_EPS) * g_ref[...] + b_ref[...]


def _out_ln(y, w, x, gate, ln_g, ln_b, alpha, step_major_t=None, tm=512, tk=1024):
    l, e = y.shape
    d = w.shape[1]
    kern = functools.partial(_out_ln_kernel, alpha=alpha)
    if step_major_t is None:
        tm = min(tm, l)
        x_in, out_shape = x, (l, d)
        x_spec = pl.BlockSpec((tm, d), lambda i, k: (i, 0))
    else:
        t = step_major_t
        tm = l // t
        x_in, out_shape = x.reshape(tm, t * d), (tm, t * d)
        x_spec = pl.BlockSpec((tm, d), lambda i, k: (0, i))
    vmem = 2 * tm * tk * 2 + 2 * tk * d * 2 + 4 * tm * d * 4 + tm * d * 4
    out = pl.pallas_call(
        kern,
        grid=(l // tm, e // tk),
        in_specs=[
            pl.BlockSpec((tm, tk), lambda i, k: (i, k)),
            pl.BlockSpec((tk, d), lambda i, k: (k, 0)),
            x_spec,
            pl.BlockSpec((1, d), lambda i, k: (0, 0)),
            pl.BlockSpec((1, d), lambda i, k: (0, 0)),
            pl.BlockSpec((1, d), lambda i, k: (0, 0)),
        ],
        out_specs=x_spec,
        out_shape=jax.ShapeDtypeStruct(out_shape, F32),
        scratch_shapes=[pltpu.VMEM((tm, d), F32)],
        compiler_params=_cparams(("arbitrary", "arbitrary"), vmem),
        name="out_proj_ln",
    )(y, w, x_in, gate, ln_g, ln_b)
    return out.reshape(l, d)


def _rot_blocks(v, nblk):
    nblk %= 8
    return pltpu.roll(v, S5_GROUP * nblk, 1) if nblk else v


def _skew_select(cols):
    blk = lax.shift_right_logical(lax.broadcasted_iota(jnp.int32, cols[0].shape, 1), 4)
    masks = [blk == j for j in range(8)]
    out = []
    for a in range(8):
        r = cols[(-a) % 8]
        for j in range(1, 8):
            r = jnp.where(masks[j], cols[(j - a) % 8], r)
        out.append(r)
    return out


def _ssm_kernel(x_ref, w_ref, p_ref, q_ref, lr_ref, li_ref, d_ref, o_ref, a_scr, y_scr, vre, vim, sre, sim):
    nc = x_ref.shape[1]
    npair = S5_OCT // 2

    rb = min(64, nc)

    def gather_steps(it, carry):
        r = pl.ds(pl.multiple_of(it * rb, rb), rb)
        for s_hi in range(2):
            for g_hi in range(2):
                cols = [_rot_blocks(pltpu.bitcast(x_ref[8 * s_hi + m, r, g_hi * LANES:(g_hi + 1) * LANES], U32), m)
                        for m in range(8)]
                res = _skew_select(cols)
                for m in range(8):
                    a_scr[8 * g_hi + m, r, s_hi * LANES:(s_hi + 1) * LANES] = pltpu.bitcast(res[m], BF16)
        return carry

    lax.fori_loop(0, nc // rb, gather_steps, 0)
    a_bf = a_scr

    nt = (((1,), (1,)), ((), ()))
    for p in range(npair):
        v = (lax.dot_general(a_bf[2 * p], p_ref[2 * p], nt, preferred_element_type=F32)
             + lax.dot_general(a_bf[2 * p + 1], p_ref[2 * p + 1], nt, preferred_element_type=F32))
        vre[pl.ds(p, nc, stride=npair), :] = v[:, :LANES]
        vim[pl.ds(p, nc, stride=npair), :] = v[:, LANES:]

    ar = lr_ref[...]
    ai = li_ref[...]

    def step(c, carry):
        xr, xi = carry
        rows = pl.ds(pl.multiple_of(c * npair, npair), npair)
        sre[rows, :] = xr
        sim[rows, :] = xi
        nxr = ar * xr - ai * xi + vre[rows, :]
        nxi = ar * xi + ai * xr + vim[rows, :]
        return nxr, nxi

    zero = jnp.zeros((npair, LANES), F32)
    lax.fori_loop(0, nc, step, (zero, zero), unroll=8)

    for p in range(npair):
        s = jnp.concatenate([sre[pl.ds(p, nc, stride=npair), :], sim[pl.ds(p, nc, stride=npair), :]],
                            axis=1).astype(BF16)
        for g in (2 * p, 2 * p + 1):
            y_scr[g] = (jnp.dot(a_bf[g], w_ref[g], preferred_element_type=F32)
                        + jnp.dot(s, q_ref[g], preferred_element_type=F32))

    rbo = min(32, nc)

    def scatter_steps(it, carry):
        r = pl.ds(pl.multiple_of(it * rbo, rbo), rbo)
        for s_hi in range(2):
            for g_hi in range(2):
                cs = slice(g_hi * LANES, (g_hi + 1) * LANES)
                cols = [y_scr[8 * g_hi + m, r, s_hi * LANES:(s_hi + 1) * LANES] for m in range(8)]
                res = _skew_select(cols)
                dsk = d_ref[:, cs]
                for m in range(8):
                    s = 8 * s_hi + m
                    y = _rot_blocks(res[m], -m) + dsk * x_ref[s, r, cs].astype(F32)
                    o_ref[s, r, cs] = _gelu_tanh(y).astype(o_ref.dtype)
        return carry

    lax.fori_loop(0, nc // rbo, scatter_steps, 0)


def _ssm_core(uz, wt, pm, qm, lam_r, lam_i, d_skip, e):
    t = S5_T
    nc = uz.shape[0] // t
    kk = t * S5_GROUP
    assert kk == 2 * LANES and S5_OCT * S5_GROUP == kk
    noct = e // kk
    npair = S5_OCT // 2
    x3 = uz.reshape(t, nc, uz.shape[1])
    vmem = (2 * (2 * t * nc * kk * 2 + 3 * S5_OCT * kk * kk * 2) + S5_OCT * nc * kk * (2 + 4)
            + 4 * nc * npair * LANES * 4)
    out = pl.pallas_call(
        _ssm_kernel,
        grid=(noct,),
        in_specs=[
            pl.BlockSpec((t, nc, kk), lambda i: (0, 0, i)),
            pl.BlockSpec((S5_OCT, kk, kk), lambda i: (i, 0, 0)),
            pl.BlockSpec((S5_OCT, kk, kk), lambda i: (i, 0, 0)),
            pl.BlockSpec((S5_OCT, kk, kk), lambda i: (i, 0, 0)),
            pl.BlockSpec((None, npair, LANES), lambda i: (i, 0, 0)),
            pl.BlockSpec((None, npair, LANES), lambda i: (i, 0, 0)),
            pl.BlockSpec((1, kk), lambda i: (0, i)),
        ],
        out_specs=pl.BlockSpec((t, nc, kk), lambda i: (0, 0, i)),
        out_shape=jax.ShapeDtypeStruct((t, nc, e), BF16),
        scratch_shapes=[pltpu.VMEM((S5_OCT, nc, kk), BF16), pltpu.VMEM((S5_OCT, nc, kk), F32)]
        + [pltpu.VMEM((nc * npair, LANES), F32) for _ in range(4)],
        compiler_params=_cparams(("arbitrary",), vmem),
        name="s5_ssm",
    )(x3, wt, pm, qm, lam_r, lam_i, d_skip)
    return out.reshape(t * nc, e)


def _cmul(ar, ai, br, bi):
    return ar * br - ai * bi, ar * bi + ai * br


def _zoh(a_re, a_im, dt):
    mag = jnp.exp(a_re * dt)
    return mag * jnp.cos(a_im * dt), mag * jnp.sin(a_im * dt)


def _s5_ops_kernel(are_ref, aim_ref, ldt_ref, bre_ref, bim_ref, cre_ref, cim_ref, arp_ref, aip_ref, ldtp_ref,
                   w_ref, pt_ref, q_ref, lamr_ref, lami_ref, rows_scr):
    t, gi = S5_T, S5_GROUP
    ns = are_ref.shape[1]
    kk = t * gi

    pr, pi = _zoh(arp_ref[...], aip_ref[...], jnp.exp(ldtp_ref[...]))
    for _ in range(t.bit_length() - 1):
        pr, pi = _cmul(pr, pi, pr, pi)
    lamr_ref[...] = pr
    lami_ref[...] = pi

    ar = are_ref[...]
    ai = aim_ref[...]
    l1r, l1i = _zoh(ar, ai, jnp.exp(ldt_ref[...]))
    nr = l1r - 1.0
    den = ar * ar + ai * ai
    quantities = [l1r, l1i]
    for _ in range(3):
        quantities += list(_cmul(quantities[-2], quantities[-1], quantities[-2], quantities[-1]))
    quantities += [(nr * ar + l1i * ai) / den, (l1i * ar - nr * ai) / den]
    for k, val in enumerate(quantities):
        rows_scr[k] = val

    eye = lax.broadcasted_iota(jnp.int32, (ns, ns), 0) == lax.broadcasted_iota(jnp.int32, (ns, ns), 1)
    lane = lax.broadcasted_iota(jnp.int32, (ns, kk), 1)
    tau = lax.shift_right_logical(lane, 4)
    bits = [(lax.shift_right_logical(tau, b) & 1) == 1 for b in range(4)]
    expand = (lax.broadcasted_iota(jnp.int32, (gi, kk), 0)
              == (lax.broadcasted_iota(jnp.int32, (gi, kk), 1) & (gi - 1))).astype(F32)
    lane_w = lax.broadcasted_iota(jnp.int32, (gi, LANES), 1)
    tn_dims = (((0,), (0,)), ((), ()))

    def column(k, g):
        row = rows_scr[k, pl.ds(g, 1), :]
        return jnp.sum(jnp.where(eye, row, 0.0), axis=1, keepdims=True)

    def lane_powers(cols, flip):
        pr = pi = None
        for b in range(4):
            on = jnp.logical_not(bits[b]) if flip else bits[b]
            fr = jnp.where(on, cols[2 * b], 1.0)
            fi = jnp.where(on, cols[2 * b + 1], 0.0)
            pr, pi = (fr, fi) if pr is None else _cmul(pr, pi, fr, fi)
        return pr, pi

    def build(g, odd):
        cols = [column(k, g) for k in range(10)]
        cr, ci = cols[8], cols[9]
        bbr = cr * bre_ref[g] - ci * bim_ref[g]
        bbi = cr * bim_ref[g] + ci * bre_ref[g]
        pwr, pwi = lane_powers(cols, False)
        rvr, rvi = lane_powers(cols, True)
        p1r, p1i = _cmul(pwr, pwi, cols[0], cols[1])
        ctr = lax.dot_general(cre_ref[g], expand, tn_dims, precision=HIGHEST, preferred_element_type=F32)
        cti = lax.dot_general(cim_ref[g], expand, tn_dims, precision=HIGHEST, preferred_element_type=F32)
        bbtr = jnp.dot(bbr, expand, precision=HIGHEST, preferred_element_type=F32)
        bbti = jnp.dot(bbi, expand, precision=HIGHEST, preferred_element_type=F32)
        clr, cli = _cmul(ctr, cti, pwr, pwi)
        kt = (lax.dot_general(bbr, clr, tn_dims, precision=HIGHEST, preferred_element_type=F32)
              - lax.dot_general(bbi, cli, tn_dims, precision=HIGHEST, preferred_element_type=F32))
        gm = g % 8

        def rot_halves(v):
            return jnp.concatenate([_rot_blocks(v[:, :LANES], gm), _rot_blocks(v[:, LANES:], gm)], axis=1)

        k0, k1 = kt[:, :LANES], kt[:, LANES:]
        for s in range(t):
            sh = (gi * s) % LANES
            r0 = pltpu.roll(k0, sh, 1) if sh else k0
            if gi * s < LANES:
                r1 = pltpu.roll(k1, sh, 1) if sh else k1
                lo = jnp.where(lane_w >= sh, r0, 0.0)
                hi = jnp.where(lane_w >= sh, r1, r0)
            else:
                lo = jnp.zeros_like(k0)
                hi = jnp.where(lane_w >= sh, r0, 0.0)
            row0 = gi * (8 * (s // 8) + (s + gm) % 8)
            w_ref[g, row0:row0 + gi, :] = rot_halves(jnp.concatenate([lo, hi], axis=1)).astype(w_ref.dtype)
        ptr, pti = _cmul(rvr, rvi, bbtr, bbti)
        qr, qi = _cmul(ctr, cti, p1r, p1i)
        zero = jnp.zeros((ns, kk), pt_ref.dtype)
        for ref, re, im in ((pt_ref, ptr, pti), (q_ref, qr, -qi)):
            ref[g, odd * ns:(odd + 1) * ns, :] = rot_halves(re).astype(ref.dtype)
            ref[g, (1 - odd) * ns:(2 - odd) * ns, :] = zero
            ref[g, (2 + odd) * ns:(3 + odd) * ns, :] = rot_halves(im).astype(ref.dtype)
            ref[g, (3 - odd) * ns:(4 - odd) * ns, :] = zero

    for g in range(S5_OCT):
        build(g, g % 2)


def _s5_operators(a_re, a_im, log_dt, b_re, b_im, c_re, c_im):
    g, ns = a_re.shape
    t, gi = S5_T, S5_GROUP
    kk = t * gi
    assert t == 16 and 4 * ns == kk and 2 * ns == LANES
    noct = g // S5_OCT
    npair = S5_OCT // 2
    a_re_p = a_re.reshape(g // 2, 2 * ns)
    a_im_p = a_im.reshape(g // 2, 2 * ns)
    ldt_p = jnp.broadcast_to(log_dt[:, None], (g, ns)).reshape(g // 2, 2 * ns)
    oct3 = lambda i: (i, 0, 0)
    ops_shape = jax.ShapeDtypeStruct((g, kk, kk), BF16)
    lam_shape = jax.ShapeDtypeStruct((noct, npair, LANES), F32)
    return pl.pallas_call(
        _s5_ops_kernel,
        grid=(noct,),
        in_specs=[
            pl.BlockSpec((S5_OCT, ns), lambda i: (i, 0)),
            pl.BlockSpec((S5_OCT, ns), lambda i: (i, 0)),
            pl.BlockSpec((S5_OCT, 1), lambda i: (i, 0)),
            pl.BlockSpec((S5_OCT, ns, gi), oct3),
            pl.BlockSpec((S5_OCT, ns, gi), oct3),
            pl.BlockSpec((S5_OCT, gi, ns), oct3),
            pl.BlockSpec((S5_OCT, gi, ns), oct3),
            pl.BlockSpec((npair, LANES), lambda i: (i, 0)),
            pl.BlockSpec((npair, LANES), lambda i: (i, 0)),
            pl.BlockSpec((npair, LANES), lambda i: (i, 0)),
        ],
        out_specs=[pl.BlockSpec((S5_OCT, kk, kk), oct3)] * 3 + [pl.BlockSpec((None, npair, LANES), oct3)] * 2,
        out_shape=[ops_shape] * 3 + [lam_shape] * 2,
        scratch_shapes=[pltpu.VMEM((10, S5_OCT, ns), F32)],
        compiler_params=_cparams(("arbitrary",), 2 * 3 * S5_OCT * kk * kk * 2 + 4 * S5_OCT * ns * LANES * 4 * 2),
        name="s5_ops",
    )(a_re, a_im, log_dt.reshape(g, 1), b_re, b_im, c_re, c_im, a_re_p, a_im_p, ldt_p)


def _glu_kernel(ya_ref, w_ref, b_ref, yc_ref, z_ref, o_ref):
    acc = jnp.dot(ya_ref[...], w_ref[...].astype(BF16), preferred_element_type=F32) + b_ref[...]
    o_ref[...] = (yc_ref[...].astype(F32) * _sigmoid(acc) * _silu(z_ref[...].astype(F32))).astype(o_ref.dtype)


def _glu(yact, w_glu, layer, b_glu, uz, tm=1024, tn=512):
    l, e = yact.shape
    tm = min(tm, l)
    zoff = e // tn
    vmem = 2 * (tm * e * 2 + e * tn * 4 + 3 * tm * tn * 2)
    return pl.pallas_call(
        _glu_kernel,
        grid=(l // tm, e // tn),
        in_specs=[pl.BlockSpec((tm, e), lambda i, j: (i, 0)),
                  pl.BlockSpec((None, e, tn), lambda i, j: (layer, 0, j)),
                  pl.BlockSpec((1, tn), lambda i, j: (0, j)),
                  pl.BlockSpec((tm, tn), lambda i, j: (i, j)),
                  pl.BlockSpec((tm, tn), lambda i, j: (i, zoff + j))],
        out_specs=pl.BlockSpec((tm, tn), lambda i, j: (i, j)),
        out_shape=jax.ShapeDtypeStruct((l, e), BF16),
        compiler_params=_cparams(("arbitrary", "arbitrary"), vmem),
        name="s5_glu",
    )(yact, w_glu, b_glu, yact, uz)


def _gla_layer(x, scale, shift, gate, ln_g, ln_b, alpha, w_in, layer, gate_w2, gate_b, norm_g, w_out):
    d = x.shape[1]
    e = w_out.shape[0]
    qk = gate_w2.shape[1]
    wg = jnp.pad(w_in[layer, :, 2 * qk + 2 * e:], ((0, 0), (0, LANES - GLA_GATE_RANK)))
    w2p = jnp.pad(gate_w2, ((0, LANES - GLA_GATE_RANK), (0, 0)))
    proj, glr = _in_proj_gate(x, scale, shift, jnp.swapaxes(w_in, 1, 2), layer, wg)
    y = _gla_core(proj, glr, w2p, gate_b.reshape(1, qk), norm_g.reshape(1, e), e, qk)
    return _out_ln(y, w_out.astype(BF16), x, gate, ln_g.reshape(1, d), ln_b.reshape(1, d), alpha)


def _s5_layer(x, scale, shift, gate, ln_g, ln_b, alpha, w_in, layer, a_re, a_im, log_dt, b_re, b_im, c_re, c_im,
              d_skip, w_glu, b_glu, w_out):
    l, d = x.shape
    e = w_out.shape[0]
    t = S5_T
    uz = _in_proj_perm(x.reshape(l // t, t * d), t, scale, shift, w_in, layer)
    wt, pm, qm, lam_r, lam_i = _s5_operators(a_re, a_im, log_dt, b_re, b_im, c_re, c_im)
    yact = _ssm_core(uz, wt, pm, qm, lam_r, lam_i, d_skip.reshape(1, e), e)
    yglu = _glu(yact, w_glu, layer, b_glu.reshape(1, e), uz)
    return _out_ln(yglu, w_out.astype(BF16), x, gate, ln_g.reshape(1, d), ln_b.reshape(1, d), alpha,
                   step_major_t=t)


def kernel(x, c, ln_g, ln_b, ada_w, ada_b, gla_w_in, gla_gate_w2, gla_gate_b, gla_norm_g, gla_w_out,
           s5_w_in, s5_a_re, s5_a_im, s5_log_dt, s5_b_re, s5_b_im, s5_c_re, s5_c_im, s5_d,
           s5_w_glu, s5_b_glu, s5_w_out):
    bsz, l, d = x.shape
    assert bsz == 1, "batch 1 only"
    depth = ln_g.shape[0]
    alpha = (2 * depth) ** 0.25
    mod = _adaln(c, ada_w, ada_b)
    h = x.reshape(l, d)
    for i in range(depth):
        shift, scale, gate = mod[i, :, :d], mod[i, :, d:2 * d], mod[i, :, 2 * d:]
        j = i // 2
        if i % 2 == 0:
            h = _gla_layer(h, scale, shift, gate, ln_g[i], ln_b[i], alpha, gla_w_in, j, gla_gate_w2[j],
                           gla_gate_b[j], gla_norm_g[j], gla_w_out[j])
        else:
            h = _s5_layer(h, scale, shift, gate, ln_g[i], ln_b[i], alpha, s5_w_in, j, s5_a_re[j], s5_a_im[j],
                          s5_log_dt[j], s5_b_re[j], s5_b_im[j], s5_c_re[j], s5_c_im[j], s5_d[j],
                          s5_w_glu, s5_b_glu[j], s5_w_out[j])
    return h.reshape(bsz, l, d)
```

```python
import functools
import math

import jax
import jax.numpy as jnp
from jax import lax
from jax.experimental import pallas as pl
from jax.experimental.pallas import tpu as pltpu

F32 = jnp.float32
BF16 = jnp.bfloat16
U32 = jnp.uint32
HIGHEST = lax.Precision.HIGHEST

CHUNK = 64
GLA_HEADS = 4
GLA_GATE_RANK = 16
GLA_TAU = 16.0
S5_GROUP = 16
S5_STATE = 64
LN_EPS = 1e-5
RMS_EPS = 1e-6

S5_T = 16
S5_OCT = 16
GLA_STEP_CHUNKS = 4
LANES = 128
V7X_SCOPED_VMEM_CAP = 60000 * 1024


def _cparams(semantics, vmem_bytes):
    limit = min(int(vmem_bytes) + (6 << 20), V7X_SCOPED_VMEM_CAP)
    return pltpu.CompilerParams(dimension_semantics=semantics, vmem_limit_bytes=limit)


def _sigmoid(x):
    return 1.0 / (1.0 + jnp.exp(-x))


def _silu(x):
    return x * _sigmoid(x)


def _gelu_tanh(y):
    cdf = 0.5 * (1.0 + jnp.tanh(math.sqrt(2.0 / math.pi) * (y + 0.044715 * (y * y * y))))
    return y * cdf


def _adaln_kernel(c_ref, w_ref, b_ref, o_ref):
    c = c_ref[...]
    o_ref[...] = jnp.sum(_silu(c) * w_ref[...], axis=0, keepdims=True) + b_ref[...]


def _adaln(c, ada_w, ada_b, tn=1024):
    depth, d, n3 = ada_w.shape
    assert c.shape == (1, d), "batch 1 only"
    c_col = c.reshape(d, 1)
    out = pl.pallas_call(
        _adaln_kernel,
        grid=(depth, n3 // tn),
        in_specs=[
            pl.BlockSpec((d, 1), lambda l, j: (0, 0)),
            pl.BlockSpec((None, d, tn), lambda l, j: (l, 0, j)),
            pl.BlockSpec((None, 1, tn), lambda l, j: (l, 0, j)),
        ],
        out_specs=pl.BlockSpec((None, 1, tn), lambda l, j: (l, 0, j)),
        out_shape=jax.ShapeDtypeStruct((depth, 1, n3), F32),
        compiler_params=_cparams(("arbitrary", "arbitrary"), 2 * d * tn * 4 + d * LANES * 4),
        name="adaln_mod",
    )(c_col, ada_w, ada_b.reshape(depth, 1, n3))
    return out


def _modulate(x_ref, sc_ref, sh_ref):
    return (x_ref[...] * (1.0 + sc_ref[...]) + sh_ref[...]).astype(BF16)


def _proj_gate_kernel(x_ref, sc_ref, sh_ref, w_ref, wg_ref, o_ref, g_ref, u_scr):
    @pl.when(pl.program_id(1) == 0)
    def _():
        u = _modulate(x_ref, sc_ref, sh_ref)
        u_scr[...] = u
        g_ref[...] = jnp.dot(u, wg_ref[...].astype(BF16), preferred_element_type=F32)

    o_ref[...] = lax.dot_general(u_scr[...], w_ref[...].astype(BF16), (((1,), (1,)), ((), ())),
                                 preferred_element_type=F32).astype(o_ref.dtype)


def _proj_perm_kernel(xa_ref, xb_ref, sc_ref, sh_ref, w_ref, o_ref, u_scr):
    nc = xa_ref.shape[0]

    @pl.when(pl.program_id(1) == 0)
    def _():
        u_scr[:nc] = _modulate(xa_ref, sc_ref, sh_ref)
        u_scr[nc:] = _modulate(xb_ref, sc_ref, sh_ref)

    o_ref[...] = jnp.dot(u_scr[...], w_ref[...].astype(BF16), preferred_element_type=F32).astype(o_ref.dtype)


def _in_proj_gate(x, scale, shift, wt, layer, wg, tm=1024, tn=1024):
    l, d = x.shape
    n = (wt.shape[1] // tn) * tn
    ng = wg.shape[1]
    tm = min(tm, l)
    vmem = 2 * tm * d * 4 + 2 * d * tn * 4 + 2 * tm * tn * 2 + tm * d * 2 + 2 * d * ng * 4 + 2 * tm * ng * 4
    return pl.pallas_call(
        _proj_gate_kernel,
        grid=(l // tm, n // tn),
        in_specs=[
            pl.BlockSpec((tm, d), lambda i, j: (i, 0)),
            pl.BlockSpec((1, d), lambda i, j: (0, 0)),
            pl.BlockSpec((1, d), lambda i, j: (0, 0)),
            pl.BlockSpec((None, tn, d), lambda i, j: (layer, j, 0)),
            pl.BlockSpec((d, ng), lambda i, j: (0, 0)),
        ],
        out_specs=[pl.BlockSpec((tm, tn), lambda i, j: (i, j)),
                   pl.BlockSpec((tm, ng), lambda i, j: (i, 0))],
        out_shape=[jax.ShapeDtypeStruct((l, n), BF16), jax.ShapeDtypeStruct((l, ng), F32)],
        scratch_shapes=[pltpu.VMEM((tm, d), BF16)],
        compiler_params=_cparams(("arbitrary", "arbitrary"), vmem),
        name="in_proj_gate",
    )(x, scale, shift, wt, wg)


def _in_proj_perm(x2, t, scale, shift, w, layer, tn=1024):
    nc = x2.shape[0]
    d = x2.shape[1] // t
    n = w.shape[2]
    tm = 2 * nc
    vmem = 2 * tm * d * 4 + 2 * d * tn * 4 + 2 * tm * tn * 2 + tm * d * 2
    return pl.pallas_call(
        _proj_perm_kernel,
        grid=(t // 2, n // tn),
        in_specs=[
            pl.BlockSpec((nc, d), lambda i, j: (0, 2 * i)),
            pl.BlockSpec((nc, d), lambda i, j: (0, 2 * i + 1)),
            pl.BlockSpec((1, d), lambda i, j: (0, 0)),
            pl.BlockSpec((1, d), lambda i, j: (0, 0)),
            pl.BlockSpec((None, d, tn), lambda i, j: (layer, 0, j)),
        ],
        out_specs=pl.BlockSpec((tm, tn), lambda i, j: (i, j)),
        out_shape=jax.ShapeDtypeStruct((nc * t, n), BF16),
        scratch_shapes=[pltpu.VMEM((tm, d), BF16)],
        compiler_params=_cparams(("arbitrary", "arbitrary"), vmem),
        name="in_proj_perm",
    )(x2, x2, scale, shift, w)


def _gla_core_kernel(*refs, dk, dv):
    nh = GLA_HEADS
    q_ref, k_ref = refs[0], refs[1]
    v_refs = refs[2:2 + nh]
    z_refs = refs[2 + nh:2 + 2 * nh]
    g_ref, w2_ref, gb_ref, ng_ref, o_ref, st_ref, kd_scr, ko_scr, qd_scr = refs[2 + 2 * nh:]

    @pl.when(pl.program_id(0) == 0)
    def _():
        st_ref[...] = jnp.zeros_like(st_ref)

    c = CHUNK
    r = q_ref.shape[0]
    sb = r // c
    nt = (((1,), (1,)), ((), ()))
    pre = jnp.dot(g_ref[...].astype(BF16), w2_ref[...], preferred_element_type=F32) + gb_ref[...]
    la = (jnp.minimum(pre, 0.0) - jnp.log(1.0 + jnp.exp(-jnp.abs(pre)))) * (1.0 / GLA_TAU)
    row = lax.broadcasted_iota(jnp.int32, (r, r), 0)
    col = lax.broadcasted_iota(jnp.int32, (r, r), 1)
    tri = (col <= row).astype(BF16)
    la_hi = la.astype(BF16)
    la_lo = (la - la_hi.astype(F32)).astype(BF16)
    gc = (jnp.dot(tri, la_hi, preferred_element_type=F32) + jnp.dot(tri, la_lo, preferred_element_type=F32))
    ends = [gc[c * (j + 1) - 1:c * (j + 1), :] for j in range(sb)]
    e_rows = jnp.concatenate([jnp.broadcast_to(ends[j], (c, gc.shape[1])) for j in range(sb)], axis=0)
    kf = k_ref[...].astype(F32) * jnp.exp(e_rows - gc)
    kd_scr[...] = kf.astype(BF16)
    ko_scr[...] = (kf * jnp.exp(ends[-1] - e_rows)).astype(BF16)
    qd_scr[...] = (q_ref[...].astype(F32) * jnp.exp(e_rows)).astype(BF16)
    dec = jnp.exp(ends[-1])
    for h in range(nh):
        ks = slice(h * dk, (h + 1) * dk)
        vs = slice(h * dv, (h + 1) * dv)
        st_in = st_ref[h]
        base = lax.dot_general(qd_scr[:, ks], st_in.astype(BF16), nt, preferred_element_type=F32)
        ams = []
        for m in range(sb):
            qs = jnp.concatenate(
                [(q_ref[c * j:c * (j + 1), ks].astype(F32) * jnp.exp(ends[j][:, ks] - ends[m][:, ks])).astype(BF16)
                 if j > m else q_ref[c * j:c * (j + 1), ks] for j in range(m, sb)], axis=0)
            kpad = jnp.concatenate(
                ([jnp.zeros((c * m, dk), BF16)] if m else []) + [kd_scr[c * m:c * (m + 1), ks]]
                + ([jnp.zeros((c * (sb - 1 - m), dk), BF16)] if m < sb - 1 else []), axis=0)
            ams.append(lax.dot_general(qs, kpad, nt, preferred_element_type=F32))
        a_rows = []
        for j in range(sb):
            aj = ams[0][c * j:c * (j + 1)]
            for m in range(1, j + 1):
                aj = aj + ams[m][c * (j - m):c * (j - m + 1)]
            a_rows.append(aj)
        a_full = jnp.concatenate(a_rows, axis=0).astype(BF16)
        o = base + jnp.dot(a_full, v_refs[h][...], preferred_element_type=F32)
        upd = lax.dot_general(v_refs[h][...], ko_scr[:, ks], (((0,), (0,)), ((), ())),
                              preferred_element_type=F32)
        st_ref[h] = dec[:, ks] * st_in + upd
        o = o * (dk ** -0.5)
        o = o * lax.rsqrt(jnp.mean(o * o, axis=-1, keepdims=True) + RMS_EPS)
        y = o * ng_ref[:, vs] * _silu(z_refs[h][...].astype(F32))
        o_ref[:, vs] = y.astype(o_ref.dtype)


def _gla_core(proj, glr, w2p, gate_b, norm_g, e, qk):
    l = proj.shape[0]
    nh = GLA_HEADS
    dk, dv = qk // nh, e // nh
    assert (2 * qk) % dv == 0
    v0 = 2 * qk // dv
    ng = glr.shape[1]
    c = min(GLA_STEP_CHUNKS * CHUNK, l)
    assert c % CHUNK == 0 and l % c == 0
    kern = functools.partial(_gla_core_kernel, dk=dk, dv=dv)
    vmem = (2 * (2 * c * e * 2 + 2 * c * qk * 2 + c * ng * 4 + ng * qk * 2 + c * e * 2) + nh * dv * dk * 4
            + 12 * c * qk * 4 + 4 * c * dv * 4)
    head_specs = [pl.BlockSpec((c, dv), functools.partial(lambda n, b: (n, b), b=v0 + h)) for h in range(2 * nh)]
    return pl.pallas_call(
        kern,
        grid=(l // c,),
        in_specs=[
            pl.BlockSpec((c, qk), lambda n: (n, 0)),
            pl.BlockSpec((c, qk), lambda n: (n, 1)),
            *head_specs,
            pl.BlockSpec((c, ng), lambda n: (n, 0)),
            pl.BlockSpec((ng, qk), lambda n: (0, 0)),
            pl.BlockSpec((1, qk), lambda n: (0, 0)),
            pl.BlockSpec((1, e), lambda n: (0, 0)),
        ],
        out_specs=pl.BlockSpec((c, e), lambda n: (n, 0)),
        out_shape=jax.ShapeDtypeStruct((l, e), BF16),
        scratch_shapes=[pltpu.VMEM((nh, dv, dk), F32)] + [pltpu.VMEM((c, qk), BF16) for _ in range(3)],
        compiler_params=_cparams(("arbitrary",), vmem),
        name="gla_core",
    )(*([proj] * (2 + 2 * nh)), glr, w2p, gate_b, norm_g)


def _out_ln_kernel(y_ref, w_ref, x_ref, gate_ref, g_ref, b_ref, o_ref, acc_ref, *, alpha):
    k = pl.program_id(1)

    @pl.when(k == 0)
    def _():
        acc_ref[...] = jnp.zeros_like(acc_ref)

    acc_ref[...] += jnp.dot(y_ref[...], w_ref[...], preferred_element_type=F32)

    @pl.when(k == pl.num_programs(1) - 1)
    def _():
        r = alpha * x_ref[...] + (1.0 + gate_ref[...]) * acc_ref[...]
        mu = jnp.mean(r, axis=-1, keepdims=True)
        rc = r - mu
        var = jnp.mean(rc * rc, axis=-1, keepdims=True)
        o_ref[...] = rc * lax.rsqrt(var + LN_EPS) * g_ref[...] + b_ref[...]


def _out_ln(y, w, x, gate, ln_g, ln_b, alpha, step_major_t=None, tm=512, tk=1024):
    l, e = y.shape
    d = w.shape[1]
    kern = functools.partial(_out_ln_kernel, alpha=alpha)
    if step_major_t is None:
        tm = min(tm, l)
        x_in, out_shape = x, (l, d)
        x_spec = pl.BlockSpec((tm, d), lambda i, k: (i, 0))
    else:
        t = step_major_t
        tm = l // t
        x_in, out_shape = x.reshape(tm, t * d), (tm, t * d)
        x_spec = pl.BlockSpec((tm, d), lambda i, k: (0, i))
    vmem = 2 * tm * tk * 2 + 2 * tk * d * 2 + 4 * tm * d * 4 + tm * d * 4
    out = pl.pallas_call(
        kern,
        grid=(l // tm, e // tk),
        in_specs=[
            pl.BlockSpec((tm, tk), lambda i, k: (i, k)),
            pl.BlockSpec((tk, d), lambda i, k: (k, 0)),
            x_spec,
            pl.BlockSpec((1, d), lambda i, k: (0, 0)),
            pl.BlockSpec((1, d), lambda i, k: (0, 0)),
            pl.BlockSpec((1, d), lambda i, k: (0, 0)),
        ],
        out_specs=x_spec,
        out_shape=jax.ShapeDtypeStruct(out_shape, F32),
        scratch_shapes=[pltpu.VMEM((tm, d), F32)],
        compiler_params=_cparams(("arbitrary", "arbitrary"), vmem),
        name="out_proj_ln",
    )(y, w, x_in, gate, ln_g, ln_b)
    return out.reshape(l, d)


def _rot_blocks(v, nblk):
    nblk %= 8
    return pltpu.roll(v, S5_GROUP * nblk, 1) if nblk else v


def _skew_select(cols):
    blk = lax.shift_right_logical(lax.broadcasted_iota(jnp.int32, cols[0].shape, 1), 4)
    masks = [blk == j for j in range(8)]
    out = []
    for a in range(8):
        r = cols[(-a) % 8]
        for j in range(1, 8):
            r = jnp.where(masks[j], cols[(j - a) % 8], r)
        out.append(r)
    return out


def _ssm_kernel(x_ref, w_ref, p_ref, q_ref, lr_ref, li_ref, d_ref, o_ref, a_scr, y_scr, vre, vim, sre, sim):
    nc = x_ref.shape[1]
    npair = S5_OCT // 2

    rb = min(64, nc)

    def gather_steps(it, carry):
        r = pl.ds(pl.multiple_of(it * rb, rb), rb)
        for s_hi in range(2):
            for g_hi in range(2):
                cols = [_rot_blocks(pltpu.bitcast(x_ref[8 * s_hi + m, r, g_hi * LANES:(g_hi + 1) * LANES], U32), m)
                        for m in range(8)]
                res = _skew_select(cols)
                for m in range(8):
                    a_scr[8 * g_hi + m, r, s_hi * LANES:(s_hi + 1) * LANES] = pltpu.bitcast(res[m], BF16)
        return carry

    lax.fori_loop(0, nc // rb, gather_steps, 0)
    a_bf = a_scr

    nt = (((1,), (1,)), ((), ()))
    for p in range(npair):
        v = (lax.dot_general(a_bf[2 * p], p_ref[2 * p], nt, preferred_element_type=F32)
             + lax.dot_general(a_bf[2 * p + 1], p_ref[2 * p + 1], nt, preferred_element_type=F32))
        vre[pl.ds(p, nc, stride=npair), :] = v[:, :LANES]
        vim[pl.ds(p, nc, stride=npair), :] = v[:, LANES:]

    ar = lr_ref[...]
    ai = li_ref[...]

    def step(c, carry):
        xr, xi = carry
        rows = pl.ds(pl.multiple_of(c * npair, npair), npair)
        sre[rows, :] = xr
        sim[rows, :] = xi
        nxr = ar * xr - ai * xi + vre[rows, :]
        nxi = ar * xi + ai * xr + vim[rows, :]
        return nxr, nxi

    zero = jnp.zeros((npair, LANES), F32)
    lax.fori_loop(0, nc, step, (zero, zero), unroll=8)

    for p in range(npair):
        s = jnp.concatenate([sre[pl.ds(p, nc, stride=npair), :], sim[pl.ds(p, nc, stride=npair), :]],
                            axis=1).astype(BF16)
        for g in (2 * p, 2 * p + 1):
            y_scr[g] = (jnp.dot(a_bf[g], w_ref[g], preferred_element_type=F32)
                        + jnp.dot(s, q_ref[g], preferred_element_type=F32))

    rbo = min(32, nc)

    def scatter_steps(it, carry):
        r = pl.ds(pl.multiple_of(it * rbo, rbo), rbo)
        for s_hi in range(2):
            for g_hi in range(2):
                cs = slice(g_hi * LANES, (g_hi + 1) * LANES)
                cols = [y_scr[8 * g_hi + m, r, s_hi * LANES:(s_hi + 1) * LANES] for m in range(8)]
                res = _skew_select(cols)
                dsk = d_ref[:, cs]
                for m in range(8):
                    s = 8 * s_hi + m
                    y = _rot_blocks(res[m], -m) + dsk * x_ref[s, r, cs].astype(F32)
                    o_ref[s, r, cs] = _gelu_tanh(y).astype(o_ref.dtype)
        return carry

    lax.fori_loop(0, nc // rbo, scatter_steps, 0)


def _ssm_core(uz, wt, pm, qm, lam_r, lam_i, d_skip, e):
    t = S5_T
    nc = uz.shape[0] // t
    kk = t * S5_GROUP
    assert kk == 2 * LANES and S5_OCT * S5_GROUP == kk
    noct = e // kk
    npair = S5_OCT // 2
    x3 = uz.reshape(t, nc, uz.shape[1])
    vmem = (2 * (2 * t * nc * kk * 2 + 3 * S5_OCT * kk * kk * 2) + S5_OCT * nc * kk * (2 + 4)
            + 4 * nc * npair * LANES * 4)
    out = pl.pallas_call(
        _ssm_kernel,
        grid=(noct,),
        in_specs=[
            pl.BlockSpec((t, nc, kk), lambda i: (0, 0, i)),
            pl.BlockSpec((S5_OCT, kk, kk), lambda i: (i, 0, 0)),
            pl.BlockSpec((S5_OCT, kk, kk), lambda i: (i, 0, 0)),
            pl.BlockSpec((S5_OCT, kk, kk), lambda i: (i, 0, 0)),
            pl.BlockSpec((None, npair, LANES), lambda i: (i, 0, 0)),
            pl.BlockSpec((None, npair, LANES), lambda i: (i, 0, 0)),
            pl.BlockSpec((1, kk), lambda i: (0, i)),
        ],
        out_specs=pl.BlockSpec((t, nc, kk), lambda i: (0, 0, i)),
        out_shape=jax.ShapeDtypeStruct((t, nc, e), BF16),
        scratch_shapes=[pltpu.VMEM((S5_OCT, nc, kk), BF16), pltpu.VMEM((S5_OCT, nc, kk), F32)]
        + [pltpu.VMEM((nc * npair, LANES), F32) for _ in range(4)],
        compiler_params=_cparams(("arbitrary",), vmem),
        name="s5_ssm",
    )(x3, wt, pm, qm, lam_r, lam_i, d_skip)
    return out.reshape(t * nc, e)


def _cmul(ar, ai, br, bi):
    return ar * br - ai * bi, ar * bi + ai * br


def _zoh(a_re, a_im, dt):
    mag = jnp.exp(a_re * dt)
    return mag * jnp.cos(a_im * dt), mag * jnp.sin(a_im * dt)


def _s5_ops_kernel(are_ref, aim_ref, ldt_ref, bre_ref, bim_ref, cre_ref, cim_ref, arp_ref, aip_ref, ldtp_ref,
                   w_ref, pt_ref, q_ref, lamr_ref, lami_ref, rows_scr):
    t, gi = S5_T, S5_GROUP
    ns = are_ref.shape[1]
    kk = t * gi

    pr, pi = _zoh(arp_ref[...], aip_ref[...], jnp.exp(ldtp_ref[...]))
    for _ in range(t.bit_length() - 1):
        pr, pi = _cmul(pr, pi, pr, pi)
    lamr_ref[...] = pr
    lami_ref[...] = pi

    ar = are_ref[...]
    ai = aim_ref[...]
    l1r, l1i = _zoh(ar, ai, jnp.exp(ldt_ref[...]))
    nr = l1r - 1.0
    den = ar * ar + ai * ai
    quantities = [l1r, l1i]
    for _ in range(3):
        quantities += list(_cmul(quantities[-2], quantities[-1], quantities[-2], quantities[-1]))
    quantities += [(nr * ar + l1i * ai) / den, (l1i * ar - nr * ai) / den]
    for k, val in enumerate(quantities):
        rows_scr[k] = val

    eye = lax.broadcasted_iota(jnp.int32, (ns, ns), 0) == lax.broadcasted_iota(jnp.int32, (ns, ns), 1)
    lane = lax.broadcasted_iota(jnp.int32, (ns, kk), 1)
    tau = lax.shift_right_logical(lane, 4)
    bits = [(lax.shift_right_logical(tau, b) & 1) == 1 for b in range(4)]
    expand = (lax.broadcasted_iota(jnp.int32, (gi, kk), 0)
              == (lax.broadcasted_iota(jnp.int32, (gi, kk), 1) & (gi - 1))).astype(F32)
    lane_w = lax.broadcasted_iota(jnp.int32, (gi, LANES), 1)
    tn_dims = (((0,), (0,)), ((), ()))

    def column(k, g):
        row = rows_scr[k, pl.ds(g, 1), :]
        return jnp.sum(jnp.where(eye, row, 0.0), axis=1, keepdims=True)

    def lane_powers(cols, flip):
        pr = pi = None
        for b in range(4):
            on = jnp.logical_not(bits[b]) if flip else bits[b]
            fr = jnp.where(on, cols[2 * b], 1.0)
            fi = jnp.where(on, cols[2 * b + 1], 0.0)
            pr, pi = (fr, fi) if pr is None else _cmul(pr, pi, fr, fi)
        return pr, pi

    def build(g, odd):
        cols = [column(k, g) for k in range(10)]
        cr, ci = cols[8], cols[9]
        bbr = cr * bre_ref[g] - ci * bim_ref[g]
        bbi = cr * bim_ref[g] + ci * bre_ref[g]
        pwr, pwi = lane_powers(cols, False)
        rvr, rvi = lane_powers(cols, True)
        p1r, p1i = _cmul(pwr, pwi, cols[0], cols[1])
        ctr = lax.dot_general(cre_ref[g], expand, tn_dims, precision=HIGHEST, preferred_element_type=F32)
        cti = lax.dot_general(cim_ref[g], expand, tn_dims, precision=HIGHEST, preferred_element_type=F32)
        bbtr = jnp.dot(bbr, expand, precision=HIGHEST, preferred_element_type=F32)
        bbti = jnp.dot(bbi, expand, precision=HIGHEST, preferred_element_type=F32)
        clr, cli = _cmul(ctr, cti, pwr, pwi)
        kt = (lax.dot_general(bbr, clr, tn_dims, precision=HIGHEST, preferred_element_type=F32)
              - lax.dot_general(bbi, cli, tn_dims, precision=HIGHEST, preferred_element_type=F32))
        gm = g % 8

        def rot_halves(v):
            return jnp.concatenate([_rot_blocks(v[:, :LANES], gm), _rot_blocks(v[:, LANES:], gm)], axis=1)

        k0, k1 = kt[:, :LANES], kt[:, LANES:]
        for s in range(t):
            sh = (gi * s) % LANES
            r0 = pltpu.roll(k0, sh, 1) if sh else k0
            if gi * s < LANES:
                r1 = pltpu.roll(k1, sh, 1) if sh else k1
                lo = jnp.where(lane_w >= sh, r0, 0.0)
                hi = jnp.where(lane_w >= sh, r1, r0)
            else:
                lo = jnp.zeros_like(k0)
                hi = jnp.where(lane_w >= sh, r0, 0.0)
            row0 = gi * (8 * (s // 8) + (s + gm) % 8)
            w_ref[g, row0:row0 + gi, :] = rot_halves(jnp.concatenate([lo, hi], axis=1)).astype(w_ref.dtype)
        ptr, pti = _cmul(rvr, rvi, bbtr, bbti)
        qr, qi = _cmul(ctr, cti, p1r, p1i)
        zero = jnp.zeros((ns, kk), pt_ref.dtype)
        for ref, re, im in ((pt_ref, ptr, pti), (q_ref, qr, -qi)):
            ref[g, odd * ns:(odd + 1) * ns, :] = rot_halves(re).astype(ref.dtype)
            ref[g, (1 - odd) * ns:(2 - odd) * ns, :] = zero
            ref[g, (2 + odd) * ns:(3 + odd) * ns, :] = rot_halves(im).astype(ref.dtype)
            ref[g, (3 - odd) * ns:(4 - odd) * ns, :] = zero

    for g in range(S5_OCT):
        build(g, g % 2)


def _s5_operators(a_re, a_im, log_dt, b_re, b_im, c_re, c_im):
    g, ns = a_re.shape
    t, gi = S5_T, S5_GROUP
    kk = t * gi
    assert t == 16 and 4 * ns == kk and 2 * ns == LANES
    noct = g // S5_OCT
    npair = S5_OCT // 2
    a_re_p = a_re.reshape(g // 2, 2 * ns)
    a_im_p = a_im.reshape(g // 2, 2 * ns)
    ldt_p = jnp.broadcast_to(log_dt[:, None], (g, ns)).reshape(g // 2, 2 * ns)
    oct3 = lambda i: (i, 0, 0)
    ops_shape = jax.ShapeDtypeStruct((g, kk, kk), BF16)
    lam_shape = jax.ShapeDtypeStruct((noct, npair, LANES), F32)
    return pl.pallas_call(
        _s5_ops_kernel,
        grid=(noct,),
        in_specs=[
            pl.BlockSpec((S5_OCT, ns), lambda i: (i, 0)),
            pl.BlockSpec((S5_OCT, ns), lambda i: (i, 0)),
            pl.BlockSpec((S5_OCT, 1), lambda i: (i, 0)),
            pl.BlockSpec((S5_OCT, ns, gi), oct3),
            pl.BlockSpec((S5_OCT, ns, gi), oct3),
            pl.BlockSpec((S5_OCT, gi, ns), oct3),
            pl.BlockSpec((S5_OCT, gi, ns), oct3),
            pl.BlockSpec((npair, LANES), lambda i: (i, 0)),
            pl.BlockSpec((npair, LANES), lambda i: (i, 0)),
            pl.BlockSpec((npair, LANES), lambda i: (i, 0)),
        ],
        out_specs=[pl.BlockSpec((S5_OCT, kk, kk), oct3)] * 3 + [pl.BlockSpec((None, npair, LANES), oct3)] * 2,
        out_shape=[ops_shape] * 3 + [lam_shape] * 2,
        scratch_shapes=[pltpu.VMEM((10, S5_OCT, ns), F32)],
        compiler_params=_cparams(("arbitrary",), 2 * 3 * S5_OCT * kk * kk * 2 + 4 * S5_OCT * ns * LANES * 4 * 2),
        name="s5_ops",
    )(a_re, a_im, log_dt.reshape(g, 1), b_re, b_im, c_re, c_im, a_re_p, a_im_p, ldt_p)


def _glu_kernel(ya_ref, w_ref, b_ref, yc_ref, z_ref, o_ref):
    acc = jnp.dot(ya_ref[...], w_ref[...].astype(BF16), preferred_element_type=F32) + b_ref[...]
    o_ref[...] = (yc_ref[...].astype(F32) * _sigmoid(acc) * _silu(z_ref[...].astype(F32))).astype(o_ref.dtype)


def _glu(yact, w_glu, layer, b_glu, uz, tm=1024, tn=512):
    l, e = yact.shape
    tm = min(tm, l)
    zoff = e // tn
    vmem = 2 * (tm * e * 2 + e * tn * 4 + 3 * tm * tn * 2)
    return pl.pallas_call(
        _glu_kernel,
        grid=(l // tm, e // tn),
        in_specs=[pl.BlockSpec((tm, e), lambda i, j: (i, 0)),
                  pl.BlockSpec((None, e, tn), lambda i, j: (layer, 0, j)),
                  pl.BlockSpec((1, tn), lambda i, j: (0, j)),
                  pl.BlockSpec((tm, tn), lambda i, j: (i, j)),
                  pl.BlockSpec((tm, tn), lambda i, j: (i, zoff + j))],
        out_specs=pl.BlockSpec((tm, tn), lambda i, j: (i, j)),
        out_shape=jax.ShapeDtypeStruct((l, e), BF16),
        compiler_params=_cparams(("arbitrary", "arbitrary"), vmem),
        name="s5_glu",
    )(yact, w_glu, b_glu, yact, uz)


def _gla_layer(x, scale, shift, gate, ln_g, ln_b, alpha, w_in, layer, gate_w2, gate_b, norm_g, w_out):
    d = x.shape[1]
    e = w_out.shape[0]
    qk = gate_w2.shape[1]
    wg = jnp.pad(w_in[layer, :, 2 * qk + 2 * e:], ((0, 0), (0, LANES - GLA_GATE_RANK)))
    w2p = jnp.pad(gate_w2, ((0, LANES - GLA_GATE_RANK), (0, 0))).astype(BF16)
    proj, glr = _in_proj_gate(x, scale, shift, jnp.swapaxes(w_in, 1, 2), layer, wg)
    y = _gla_core(proj, glr, w2p, gate_b.reshape(1, qk), norm_g.reshape(1, e), e, qk)
    return _out_ln(y, w_out.astype(BF16), x, gate, ln_g.reshape(1, d), ln_b.reshape(1, d), alpha)


def _s5_layer(x, scale, shift, gate, ln_g, ln_b, alpha, w_in, layer, a_re, a_im, log_dt, b_re, b_im, c_re, c_im,
              d_skip, w_glu, b_glu, w_out):
    l, d = x.shape
    e = w_out.shape[0]
    t = S5_T
    uz = _in_proj_perm(x.reshape(l // t, t * d), t, scale, shift, w_in, layer)
    wt, pm, qm, lam_r, lam_i = _s5_operators(a_re, a_im, log_dt, b_re, b_im, c_re, c_im)
    yact = _ssm_core(uz, wt, pm, qm, lam_r, lam_i, d_skip.reshape(1, e), e)
    yglu = _glu(yact, w_glu, layer, b_glu.reshape(1, e), uz)
    return _out_ln(yglu, w_out.astype(BF16), x, gate, ln_g.reshape(1, d), ln_b.reshape(1, d), alpha,
                   step_major_t=t)


def kernel(x, c, ln_g, ln_b, ada_w, ada_b, gla_w_in, gla_gate_w2, gla_gate_b, gla_norm_g, gla_w_out,
           s5_w_in, s5_a_re, s5_a_im, s5_log_dt, s5_b_re, s5_b_im, s5_c_re, s5_c_im, s5_d,
           s5_w_glu, s5_b_glu, s5_w_out):
    bsz, l, d = x.shape
    assert bsz == 1, "batch 1 only"
    depth = ln_g.shape[0]
    alpha = (2 * depth) ** 0.25
    mod = _adaln(c, ada_w, ada_b)
    h = x.reshape(l, d)
    for i in range(depth):
        shift, scale, gate = mod[i, :, :d], mod[i, :, d:2 * d], mod[i, :, 2 * d:]
        j = i // 2
        if i % 2 == 0:
            h = _gla_layer(h, scale, shift, gate, ln_g[i], ln_b[i], alpha, gla_w_in, j, gla_gate_w2[j],
                           gla_gate_b[j], gla_norm_g[j], gla_w_out[j])
        else:
            h = _s5_layer(h, scale, shift, gate, ln_g[i], ln_b[i], alpha, s5_w_in, j, s5_a_re[j], s5_a_im[j],
                          s5_log_dt[j], s5_b_re[j], s5_b_im[j], s5_c_re[j], s5_c_im[j], s5_d[j],
                          s5_w_glu, s5_b_glu[j], s5_w_out[j])
    return h.reshape(bsz, l, d)
```

```python
import functools
import math

import jax
import jax.numpy as jnp
from jax import lax
from jax.experimental import pallas as pl
from jax.experimental.pallas import tpu as pltpu

F32 = jnp.float32
BF16 = jnp.bfloat16
U32 = jnp.uint32
HIGHEST = lax.Precision.HIGHEST

CHUNK = 64
GLA_HEADS = 4
GLA_GATE_RANK = 16
GLA_TAU = 16.0
S5_GROUP = 16
S5_STATE = 64
LN_EPS = 1e-5
RMS_EPS = 1e-6

S5_T = 16
S5_OCT = 16
GLA_STEP_CHUNKS = 4
S5_TILE_BLOCKS = 64
LANES = 128
V7X_SCOPED_VMEM_CAP = 60000 * 1024


def _cparams(semantics, vmem_bytes):
    limit = min(int(vmem_bytes) + (6 << 20), V7X_SCOPED_VMEM_CAP)
    return pltpu.CompilerParams(dimension_semantics=semantics, vmem_limit_bytes=limit)


def _sigmoid(x):
    return 1.0 / (1.0 + jnp.exp(-x))


def _silu(x):
    return x * _sigmoid(x)


def _gelu_tanh(y):
    cdf = 0.5 * (1.0 + jnp.tanh(math.sqrt(2.0 / math.pi) * (y + 0.044715 * (y * y * y))))
    return y * cdf


def _adaln_kernel(c_ref, w_ref, b_ref, o_ref):
    c = c_ref[...]
    o_ref[...] = jnp.sum(_silu(c) * w_ref[...], axis=0, keepdims=True) + b_ref[...]


def _adaln(c, ada_w, ada_b, tn=1024):
    depth, d, n3 = ada_w.shape
    assert c.shape == (1, d), "batch 1 only"
    c_col = c.reshape(d, 1)
    out = pl.pallas_call(
        _adaln_kernel,
        grid=(depth, n3 // tn),
        in_specs=[
            pl.BlockSpec((d, 1), lambda l, j: (0, 0)),
            pl.BlockSpec((None, d, tn), lambda l, j: (l, 0, j)),
            pl.BlockSpec((None, 1, tn), lambda l, j: (l, 0, j)),
        ],
        out_specs=pl.BlockSpec((None, 1, tn), lambda l, j: (l, 0, j)),
        out_shape=jax.ShapeDtypeStruct((depth, 1, n3), F32),
        compiler_params=_cparams(("arbitrary", "arbitrary"), 2 * d * tn * 4 + d * LANES * 4),
        name="adaln_mod",
    )(c_col, ada_w, ada_b.reshape(depth, 1, n3))
    return out


def _modulate(x_ref, sc_ref, sh_ref):
    return (x_ref[...] * (1.0 + sc_ref[...]) + sh_ref[...]).astype(BF16)


def _proj_gate_kernel(x_ref, sc_ref, sh_ref, w_ref, wg_ref, o_ref, g_ref, u_scr):
    @pl.when(pl.program_id(1) == 0)
    def _():
        u = _modulate(x_ref, sc_ref, sh_ref)
        u_scr[...] = u
        g_ref[...] = jnp.dot(u, wg_ref[...].astype(BF16), preferred_element_type=F32)

    o_ref[...] = lax.dot_general(u_scr[...], w_ref[...].astype(BF16), (((1,), (1,)), ((), ())),
                                 preferred_element_type=F32).astype(o_ref.dtype)


def _sublane_transpose8(arrs):
    sub = lax.broadcasted_iota(jnp.int32, arrs[0].shape, 1)
    for b in range(3):
        dist = 1 << b
        sel = (sub & dist) != 0
        new = list(arrs)
        for j in range(8):
            if not j & dist:
                lo, hi = arrs[j], arrs[j + dist]
                new[j] = jnp.where(sel, pltpu.roll(hi, dist, 1), lo)
                new[j + dist] = jnp.where(sel, hi, pltpu.roll(lo, 8 - dist, 1))
        arrs = new
    return arrs


STEP_VIEW_LANES = 512


def _load_block_residues(x_ref, ls):
    tc = x_ref.shape[0]
    return [jnp.stack([x_ref[8 * cg + j, :, ls] for cg in range(tc // 8)]) for j in range(8)]


def _proj_perm_kernel(xa_ref, xb_ref, sc_ref, sh_ref, w_ref, o_ref, u_scr):
    tc, hs, d = xa_ref.shape
    assert hs == 8 and tc % 8 == 0

    @pl.when(pl.program_id(1) == 0)
    def _():
        for half, x_ref in enumerate((xa_ref, xb_ref)):
            for l0 in range(0, d, STEP_VIEW_LANES):
                ls = slice(l0, l0 + STEP_VIEW_LANES)
                rows = _sublane_transpose8(_load_block_residues(x_ref, ls))
                for sl in range(hs):
                    s = hs * half + sl
                    u = rows[sl].reshape(tc, STEP_VIEW_LANES) * (1.0 + sc_ref[:, ls]) + sh_ref[:, ls]
                    u_scr[s * tc:(s + 1) * tc, ls] = u.astype(BF16)

    o_ref[...] = jnp.dot(u_scr[...], w_ref[...].astype(BF16), preferred_element_type=F32).astype(o_ref.dtype)


def _in_proj_gate(x, scale, shift, wt, layer, wg, tm=1024, tn=1024):
    l, d = x.shape
    n = (wt.shape[1] // tn) * tn
    ng = wg.shape[1]
    tm = min(tm, l)
    vmem = 2 * tm * d * 4 + 2 * d * tn * 4 + 2 * tm * tn * 2 + tm * d * 2 + 2 * d * ng * 4 + 2 * tm * ng * 4
    return pl.pallas_call(
        _proj_gate_kernel,
        grid=(l // tm, n // tn),
        in_specs=[
            pl.BlockSpec((tm, d), lambda i, j: (i, 0)),
            pl.BlockSpec((1, d), lambda i, j: (0, 0)),
            pl.BlockSpec((1, d), lambda i, j: (0, 0)),
            pl.BlockSpec((None, tn, d), lambda i, j: (layer, j, 0)),
            pl.BlockSpec((d, ng), lambda i, j: (0, 0)),
        ],
        out_specs=[pl.BlockSpec((tm, tn), lambda i, j: (i, j)),
                   pl.BlockSpec((tm, ng), lambda i, j: (i, 0))],
        out_shape=[jax.ShapeDtypeStruct((l, n), BF16), jax.ShapeDtypeStruct((l, ng), F32)],
        scratch_shapes=[pltpu.VMEM((tm, d), BF16)],
        compiler_params=_cparams(("arbitrary", "arbitrary"), vmem),
        name="in_proj_gate",
    )(x, scale, shift, wt, wg)


def _step_major_view(x, t, tc):
    l, d = x.shape
    assert l % (t * tc) == 0 and (t // 2) % 8 == 0
    return x.reshape(l // t, 2, t // 2, d)


def _in_proj_perm(x, t, tc, scale, shift, w, layer, tn=1024):
    l, d = x.shape
    x4 = _step_major_view(x, t, tc)
    n = w.shape[2]
    tm = t * tc
    vmem = 2 * tm * d * 4 + 2 * d * tn * 4 + 2 * tm * tn * 2 + tm * d * 2
    return pl.pallas_call(
        _proj_perm_kernel,
        grid=(l // tm, n // tn),
        in_specs=[
            pl.BlockSpec((tc, None, t // 2, d), lambda i, j: (i, 0, 0, 0)),
            pl.BlockSpec((tc, None, t // 2, d), lambda i, j: (i, 1, 0, 0)),
            pl.BlockSpec((1, d), lambda i, j: (0, 0)),
            pl.BlockSpec((1, d), lambda i, j: (0, 0)),
            pl.BlockSpec((None, d, tn), lambda i, j: (layer, 0, j)),
        ],
        out_specs=pl.BlockSpec((tm, tn), lambda i, j: (i, j)),
        out_shape=jax.ShapeDtypeStruct((l, n), BF16),
        scratch_shapes=[pltpu.VMEM((tm, d), BF16)],
        compiler_params=_cparams(("arbitrary", "arbitrary"), vmem),
        name="in_proj_perm",
    )(x4, x4, scale, shift, w)


def _gla_core_kernel(*refs, dk, dv):
    nh = GLA_HEADS
    q_ref, k_ref = refs[0], refs[1]
    v_refs = refs[2:2 + nh]
    z_refs = refs[2 + nh:2 + 2 * nh]
    g_ref, w2_ref, gb_ref, ng_ref, o_ref, st_ref, kd_scr, ko_scr, qd_scr = refs[2 + 2 * nh:]

    @pl.when(pl.program_id(0) == 0)
    def _():
        st_ref[...] = jnp.zeros_like(st_ref)

    c = CHUNK
    r = q_ref.shape[0]
    sb = r // c
    nt = (((1,), (1,)), ((), ()))
    pre = jnp.dot(g_ref[...].astype(BF16), w2_ref[...], preferred_element_type=F32) + gb_ref[...]
    la = (jnp.minimum(pre, 0.0) - jnp.log(1.0 + jnp.exp(-jnp.abs(pre)))) * (1.0 / GLA_TAU)
    row = lax.broadcasted_iota(jnp.int32, (r, r), 0)
    col = lax.broadcasted_iota(jnp.int32, (r, r), 1)
    tri = (col <= row).astype(BF16)
    la_hi = la.astype(BF16)
    la_lo = (la - la_hi.astype(F32)).astype(BF16)
    gc = (jnp.dot(tri, la_hi, preferred_element_type=F32) + jnp.dot(tri, la_lo, preferred_element_type=F32))
    ends = [gc[c * (j + 1) - 1:c * (j + 1), :] for j in range(sb)]
    e_rows = jnp.concatenate([jnp.broadcast_to(ends[j], (c, gc.shape[1])) for j in range(sb)], axis=0)
    kf = k_ref[...].astype(F32) * jnp.exp(e_rows - gc)
    kd_scr[...] = kf.astype(BF16)
    ko_scr[...] = (kf * jnp.exp(ends[-1] - e_rows)).astype(BF16)
    qd_scr[...] = (q_ref[...].astype(F32) * jnp.exp(e_rows)).astype(BF16)
    dec = jnp.exp(ends[-1])
    for h in range(nh):
        ks = slice(h * dk, (h + 1) * dk)
        vs = slice(h * dv, (h + 1) * dv)
        st_in = st_ref[h]
        base = lax.dot_general(qd_scr[:, ks], st_in.astype(BF16), nt, preferred_element_type=F32)
        ams = []
        for m in range(sb):
            qs = jnp.concatenate(
                [(q_ref[c * j:c * (j + 1), ks].astype(F32) * jnp.exp(ends[j][:, ks] - ends[m][:, ks])).astype(BF16)
                 if j > m else q_ref[c * j:c * (j + 1), ks] for j in range(m, sb)], axis=0)
            kpad = jnp.concatenate(
                ([jnp.zeros((c * m, dk), BF16)] if m else []) + [kd_scr[c * m:c * (m + 1), ks]]
                + ([jnp.zeros((c * (sb - 1 - m), dk), BF16)] if m < sb - 1 else []), axis=0)
            ams.append(lax.dot_general(qs, kpad, nt, preferred_element_type=F32))
        a_rows = []
        for j in range(sb):
            aj = ams[0][c * j:c * (j + 1)]
            for m in range(1, j + 1):
                aj = aj + ams[m][c * (j - m):c * (j - m + 1)]
            a_rows.append(aj)
        a_full = jnp.concatenate(a_rows, axis=0).astype(BF16)
        o = base + jnp.dot(a_full, v_refs[h][...], preferred_element_type=F32)
        upd = lax.dot_general(v_refs[h][...], ko_scr[:, ks], (((0,), (0,)), ((), ())),
                              preferred_element_type=F32)
        st_ref[h] = dec[:, ks] * st_in + upd
        o = o * (dk ** -0.5)
        o = o * lax.rsqrt(jnp.mean(o * o, axis=-1, keepdims=True) + RMS_EPS)
        y = o * ng_ref[:, vs] * _silu(z_refs[h][...].astype(F32))
        o_ref[:, vs] = y.astype(o_ref.dtype)


def _gla_core(proj, glr, w2p, gate_b, norm_g, e, qk):
    l = proj.shape[0]
    nh = GLA_HEADS
    dk, dv = qk // nh, e // nh
    assert (2 * qk) % dv == 0
    v0 = 2 * qk // dv
    ng = glr.shape[1]
    c = min(GLA_STEP_CHUNKS * CHUNK, l)
    assert c % CHUNK == 0 and l % c == 0
    kern = functools.partial(_gla_core_kernel, dk=dk, dv=dv)
    vmem = (2 * (2 * c * e * 2 + 2 * c * qk * 2 + c * ng * 4 + ng * qk * 2 + c * e * 2) + nh * dv * dk * 4
            + 12 * c * qk * 4 + 4 * c * dv * 4)
    head_specs = [pl.BlockSpec((c, dv), functools.partial(lambda n, b: (n, b), b=v0 + h)) for h in range(2 * nh)]
    return pl.pallas_call(
        kern,
        grid=(l // c,),
        in_specs=[
            pl.BlockSpec((c, qk), lambda n: (n, 0)),
            pl.BlockSpec((c, qk), lambda n: (n, 1)),
            *head_specs,
            pl.BlockSpec((c, ng), lambda n: (n, 0)),
            pl.BlockSpec((ng, qk), lambda n: (0, 0)),
            pl.BlockSpec((1, qk), lambda n: (0, 0)),
            pl.BlockSpec((1, e), lambda n: (0, 0)),
        ],
        out_specs=pl.BlockSpec((c, e), lambda n: (n, 0)),
        out_shape=jax.ShapeDtypeStruct((l, e), BF16),
        scratch_shapes=[pltpu.VMEM((nh, dv, dk), F32)] + [pltpu.VMEM((c, qk), BF16) for _ in range(3)],
        compiler_params=_cparams(("arbitrary",), vmem),
        name="gla_core",
    )(*([proj] * (2 + 2 * nh)), glr, w2p, gate_b, norm_g)


def _out_ln_kernel(y_ref, w_ref, x_ref, gate_ref, g_ref, b_ref, o_ref, acc_ref, *, alpha):
    k = pl.program_id(1)

    @pl.when(k == 0)
    def _():
        acc_ref[...] = jnp.zeros_like(acc_ref)

    acc_ref[...] += jnp.dot(y_ref[...], w_ref[...], preferred_element_type=F32)

    def layer_norm(r):
        mu = jnp.mean(r, axis=-1, keepdims=True)
        rc = r - mu
        var = jnp.mean(rc * rc, axis=-1, keepdims=True)
        return rc * lax.rsqrt(var + LN_EPS) * g_ref[...] + b_ref[...]

    @pl.when(k == pl.num_programs(1) - 1)
    def _():
        if len(x_ref.shape) == 2:
            o_ref[...] = layer_norm(alpha * x_ref[...] + (1.0 + gate_ref[...]) * acc_ref[...])
        else:
            tc, hs, d = x_ref.shape
            assert hs == 8 and tc % 8 == 0
            chunks = [slice(l0, l0 + STEP_VIEW_LANES) for l0 in range(0, d, STEP_VIEW_LANES)]
            for ls in chunks:
                xs = _sublane_transpose8(_load_block_residues(x_ref, ls))
                for sl in range(hs):
                    rows = slice(sl * tc, (sl + 1) * tc)
                    acc_ref[rows, ls] = (alpha * xs[sl].reshape(tc, STEP_VIEW_LANES)
                                         + (1.0 + gate_ref[:, ls]) * acc_ref[rows, ls])
            acc_ref[...] = layer_norm(acc_ref[...])
            for ls in chunks:
                outs = _sublane_transpose8([acc_ref[sl * tc:(sl + 1) * tc, ls].reshape(tc // 8, 8, STEP_VIEW_LANES)
                                            for sl in range(hs)])
                for j in range(8):
                    for cg in range(tc // 8):
                        o_ref[8 * cg + j, :, ls] = outs[j][cg]


def _out_ln(y, w, layer, x, gate, ln_g, ln_b, alpha, step_major=None, tm=512, tk=1024):
    l, e = y.shape
    d = w.shape[2]
    kern = functools.partial(_out_ln_kernel, alpha=alpha)
    if step_major is None:
        tm = min(tm, l)
        x_in, out_shape = x, (l, d)
        x_spec = pl.BlockSpec((tm, d), lambda i, k: (i, 0))
    else:
        t, tc = step_major
        tm = tc * t // 2
        x_in = _step_major_view(x, t, tc)
        out_shape = x_in.shape
        x_spec = pl.BlockSpec((tc, None, t // 2, d), lambda i, k: (i // 2, i % 2, 0, 0))
    vmem = 2 * tm * tk * 2 + 2 * tk * d * 2 + 4 * tm * d * 4 + tm * d * 4
    if step_major is not None:
        vmem += 2 * tm * d * 4
    out = pl.pallas_call(
        kern,
        grid=(l // tm, e // tk),
        in_specs=[
            pl.BlockSpec((tm, tk), lambda i, k: (i, k)),
            pl.BlockSpec((None, tk, d), lambda i, k: (layer, k, 0)),
            x_spec,
            pl.BlockSpec((1, d), lambda i, k: (0, 0)),
            pl.BlockSpec((1, d), lambda i, k: (0, 0)),
            pl.BlockSpec((1, d), lambda i, k: (0, 0)),
        ],
        out_specs=x_spec,
        out_shape=jax.ShapeDtypeStruct(out_shape, F32),
        scratch_shapes=[pltpu.VMEM((tm, d), F32)],
        compiler_params=_cparams(("arbitrary", "arbitrary"), vmem),
        name="out_proj_ln",
    )(y, w, x_in, gate, ln_g, ln_b)
    return out.reshape(l, d)


def _rot_blocks(v, nblk):
    nblk %= 8
    return pltpu.roll(v, S5_GROUP * nblk, 1) if nblk else v


def _skew_select(cols):
    blk = lax.shift_right_logical(lax.broadcasted_iota(jnp.int32, cols[0].shape, 1), 4)
    masks = [blk == j for j in range(8)]
    out = []
    for a in range(8):
        r = cols[(-a) % 8]
        for j in range(1, 8):
            r = jnp.where(masks[j], cols[(j - a) % 8], r)
        out.append(r)
    return out


def _ssm_kernel(x_ref, w_ref, p_ref, q_ref, lr_ref, li_ref, d_ref, o_ref, a_scr, y_scr, vre, vim, sre, sim):
    ntile, _, tc, _ = x_ref.shape
    nc = ntile * tc
    npair = S5_OCT // 2

    def gather_steps(tile, carry):
        r = pl.ds(pl.multiple_of(tile * tc, tc), tc)
        for s_hi in range(2):
            for g_hi in range(2):
                cols = [_rot_blocks(pltpu.bitcast(x_ref[tile, 8 * s_hi + m, :, g_hi * LANES:(g_hi + 1) * LANES], U32),
                                    m) for m in range(8)]
                res = _skew_select(cols)
                for m in range(8):
                    a_scr[8 * g_hi + m, r, s_hi * LANES:(s_hi + 1) * LANES] = pltpu.bitcast(res[m], BF16)
        return carry

    lax.fori_loop(0, ntile, gather_steps, 0)
    a_bf = a_scr

    nt = (((1,), (1,)), ((), ()))
    for p in range(npair):
        v = (lax.dot_general(a_bf[2 * p], p_ref[2 * p], nt, preferred_element_type=F32)
             + lax.dot_general(a_bf[2 * p + 1], p_ref[2 * p + 1], nt, preferred_element_type=F32))
        vre[pl.ds(p, nc, stride=npair), :] = v[:, :LANES]
        vim[pl.ds(p, nc, stride=npair), :] = v[:, LANES:]

    ar = lr_ref[...]
    ai = li_ref[...]

    def step(c, carry):
        xr, xi = carry
        rows = pl.ds(pl.multiple_of(c * npair, npair), npair)
        sre[rows, :] = xr
        sim[rows, :] = xi
        nxr = ar * xr - ai * xi + vre[rows, :]
        nxi = ar * xi + ai * xr + vim[rows, :]
        return nxr, nxi

    zero = jnp.zeros((npair, LANES), F32)
    lax.fori_loop(0, nc, step, (zero, zero), unroll=8)

    for p in range(npair):
        s = jnp.concatenate([sre[pl.ds(p, nc, stride=npair), :], sim[pl.ds(p, nc, stride=npair), :]],
                            axis=1).astype(BF16)
        for g in (2 * p, 2 * p + 1):
            y_scr[g] = (jnp.dot(a_bf[g], w_ref[g], preferred_element_type=F32)
                        + jnp.dot(s, q_ref[g], preferred_element_type=F32))

    rbo = min(32, tc)
    per_tile = tc // rbo

    def scatter_steps(it, carry):
        r = pl.ds(pl.multiple_of(it * rbo, rbo), rbo)
        tile = it // per_tile
        rt = pl.ds(pl.multiple_of((it % per_tile) * rbo, rbo), rbo)
        for s_hi in range(2):
            for g_hi in range(2):
                cs = slice(g_hi * LANES, (g_hi + 1) * LANES)
                cols = [y_scr[8 * g_hi + m, r, s_hi * LANES:(s_hi + 1) * LANES] for m in range(8)]
                res = _skew_select(cols)
                dsk = d_ref[:, cs]
                for m in range(8):
                    s = 8 * s_hi + m
                    y = _rot_blocks(res[m], -m) + dsk * x_ref[tile, s, rt, cs].astype(F32)
                    o_ref[tile, s, rt, cs] = _gelu_tanh(y).astype(o_ref.dtype)
        return carry

    lax.fori_loop(0, nc // rbo, scatter_steps, 0)


def _ssm_core(uz, tc, wt, pm, qm, lam_r, lam_i, d_skip, e):
    t = S5_T
    nc = uz.shape[0] // t
    kk = t * S5_GROUP
    assert kk == 2 * LANES and S5_OCT * S5_GROUP == kk
    noct = e // kk
    npair = S5_OCT // 2
    ntile = nc // tc
    x4 = uz.reshape(ntile, t, tc, uz.shape[1])
    blk = (ntile, t, tc, kk)
    vmem = (2 * (2 * t * nc * kk * 2 + 3 * S5_OCT * kk * kk * 2) + S5_OCT * nc * kk * (2 + 4)
            + 4 * nc * npair * LANES * 4)
    out = pl.pallas_call(
        _ssm_kernel,
        grid=(noct,),
        in_specs=[
            pl.BlockSpec(blk, lambda i: (0, 0, 0, i)),
            pl.BlockSpec((S5_OCT, kk, kk), lambda i: (i, 0, 0)),
            pl.BlockSpec((S5_OCT, kk, kk), lambda i: (i, 0, 0)),
            pl.BlockSpec((S5_OCT, kk, kk), lambda i: (i, 0, 0)),
            pl.BlockSpec((None, npair, LANES), lambda i: (i, 0, 0)),
            pl.BlockSpec((None, npair, LANES), lambda i: (i, 0, 0)),
            pl.BlockSpec((1, kk), lambda i: (0, i)),
        ],
        out_specs=pl.BlockSpec(blk, lambda i: (0, 0, 0, i)),
        out_shape=jax.ShapeDtypeStruct((ntile, t, tc, e), BF16),
        scratch_shapes=[pltpu.VMEM((S5_OCT, nc, kk), BF16), pltpu.VMEM((S5_OCT, nc, kk), F32)]
        + [pltpu.VMEM((nc * npair, LANES), F32) for _ in range(4)],
        compiler_params=_cparams(("arbitrary",), vmem),
        name="s5_ssm",
    )(x4, wt, pm, qm, lam_r, lam_i, d_skip)
    return out.reshape(t * nc, e)


def _cmul(ar, ai, br, bi):
    return ar * br - ai * bi, ar * bi + ai * br


def _zoh(a_re, a_im, dt):
    mag = jnp.exp(a_re * dt)
    return mag * jnp.cos(a_im * dt), mag * jnp.sin(a_im * dt)


def _s5_ops_kernel(are_ref, aim_ref, ldt_ref, bre_ref, bim_ref, cre_ref, cim_ref, arp_ref, aip_ref, ldtp_ref,
                   w_ref, pt_ref, q_ref, lamr_ref, lami_ref, rows_scr):
    t, gi = S5_T, S5_GROUP
    ns = are_ref.shape[1]
    kk = t * gi

    pr, pi = _zoh(arp_ref[...], aip_ref[...], jnp.exp(ldtp_ref[...]))
    for _ in range(t.bit_length() - 1):
        pr, pi = _cmul(pr, pi, pr, pi)
    lamr_ref[...] = pr
    lami_ref[...] = pi

    ar = are_ref[...]
    ai = aim_ref[...]
    l1r, l1i = _zoh(ar, ai, jnp.exp(ldt_ref[...]))
    nr = l1r - 1.0
    den = ar * ar + ai * ai
    quantities = [l1r, l1i]
    for _ in range(3):
        quantities += list(_cmul(quantities[-2], quantities[-1], quantities[-2], quantities[-1]))
    quantities += [(nr * ar + l1i * ai) / den, (l1i * ar - nr * ai) / den]
    for k, val in enumerate(quantities):
        rows_scr[k] = val

    eye = lax.broadcasted_iota(jnp.int32, (ns, ns), 0) == lax.broadcasted_iota(jnp.int32, (ns, ns), 1)
    lane = lax.broadcasted_iota(jnp.int32, (ns, kk), 1)
    tau = lax.shift_right_logical(lane, 4)
    bits = [(lax.shift_right_logical(tau, b) & 1) == 1 for b in range(4)]
    expand = (lax.broadcasted_iota(jnp.int32, (gi, kk), 0)
              == (lax.broadcasted_iota(jnp.int32, (gi, kk), 1) & (gi - 1))).astype(F32)
    lane_w = lax.broadcasted_iota(jnp.int32, (gi, LANES), 1)
    tn_dims = (((0,), (0,)), ((), ()))

    def column(k, g):
        row = rows_scr[k, pl.ds(g, 1), :]
        return jnp.sum(jnp.where(eye, row, 0.0), axis=1, keepdims=True)

    def lane_powers(cols, flip):
        pr = pi = None
        for b in range(4):
            on = jnp.logical_not(bits[b]) if flip else bits[b]
            fr = jnp.where(on, cols[2 * b], 1.0)
            fi = jnp.where(on, cols[2 * b + 1], 0.0)
            pr, pi = (fr, fi) if pr is None else _cmul(pr, pi, fr, fi)
        return pr, pi

    def build(g, odd):
        cols = [column(k, g) for k in range(10)]
        cr, ci = cols[8], cols[9]
        bbr = cr * bre_ref[g] - ci * bim_ref[g]
        bbi = cr * bim_ref[g] + ci * bre_ref[g]
        pwr, pwi = lane_powers(cols, False)
        rvr, rvi = lane_powers(cols, True)
        p1r, p1i = _cmul(pwr, pwi, cols[0], cols[1])
        ctr = lax.dot_general(cre_ref[g], expand, tn_dims, precision=HIGHEST, preferred_element_type=F32)
        cti = lax.dot_general(cim_ref[g], expand, tn_dims, precision=HIGHEST, preferred_element_type=F32)
        bbtr = jnp.dot(bbr, expand, precision=HIGHEST, preferred_element_type=F32)
        bbti = jnp.dot(bbi, expand, precision=HIGHEST, preferred_element_type=F32)
        clr, cli = _cmul(ctr, cti, pwr, pwi)
        kt = (lax.dot_general(bbr, clr, tn_dims, precision=HIGHEST, preferred_element_type=F32)
              - lax.dot_general(bbi, cli, tn_dims, precision=HIGHEST, preferred_element_type=F32))
        gm = g % 8

        def rot_halves(v):
            return jnp.concatenate([_rot_blocks(v[:, :LANES], gm), _rot_blocks(v[:, LANES:], gm)], axis=1)

        k0, k1 = kt[:, :LANES], kt[:, LANES:]
        for s in range(t):
            sh = (gi * s) % LANES
            r0 = pltpu.roll(k0, sh, 1) if sh else k0
            if gi * s < LANES:
                r1 = pltpu.roll(k1, sh, 1) if sh else k1
                lo = jnp.where(lane_w >= sh, r0, 0.0)
                hi = jnp.where(lane_w >= sh, r1, r0)
            else:
                lo = jnp.zeros_like(k0)
                hi = jnp.where(lane_w >= sh, r0, 0.0)
            row0 = gi * (8 * (s // 8) + (s + gm) % 8)
            w_ref[g, row0:row0 + gi, :] = rot_halves(jnp.concatenate([lo, hi], axis=1)).astype(w_ref.dtype)
        ptr, pti = _cmul(rvr, rvi, bbtr, bbti)
        qr, qi = _cmul(ctr, cti, p1r, p1i)
        zero = jnp.zeros((ns, kk), pt_ref.dtype)
        for ref, re, im in ((pt_ref, ptr, pti), (q_ref, qr, -qi)):
            ref[g, odd * ns:(odd + 1) * ns, :] = rot_halves(re).astype(ref.dtype)
            ref[g, (1 - odd) * ns:(2 - odd) * ns, :] = zero
            ref[g, (2 + odd) * ns:(3 + odd) * ns, :] = rot_halves(im).astype(ref.dtype)
            ref[g, (3 - odd) * ns:(4 - odd) * ns, :] = zero

    for g in range(S5_OCT):
        build(g, g % 2)


def _s5_operators(a_re, a_im, log_dt, b_re, b_im, c_re, c_im):
    g, ns = a_re.shape
    t, gi = S5_T, S5_GROUP
    kk = t * gi
    assert t == 16 and 4 * ns == kk and 2 * ns == LANES
    noct = g // S5_OCT
    npair = S5_OCT // 2
    a_re_p = a_re.reshape(g // 2, 2 * ns)
    a_im_p = a_im.reshape(g // 2, 2 * ns)
    ldt_p = jnp.broadcast_to(log_dt[:, None], (g, ns)).reshape(g // 2, 2 * ns)
    oct3 = lambda i: (i, 0, 0)
    ops_shape = jax.ShapeDtypeStruct((g, kk, kk), BF16)
    lam_shape = jax.ShapeDtypeStruct((noct, npair, LANES), F32)
    return pl.pallas_call(
        _s5_ops_kernel,
        grid=(noct,),
        in_specs=[
            pl.BlockSpec((S5_OCT, ns), lambda i: (i, 0)),
            pl.BlockSpec((S5_OCT, ns), lambda i: (i, 0)),
            pl.BlockSpec((S5_OCT, 1), lambda i: (i, 0)),
            pl.BlockSpec((S5_OCT, ns, gi), oct3),
            pl.BlockSpec((S5_OCT, ns, gi), oct3),
            pl.BlockSpec((S5_OCT, gi, ns), oct3),
            pl.BlockSpec((S5_OCT, gi, ns), oct3),
            pl.BlockSpec((npair, LANES), lambda i: (i, 0)),
            pl.BlockSpec((npair, LANES), lambda i: (i, 0)),
            pl.BlockSpec((npair, LANES), lambda i: (i, 0)),
        ],
        out_specs=[pl.BlockSpec((S5_OCT, kk, kk), oct3)] * 3 + [pl.BlockSpec((None, npair, LANES), oct3)] * 2,
        out_shape=[ops_shape] * 3 + [lam_shape] * 2,
        scratch_shapes=[pltpu.VMEM((10, S5_OCT, ns), F32)],
        compiler_params=_cparams(("arbitrary",), 2 * 3 * S5_OCT * kk * kk * 2 + 4 * S5_OCT * ns * LANES * 4 * 2),
        name="s5_ops",
    )(a_re, a_im, log_dt.reshape(g, 1), b_re, b_im, c_re, c_im, a_re_p, a_im_p, ldt_p)


def _glu_kernel(ya_ref, w_ref, b_ref, yc_ref, z_ref, o_ref):
    acc = jnp.dot(ya_ref[...], w_ref[...].astype(BF16), preferred_element_type=F32) + b_ref[...]
    o_ref[...] = (yc_ref[...].astype(F32) * _sigmoid(acc) * _silu(z_ref[...].astype(F32))).astype(o_ref.dtype)


def _glu(yact, w_glu, layer, b_glu, uz, tm=1024, tn=512):
    l, e = yact.shape
    tm = min(tm, l)
    zoff = e // tn
    vmem = 2 * (tm * e * 2 + e * tn * 4 + 3 * tm * tn * 2)
    return pl.pallas_call(
        _glu_kernel,
        grid=(l // tm, e // tn),
        in_specs=[pl.BlockSpec((tm, e), lambda i, j: (i, 0)),
                  pl.BlockSpec((None, e, tn), lambda i, j: (layer, 0, j)),
                  pl.BlockSpec((1, tn), lambda i, j: (0, j)),
                  pl.BlockSpec((tm, tn), lambda i, j: (i, j)),
                  pl.BlockSpec((tm, tn), lambda i, j: (i, zoff + j))],
        out_specs=pl.BlockSpec((tm, tn), lambda i, j: (i, j)),
        out_shape=jax.ShapeDtypeStruct((l, e), BF16),
        compiler_params=_cparams(("arbitrary", "arbitrary"), vmem),
        name="s5_glu",
    )(yact, w_glu, b_glu, yact, uz)


def _gla_layer(x, scale, shift, gate, ln_g, ln_b, alpha, w_in, layer, gate_w2, gate_b, norm_g, w_out):
    d = x.shape[1]
    e = w_out.shape[1]
    qk = gate_w2.shape[1]
    wg = jnp.pad(w_in[layer, :, 2 * qk + 2 * e:], ((0, 0), (0, LANES - GLA_GATE_RANK)))
    w2p = jnp.pad(gate_w2, ((0, LANES - GLA_GATE_RANK), (0, 0))).astype(BF16)
    proj, glr = _in_proj_gate(x, scale, shift, jnp.swapaxes(w_in, 1, 2), layer, wg)
    y = _gla_core(proj, glr, w2p, gate_b.reshape(1, qk), norm_g.reshape(1, e), e, qk)
    return _out_ln(y, w_out, layer, x, gate, ln_g.reshape(1, d), ln_b.reshape(1, d), alpha)


def _s5_layer(x, scale, shift, gate, ln_g, ln_b, alpha, w_in, layer, a_re, a_im, log_dt, b_re, b_im, c_re, c_im,
              d_skip, w_glu, b_glu, w_out):
    l, d = x.shape
    e = w_out.shape[1]
    t = S5_T
    tc = min(S5_TILE_BLOCKS, l // t)
    uz = _in_proj_perm(x, t, tc, scale, shift, w_in, layer)
    wt, pm, qm, lam_r, lam_i = _s5_operators(a_re, a_im, log_dt, b_re, b_im, c_re, c_im)
    yact = _ssm_core(uz, tc, wt, pm, qm, lam_r, lam_i, d_skip.reshape(1, e), e)
    yglu = _glu(yact, w_glu, layer, b_glu.reshape(1, e), uz)
    return _out_ln(yglu, w_out, layer, x, gate, ln_g.reshape(1, d), ln_b.reshape(1, d), alpha,
                   step_major=(t, tc))


def kernel(x, c, ln_g, ln_b, ada_w, ada_b, gla_w_in, gla_gate_w2, gla_gate_b, gla_norm_g, gla_w_out,
           s5_w_in, s5_a_re, s5_a_im, s5_log_dt, s5_b_re, s5_b_im, s5_c_re, s5_c_im, s5_d,
           s5_w_glu, s5_b_glu, s5_w_out):
    bsz, l, d = x.shape
    assert bsz == 1, "batch 1 only"
    depth = ln_g.shape[0]
    alpha = (2 * depth) ** 0.25
    mod = _adaln(c, ada_w, ada_b)
    h = x.reshape(l, d)
    gla_w_out_b = gla_w_out.astype(BF16)
    s5_w_out_b = s5_w_out.astype(BF16)
    for i in range(depth):
        shift, scale, gate = mod[i, :, :d], mod[i, :, d:2 * d], mod[i, :, 2 * d:]
        j = i // 2
        if i % 2 == 0:
            h = _gla_layer(h, scale, shift, gate, ln_g[i], ln_b[i], alpha, gla_w_in, j, gla_gate_w2[j],
                           gla_gate_b[j], gla_norm_g[j], gla_w_out_b)
        else:
            h = _s5_layer(h, scale, shift, gate, ln_g[i], ln_b[i], alpha, s5_w_in, j, s5_a_re[j], s5_a_im[j],
                          s5_log_dt[j], s5_b_re[j], s5_b_im[j], s5_c_re[j], s5_c_im[j], s5_d[j],
                          s5_w_glu, s5_b_glu[j], s5_w_out_b)
    return h.reshape(bsz, l, d)
```

```python
import functools
import math

import jax
import jax.numpy as jnp
from jax import lax
from jax.experimental import pallas as pl
from jax.experimental.pallas import tpu as pltpu

F32 = jnp.float32
BF16 = jnp.bfloat16
U32 = jnp.uint32
HIGHEST = lax.Precision.HIGHEST

CHUNK = 64
GLA_HEADS = 4
GLA_GATE_RANK = 16
GLA_TAU = 16.0
S5_GROUP = 16
S5_STATE = 64
LN_EPS = 1e-5
RMS_EPS = 1e-6

S5_T = 16
S5_OCT = 16
GLA_STEP_CHUNKS = 4
S5_TILE_BLOCKS = 64
GLU_ROW_CHUNK = 128
OUT_ROW_CHUNK = 128
LANES = 128
V7X_SCOPED_VMEM_CAP = 60000 * 1024


def _cparams(semantics, vmem_bytes):
    limit = min(int(vmem_bytes) + (6 << 20), V7X_SCOPED_VMEM_CAP)
    return pltpu.CompilerParams(dimension_semantics=semantics, vmem_limit_bytes=limit)


def _sigmoid(x):
    return 1.0 / (1.0 + jnp.exp(-x))


def _silu(x):
    return x * _sigmoid(x)


def _gelu_tanh(y):
    cdf = 0.5 * (1.0 + jnp.tanh(math.sqrt(2.0 / math.pi) * (y + 0.044715 * (y * y * y))))
    return y * cdf


def _adaln_kernel(c_ref, w_ref, b_ref, o_ref):
    c = c_ref[...]
    o_ref[...] = jnp.sum(_silu(c) * w_ref[...], axis=0, keepdims=True) + b_ref[...]


def _adaln(c, ada_w, ada_b, tn=1024):
    depth, d, n3 = ada_w.shape
    assert c.shape == (1, d), "batch 1 only"
    c_col = c.reshape(d, 1)
    out = pl.pallas_call(
        _adaln_kernel,
        grid=(depth, n3 // tn),
        in_specs=[
            pl.BlockSpec((d, 1), lambda l, j: (0, 0)),
            pl.BlockSpec((None, d, tn), lambda l, j: (l, 0, j)),
            pl.BlockSpec((None, 1, tn), lambda l, j: (l, 0, j)),
        ],
        out_specs=pl.BlockSpec((None, 1, tn), lambda l, j: (l, 0, j)),
        out_shape=jax.ShapeDtypeStruct((depth, 1, n3), F32),
        compiler_params=_cparams(("arbitrary", "arbitrary"), 2 * d * tn * 4 + d * LANES * 4),
        name="adaln_mod",
    )(c_col, ada_w, ada_b.reshape(depth, 1, n3))
    return out


def _modulate(x_ref, sc_ref, sh_ref):
    return (x_ref[...] * (1.0 + sc_ref[...]) + sh_ref[...]).astype(BF16)


def _proj_gate_kernel(x_ref, sc_ref, sh_ref, w_ref, wg_ref, o_ref, g_ref, u_scr):
    @pl.when(pl.program_id(1) == 0)
    def _():
        u = _modulate(x_ref, sc_ref, sh_ref)
        u_scr[...] = u
        g_ref[...] = jnp.dot(u, wg_ref[...].astype(BF16), preferred_element_type=F32)

    o_ref[...] = lax.dot_general(u_scr[...], w_ref[...].astype(BF16), (((1,), (1,)), ((), ())),
                                 preferred_element_type=F32).astype(o_ref.dtype)


def _sublane_transpose8(arrs):
    sub = lax.broadcasted_iota(jnp.int32, arrs[0].shape, 1)
    for b in range(3):
        dist = 1 << b
        sel = (sub & dist) != 0
        new = list(arrs)
        for j in range(8):
            if not j & dist:
                lo, hi = arrs[j], arrs[j + dist]
                new[j] = jnp.where(sel, pltpu.roll(hi, dist, 1), lo)
                new[j + dist] = jnp.where(sel, hi, pltpu.roll(lo, 8 - dist, 1))
        arrs = new
    return arrs


STEP_VIEW_LANES = 512


def _load_block_residues(x_ref, ls):
    tc = x_ref.shape[0]
    return [jnp.stack([x_ref[8 * cg + j, :, ls] for cg in range(tc // 8)]) for j in range(8)]


def _proj_perm_kernel(xa_ref, xb_ref, sc_ref, sh_ref, w_ref, o_ref, u_scr):
    tc, hs, d = xa_ref.shape
    assert hs == 8 and tc % 8 == 0

    @pl.when(pl.program_id(1) == 0)
    def _():
        for half, x_ref in enumerate((xa_ref, xb_ref)):
            for l0 in range(0, d, STEP_VIEW_LANES):
                ls = slice(l0, l0 + STEP_VIEW_LANES)
                rows = _sublane_transpose8(_load_block_residues(x_ref, ls))
                for sl in range(hs):
                    s = hs * half + sl
                    u = rows[sl].reshape(tc, STEP_VIEW_LANES) * (1.0 + sc_ref[:, ls]) + sh_ref[:, ls]
                    u_scr[s * tc:(s + 1) * tc, ls] = u.astype(BF16)

    o_ref[...] = jnp.dot(u_scr[...], w_ref[...].astype(BF16), preferred_element_type=F32).astype(o_ref.dtype)


def _in_proj_gate(x, scale, shift, wt, layer, wg, tm=1024, tn=1024):
    l, d = x.shape
    n = (wt.shape[1] // tn) * tn
    ng = wg.shape[1]
    tm = min(tm, l)
    vmem = 2 * tm * d * 4 + 2 * d * tn * 4 + 2 * tm * tn * 2 + tm * d * 2 + 2 * d * ng * 4 + 2 * tm * ng * 4
    return pl.pallas_call(
        _proj_gate_kernel,
        grid=(l // tm, n // tn),
        in_specs=[
            pl.BlockSpec((tm, d), lambda i, j: (i, 0)),
            pl.BlockSpec((1, d), lambda i, j: (0, 0)),
            pl.BlockSpec((1, d), lambda i, j: (0, 0)),
            pl.BlockSpec((None, tn, d), lambda i, j: (layer, j, 0)),
            pl.BlockSpec((d, ng), lambda i, j: (0, 0)),
        ],
        out_specs=[pl.BlockSpec((tm, tn), lambda i, j: (i, j)),
                   pl.BlockSpec((tm, ng), lambda i, j: (i, 0))],
        out_shape=[jax.ShapeDtypeStruct((l, n), BF16), jax.ShapeDtypeStruct((l, ng), F32)],
        scratch_shapes=[pltpu.VMEM((tm, d), BF16)],
        compiler_params=_cparams(("arbitrary", "arbitrary"), vmem),
        name="in_proj_gate",
    )(x, scale, shift, wt, wg)


def _step_major_view(x, t, tc):
    l, d = x.shape
    assert l % (t * tc) == 0 and (t // 2) % 8 == 0
    return x.reshape(l // t, 2, t // 2, d)


def _in_proj_perm(x, t, tc, scale, shift, w, layer, tn=1024):
    l, d = x.shape
    x4 = _step_major_view(x, t, tc)
    n = w.shape[2]
    tm = t * tc
    vmem = 2 * tm * d * 4 + 2 * d * tn * 4 + 2 * tm * tn * 2 + tm * d * 2
    return pl.pallas_call(
        _proj_perm_kernel,
        grid=(l // tm, n // tn),
        in_specs=[
            pl.BlockSpec((tc, None, t // 2, d), lambda i, j: (i, 0, 0, 0)),
            pl.BlockSpec((tc, None, t // 2, d), lambda i, j: (i, 1, 0, 0)),
            pl.BlockSpec((1, d), lambda i, j: (0, 0)),
            pl.BlockSpec((1, d), lambda i, j: (0, 0)),
            pl.BlockSpec((None, d, tn), lambda i, j: (layer, 0, j)),
        ],
        out_specs=pl.BlockSpec((tm, tn), lambda i, j: (i, j)),
        out_shape=jax.ShapeDtypeStruct((l, n), BF16),
        scratch_shapes=[pltpu.VMEM((tm, d), BF16)],
        compiler_params=_cparams(("arbitrary", "arbitrary"), vmem),
        name="in_proj_perm",
    )(x4, x4, scale, shift, w)


def _gla_core_kernel(*refs, dk, dv):
    nh = GLA_HEADS
    q_ref, k_ref = refs[0], refs[1]
    v_refs = refs[2:2 + nh]
    z_refs = refs[2 + nh:2 + 2 * nh]
    g_ref, w2_ref, gb_ref, ng_ref, o_ref, st_ref, kd_scr, ko_scr, qd_scr = refs[2 + 2 * nh:]

    @pl.when(pl.program_id(0) == 0)
    def _():
        st_ref[...] = jnp.zeros_like(st_ref)

    c = CHUNK
    r = q_ref.shape[0]
    sb = r // c
    nt = (((1,), (1,)), ((), ()))
    pre = jnp.dot(g_ref[...].astype(BF16), w2_ref[...], preferred_element_type=F32) + gb_ref[...]
    la = (jnp.minimum(pre, 0.0) - jnp.log(1.0 + jnp.exp(-jnp.abs(pre)))) * (1.0 / GLA_TAU)
    row = lax.broadcasted_iota(jnp.int32, (r, r), 0)
    col = lax.broadcasted_iota(jnp.int32, (r, r), 1)
    tri = (col <= row).astype(BF16)
    la_hi = la.astype(BF16)
    la_lo = (la - la_hi.astype(F32)).astype(BF16)
    gc = (jnp.dot(tri, la_hi, preferred_element_type=F32) + jnp.dot(tri, la_lo, preferred_element_type=F32))
    ends = [gc[c * (j + 1) - 1:c * (j + 1), :] for j in range(sb)]
    e_rows = jnp.concatenate([jnp.broadcast_to(ends[j], (c, gc.shape[1])) for j in range(sb)], axis=0)
    kf = k_ref[...].astype(F32) * jnp.exp(e_rows - gc)
    kd_scr[...] = kf.astype(BF16)
    ko_scr[...] = (kf * jnp.exp(ends[-1] - e_rows)).astype(BF16)
    qd_scr[...] = (q_ref[...].astype(F32) * jnp.exp(e_rows)).astype(BF16)
    dec = jnp.exp(ends[-1])
    for h in range(nh):
        ks = slice(h * dk, (h + 1) * dk)
        vs = slice(h * dv, (h + 1) * dv)
        st_in = st_ref[h]
        base = lax.dot_general(qd_scr[:, ks], st_in.astype(BF16), nt, preferred_element_type=F32)
        ams = []
        for m in range(sb):
            qs = jnp.concatenate(
                [(q_ref[c * j:c * (j + 1), ks].astype(F32) * jnp.exp(ends[j][:, ks] - ends[m][:, ks])).astype(BF16)
                 if j > m else q_ref[c * j:c * (j + 1), ks] for j in range(m, sb)], axis=0)
            kpad = jnp.concatenate(
                ([jnp.zeros((c * m, dk), BF16)] if m else []) + [kd_scr[c * m:c * (m + 1), ks]]
                + ([jnp.zeros((c * (sb - 1 - m), dk), BF16)] if m < sb - 1 else []), axis=0)
            ams.append(lax.dot_general(qs, kpad, nt, preferred_element_type=F32))
        a_rows = []
        for j in range(sb):
            aj = ams[0][c * j:c * (j + 1)]
            for m in range(1, j + 1):
                aj = aj + ams[m][c * (j - m):c * (j - m + 1)]
            a_rows.append(aj)
        a_full = jnp.concatenate(a_rows, axis=0).astype(BF16)
        o = base + jnp.dot(a_full, v_refs[h][...], preferred_element_type=F32)
        upd = lax.dot_general(v_refs[h][...], ko_scr[:, ks], (((0,), (0,)), ((), ())),
                              preferred_element_type=F32)
        st_ref[h] = dec[:, ks] * st_in + upd
        o = o * (dk ** -0.5)
        o = o * lax.rsqrt(jnp.mean(o * o, axis=-1, keepdims=True) + RMS_EPS)
        y = o * ng_ref[:, vs] * _silu(z_refs[h][...].astype(F32))
        o_ref[:, vs] = y.astype(o_ref.dtype)


def _gla_core(proj, glr, w2p, gate_b, norm_g, e, qk):
    l = proj.shape[0]
    nh = GLA_HEADS
    dk, dv = qk // nh, e // nh
    assert (2 * qk) % dv == 0
    v0 = 2 * qk // dv
    ng = glr.shape[1]
    c = min(GLA_STEP_CHUNKS * CHUNK, l)
    assert c % CHUNK == 0 and l % c == 0
    kern = functools.partial(_gla_core_kernel, dk=dk, dv=dv)
    vmem = (2 * (2 * c * e * 2 + 2 * c * qk * 2 + c * ng * 4 + ng * qk * 2 + c * e * 2) + nh * dv * dk * 4
            + 12 * c * qk * 4 + 4 * c * dv * 4)
    head_specs = [pl.BlockSpec((c, dv), functools.partial(lambda n, b: (n, b), b=v0 + h)) for h in range(2 * nh)]
    return pl.pallas_call(
        kern,
        grid=(l // c,),
        in_specs=[
            pl.BlockSpec((c, qk), lambda n: (n, 0)),
            pl.BlockSpec((c, qk), lambda n: (n, 1)),
            *head_specs,
            pl.BlockSpec((c, ng), lambda n: (n, 0)),
            pl.BlockSpec((ng, qk), lambda n: (0, 0)),
            pl.BlockSpec((1, qk), lambda n: (0, 0)),
            pl.BlockSpec((1, e), lambda n: (0, 0)),
        ],
        out_specs=pl.BlockSpec((c, e), lambda n: (n, 0)),
        out_shape=jax.ShapeDtypeStruct((l, e), BF16),
        scratch_shapes=[pltpu.VMEM((nh, dv, dk), F32)] + [pltpu.VMEM((c, qk), BF16) for _ in range(3)],
        compiler_params=_cparams(("arbitrary",), vmem),
        name="gla_core",
    )(*([proj] * (2 + 2 * nh)), glr, w2p, gate_b, norm_g)


def _out_ln_kernel(y_ref, w_ref, x_ref, gate_ref, g_ref, b_ref, o_ref, *scratch, alpha):
    tm = y_ref.shape[0]
    rc = min(OUT_ROW_CHUNK, tm)

    def layer_norm(r):
        mu = jnp.mean(r, axis=-1, keepdims=True)
        cen = r - mu
        var = jnp.mean(cen * cen, axis=-1, keepdims=True)
        return cen * lax.rsqrt(var + LN_EPS) * g_ref[...] + b_ref[...]

    def chunk_out(rows, x_rows):
        h = jnp.dot(y_ref[rows, :], w_ref[...], preferred_element_type=F32)
        return layer_norm(alpha * x_rows + (1.0 + gate_ref[...]) * h)

    if len(x_ref.shape) == 2:
        for r0 in range(0, tm, rc):
            rows = slice(r0, r0 + rc)
            o_ref[rows, :] = chunk_out(rows, x_ref[rows, :])
    else:
        (r_scr,) = scratch
        tc, hs, d = x_ref.shape
        assert hs == 8 and tc % 8 == 0 and rc % tc == 0
        chunks = [slice(l0, l0 + STEP_VIEW_LANES) for l0 in range(0, d, STEP_VIEW_LANES)]
        for ls in chunks:
            xs = _sublane_transpose8(_load_block_residues(x_ref, ls))
            for sl in range(hs):
                r_scr[sl * tc:(sl + 1) * tc, ls] = xs[sl].reshape(tc, STEP_VIEW_LANES)
        for r0 in range(0, tm, rc):
            rows = slice(r0, r0 + rc)
            r_scr[rows, :] = chunk_out(rows, r_scr[rows, :])
        for ls in chunks:
            outs = _sublane_transpose8([r_scr[sl * tc:(sl + 1) * tc, ls].reshape(tc // 8, 8, STEP_VIEW_LANES)
                                        for sl in range(hs)])
            for j in range(8):
                for cg in range(tc // 8):
                    o_ref[8 * cg + j, :, ls] = outs[j][cg]


def _out_ln(y, w, layer, x, gate, ln_g, ln_b, alpha, step_major=None, tm=512):
    l, e = y.shape
    d = w.shape[2]
    kern = functools.partial(_out_ln_kernel, alpha=alpha)
    if step_major is None:
        tm = min(tm, l)
        x_in, out_shape = x, (l, d)
        x_spec = pl.BlockSpec((tm, d), lambda i: (i, 0))
        scratch = []
    else:
        t, tc = step_major
        tm = tc * t // 2
        x_in = _step_major_view(x, t, tc)
        out_shape = x_in.shape
        x_spec = pl.BlockSpec((tc, None, t // 2, d), lambda i: (i // 2, i % 2, 0, 0))
        scratch = [pltpu.VMEM((tm, d), F32)]
    vmem = 2 * tm * e * 2 + e * d * 2 + 4 * tm * d * 4 + len(scratch) * tm * d * 4 + 4 * OUT_ROW_CHUNK * d * 4
    out = pl.pallas_call(
        kern,
        grid=(l // tm,),
        in_specs=[
            pl.BlockSpec((tm, e), lambda i: (i, 0)),
            pl.BlockSpec((None, e, d), lambda i: (layer, 0, 0), pipeline_mode=pl.Buffered(1)),
            x_spec,
            pl.BlockSpec((1, d), lambda i: (0, 0)),
            pl.BlockSpec((1, d), lambda i: (0, 0)),
            pl.BlockSpec((1, d), lambda i: (0, 0)),
        ],
        out_specs=x_spec,
        out_shape=jax.ShapeDtypeStruct(out_shape, F32),
        scratch_shapes=scratch,
        compiler_params=_cparams(("arbitrary",), vmem),
        name="out_proj_ln",
    )(y, w, x_in, gate, ln_g, ln_b)
    return out.reshape(l, d)


def _rot_blocks(v, nblk):
    nblk %= 8
    return pltpu.roll(v, S5_GROUP * nblk, 1) if nblk else v


def _skew_select(cols):
    blk = lax.shift_right_logical(lax.broadcasted_iota(jnp.int32, cols[0].shape, 1), 4)
    masks = [blk == j for j in range(8)]
    out = []
    for a in range(8):
        r = cols[(-a) % 8]
        for j in range(1, 8):
            r = jnp.where(masks[j], cols[(j - a) % 8], r)
        out.append(r)
    return out


def _ssm_kernel(x_ref, w_ref, p_ref, q_ref, lr_ref, li_ref, d_ref, o_ref, a_scr, y_scr, vre, vim, sre, sim):
    ntile, _, tc, _ = x_ref.shape
    nc = ntile * tc
    npair = S5_OCT // 2

    def gather_steps(tile, carry):
        r = pl.ds(pl.multiple_of(tile * tc, tc), tc)
        for s_hi in range(2):
            for g_hi in range(2):
                cols = [_rot_blocks(pltpu.bitcast(x_ref[tile, 8 * s_hi + m, :, g_hi * LANES:(g_hi + 1) * LANES], U32),
                                    m) for m in range(8)]
                res = _skew_select(cols)
                for m in range(8):
                    a_scr[8 * g_hi + m, r, s_hi * LANES:(s_hi + 1) * LANES] = pltpu.bitcast(res[m], BF16)
        return carry

    lax.fori_loop(0, ntile, gather_steps, 0)
    a_bf = a_scr

    nt = (((1,), (1,)), ((), ()))
    for p in range(npair):
        v = (lax.dot_general(a_bf[2 * p], p_ref[2 * p], nt, preferred_element_type=F32)
             + lax.dot_general(a_bf[2 * p + 1], p_ref[2 * p + 1], nt, preferred_element_type=F32))
        vre[pl.ds(p, nc, stride=npair), :] = v[:, :LANES]
        vim[pl.ds(p, nc, stride=npair), :] = v[:, LANES:]

    ar = lr_ref[...]
    ai = li_ref[...]

    def step(c, carry):
        xr, xi = carry
        rows = pl.ds(pl.multiple_of(c * npair, npair), npair)
        sre[rows, :] = xr
        sim[rows, :] = xi
        nxr = ar * xr - ai * xi + vre[rows, :]
        nxi = ar * xi + ai * xr + vim[rows, :]
        return nxr, nxi

    zero = jnp.zeros((npair, LANES), F32)
    lax.fori_loop(0, nc, step, (zero, zero), unroll=8)

    for p in range(npair):
        s = jnp.concatenate([sre[pl.ds(p, nc, stride=npair), :], sim[pl.ds(p, nc, stride=npair), :]],
                            axis=1).astype(BF16)
        for g in (2 * p, 2 * p + 1):
            y_scr[g] = (jnp.dot(a_bf[g], w_ref[g], preferred_element_type=F32)
                        + jnp.dot(s, q_ref[g], preferred_element_type=F32))

    rbo = min(32, tc)
    per_tile = tc // rbo

    def scatter_steps(it, carry):
        r = pl.ds(pl.multiple_of(it * rbo, rbo), rbo)
        tile = it // per_tile
        rt = pl.ds(pl.multiple_of((it % per_tile) * rbo, rbo), rbo)
        for s_hi in range(2):
            for g_hi in range(2):
                cs = slice(g_hi * LANES, (g_hi + 1) * LANES)
                cols = [y_scr[8 * g_hi + m, r, s_hi * LANES:(s_hi + 1) * LANES] for m in range(8)]
                res = _skew_select(cols)
                dsk = d_ref[:, cs]
                for m in range(8):
                    s = 8 * s_hi + m
                    y = _rot_blocks(res[m], -m) + dsk * x_ref[tile, s, rt, cs].astype(F32)
                    o_ref[tile, s, rt, cs] = _gelu_tanh(y).astype(o_ref.dtype)
        return carry

    lax.fori_loop(0, nc // rbo, scatter_steps, 0)


def _ssm_core(uz, tc, wt, pm, qm, lam_r, lam_i, d_skip, e):
    t = S5_T
    nc = uz.shape[0] // t
    kk = t * S5_GROUP
    assert kk == 2 * LANES and S5_OCT * S5_GROUP == kk
    noct = e // kk
    npair = S5_OCT // 2
    ntile = nc // tc
    x4 = uz.reshape(ntile, t, tc, uz.shape[1])
    blk = (ntile, t, tc, kk)
    vmem = (2 * (2 * t * nc * kk * 2 + 3 * S5_OCT * kk * kk * 2) + S5_OCT * nc * kk * (2 + 4)
            + 4 * nc * npair * LANES * 4)
    out = pl.pallas_call(
        _ssm_kernel,
        grid=(noct,),
        in_specs=[
            pl.BlockSpec(blk, lambda i: (0, 0, 0, i)),
            pl.BlockSpec((S5_OCT, kk, kk), lambda i: (i, 0, 0)),
            pl.BlockSpec((S5_OCT, kk, kk), lambda i: (i, 0, 0)),
            pl.BlockSpec((S5_OCT, kk, kk), lambda i: (i, 0, 0)),
            pl.BlockSpec((None, npair, LANES), lambda i: (i, 0, 0)),
            pl.BlockSpec((None, npair, LANES), lambda i: (i, 0, 0)),
            pl.BlockSpec((1, kk), lambda i: (0, i)),
        ],
        out_specs=pl.BlockSpec(blk, lambda i: (0, 0, 0, i)),
        out_shape=jax.ShapeDtypeStruct((ntile, t, tc, e), BF16),
        scratch_shapes=[pltpu.VMEM((S5_OCT, nc, kk), BF16), pltpu.VMEM((S5_OCT, nc, kk), F32)]
        + [pltpu.VMEM((nc * npair, LANES), F32) for _ in range(4)],
        compiler_params=_cparams(("arbitrary",), vmem),
        name="s5_ssm",
    )(x4, wt, pm, qm, lam_r, lam_i, d_skip)
    return out.reshape(t * nc, e)


def _cmul(ar, ai, br, bi):
    return ar * br - ai * bi, ar * bi + ai * br


def _zoh(a_re, a_im, dt):
    mag = jnp.exp(a_re * dt)
    return mag * jnp.cos(a_im * dt), mag * jnp.sin(a_im * dt)


def _s5_ops_kernel(are_ref, aim_ref, ldt_ref, bre_ref, bim_ref, cre_ref, cim_ref, arp_ref, aip_ref, ldtp_ref,
                   w_ref, pt_ref, q_ref, lamr_ref, lami_ref, rows_scr):
    t, gi = S5_T, S5_GROUP
    ns = are_ref.shape[1]
    kk = t * gi

    pr, pi = _zoh(arp_ref[...], aip_ref[...], jnp.exp(ldtp_ref[...]))
    for _ in range(t.bit_length() - 1):
        pr, pi = _cmul(pr, pi, pr, pi)
    lamr_ref[...] = pr
    lami_ref[...] = pi

    ar = are_ref[...]
    ai = aim_ref[...]
    l1r, l1i = _zoh(ar, ai, jnp.exp(ldt_ref[...]))
    nr = l1r - 1.0
    den = ar * ar + ai * ai
    quantities = [l1r, l1i]
    for _ in range(3):
        quantities += list(_cmul(quantities[-2], quantities[-1], quantities[-2], quantities[-1]))
    quantities += [(nr * ar + l1i * ai) / den, (l1i * ar - nr * ai) / den]
    for k, val in enumerate(quantities):
        rows_scr[k] = val

    eye = lax.broadcasted_iota(jnp.int32, (ns, ns), 0) == lax.broadcasted_iota(jnp.int32, (ns, ns), 1)
    lane = lax.broadcasted_iota(jnp.int32, (ns, kk), 1)
    tau = lax.shift_right_logical(lane, 4)
    bits = [(lax.shift_right_logical(tau, b) & 1) == 1 for b in range(4)]
    expand = (lax.broadcasted_iota(jnp.int32, (gi, kk), 0)
              == (lax.broadcasted_iota(jnp.int32, (gi, kk), 1) & (gi - 1))).astype(F32)
    lane_w = lax.broadcasted_iota(jnp.int32, (gi, LANES), 1)
    tn_dims = (((0,), (0,)), ((), ()))

    def column(k, g):
        row = rows_scr[k, pl.ds(g, 1), :]
        return jnp.sum(jnp.where(eye, row, 0.0), axis=1, keepdims=True)

    def lane_powers(cols, flip):
        pr = pi = None
        for b in range(4):
            on = jnp.logical_not(bits[b]) if flip else bits[b]
            fr = jnp.where(on, cols[2 * b], 1.0)
            fi = jnp.where(on, cols[2 * b + 1], 0.0)
            pr, pi = (fr, fi) if pr is None else _cmul(pr, pi, fr, fi)
        return pr, pi

    def build(g, odd):
        cols = [column(k, g) for k in range(10)]
        cr, ci = cols[8], cols[9]
        bbr = cr * bre_ref[g] - ci * bim_ref[g]
        bbi = cr * bim_ref[g] + ci * bre_ref[g]
        pwr, pwi = lane_powers(cols, False)
        rvr, rvi = lane_powers(cols, True)
        p1r, p1i = _cmul(pwr, pwi, cols[0], cols[1])
        ctr = lax.dot_general(cre_ref[g], expand, tn_dims, precision=HIGHEST, preferred_element_type=F32)
        cti = lax.dot_general(cim_ref[g], expand, tn_dims, precision=HIGHEST, preferred_element_type=F32)
        bbtr = jnp.dot(bbr, expand, precision=HIGHEST, preferred_element_type=F32)
        bbti = jnp.dot(bbi, expand, precision=HIGHEST, preferred_element_type=F32)
        clr, cli = _cmul(ctr, cti, pwr, pwi)
        kt = (lax.dot_general(bbr, clr, tn_dims, precision=HIGHEST, preferred_element_type=F32)
              - lax.dot_general(bbi, cli, tn_dims, precision=HIGHEST, preferred_element_type=F32))
        gm = g % 8

        def rot_halves(v):
            return jnp.concatenate([_rot_blocks(v[:, :LANES], gm), _rot_blocks(v[:, LANES:], gm)], axis=1)

        k0, k1 = kt[:, :LANES], kt[:, LANES:]
        for s in range(t):
            sh = (gi * s) % LANES
            r0 = pltpu.roll(k0, sh, 1) if sh else k0
            if gi * s < LANES:
                r1 = pltpu.roll(k1, sh, 1) if sh else k1
                lo = jnp.where(lane_w >= sh, r0, 0.0)
                hi = jnp.where(lane_w >= sh, r1, r0)
            else:
                lo = jnp.zeros_like(k0)
                hi = jnp.where(lane_w >= sh, r0, 0.0)
            row0 = gi * (8 * (s // 8) + (s + gm) % 8)
            w_ref[g, row0:row0 + gi, :] = rot_halves(jnp.concatenate([lo, hi], axis=1)).astype(w_ref.dtype)
        ptr, pti = _cmul(rvr, rvi, bbtr, bbti)
        qr, qi = _cmul(ctr, cti, p1r, p1i)
        zero = jnp.zeros((ns, kk), pt_ref.dtype)
        for ref, re, im in ((pt_ref, ptr, pti), (q_ref, qr, -qi)):
            ref[g, odd * ns:(odd + 1) * ns, :] = rot_halves(re).astype(ref.dtype)
            ref[g, (1 - odd) * ns:(2 - odd) * ns, :] = zero
            ref[g, (2 + odd) * ns:(3 + odd) * ns, :] = rot_halves(im).astype(ref.dtype)
            ref[g, (3 - odd) * ns:(4 - odd) * ns, :] = zero

    for g in range(S5_OCT):
        build(g, g % 2)


def _s5_operators(a_re, a_im, log_dt, b_re, b_im, c_re, c_im):
    g, ns = a_re.shape
    t, gi = S5_T, S5_GROUP
    kk = t * gi
    assert t == 16 and 4 * ns == kk and 2 * ns == LANES
    noct = g // S5_OCT
    npair = S5_OCT // 2
    a_re_p = a_re.reshape(g // 2, 2 * ns)
    a_im_p = a_im.reshape(g // 2, 2 * ns)
    ldt_p = jnp.broadcast_to(log_dt[:, None], (g, ns)).reshape(g // 2, 2 * ns)
    oct3 = lambda i: (i, 0, 0)
    ops_shape = jax.ShapeDtypeStruct((g, kk, kk), BF16)
    lam_shape = jax.ShapeDtypeStruct((noct, npair, LANES), F32)
    return pl.pallas_call(
        _s5_ops_kernel,
        grid=(noct,),
        in_specs=[
            pl.BlockSpec((S5_OCT, ns), lambda i: (i, 0)),
            pl.BlockSpec((S5_OCT, ns), lambda i: (i, 0)),
            pl.BlockSpec((S5_OCT, 1), lambda i: (i, 0)),
            pl.BlockSpec((S5_OCT, ns, gi), oct3),
            pl.BlockSpec((S5_OCT, ns, gi), oct3),
            pl.BlockSpec((S5_OCT, gi, ns), oct3),
            pl.BlockSpec((S5_OCT, gi, ns), oct3),
            pl.BlockSpec((npair, LANES), lambda i: (i, 0)),
            pl.BlockSpec((npair, LANES), lambda i: (i, 0)),
            pl.BlockSpec((npair, LANES), lambda i: (i, 0)),
        ],
        out_specs=[pl.BlockSpec((S5_OCT, kk, kk), oct3)] * 3 + [pl.BlockSpec((None, npair, LANES), oct3)] * 2,
        out_shape=[ops_shape] * 3 + [lam_shape] * 2,
        scratch_shapes=[pltpu.VMEM((10, S5_OCT, ns), F32)],
        compiler_params=_cparams(("arbitrary",), 2 * 3 * S5_OCT * kk * kk * 2 + 4 * S5_OCT * ns * LANES * 4 * 2),
        name="s5_ops",
    )(a_re, a_im, log_dt.reshape(g, 1), b_re, b_im, c_re, c_im, a_re_p, a_im_p, ldt_p)


def _glu_kernel(ya_ref, w_ref, b_ref, yc_ref, z_ref, o_ref):
    w = w_ref[...].astype(BF16)
    rc = min(GLU_ROW_CHUNK, ya_ref.shape[0])
    for r0 in range(0, ya_ref.shape[0], rc):
        rows = slice(r0, r0 + rc)
        acc = jnp.dot(ya_ref[rows, :], w, preferred_element_type=F32) + b_ref[...]
        o_ref[rows, :] = (yc_ref[rows, :].astype(F32) * _sigmoid(acc)
                          * _silu(z_ref[rows, :].astype(F32))).astype(o_ref.dtype)


def _glu(yact, w_glu, layer, b_glu, uz, tm=1024, tn=512):
    l, e = yact.shape
    tm = min(tm, l)
    zoff = e // tn
    vmem = 2 * (tm * e * 2 + e * tn * 4 + 3 * tm * tn * 2)
    return pl.pallas_call(
        _glu_kernel,
        grid=(l // tm, e // tn),
        in_specs=[pl.BlockSpec((tm, e), lambda i, j: (i, 0)),
                  pl.BlockSpec((None, e, tn), lambda i, j: (layer, 0, j)),
                  pl.BlockSpec((1, tn), lambda i, j: (0, j)),
                  pl.BlockSpec((tm, tn), lambda i, j: (i, j)),
                  pl.BlockSpec((tm, tn), lambda i, j: (i, zoff + j))],
        out_specs=pl.BlockSpec((tm, tn), lambda i, j: (i, j)),
        out_shape=jax.ShapeDtypeStruct((l, e), BF16),
        compiler_params=_cparams(("arbitrary", "arbitrary"), vmem),
        name="s5_glu",
    )(yact, w_glu, b_glu, yact, uz)


def _gla_layer(x, scale, shift, gate, ln_g, ln_b, alpha, w_in, layer, gate_w2, gate_b, norm_g, w_out):
    d = x.shape[1]
    e = w_out.shape[1]
    qk = gate_w2.shape[1]
    wg = jnp.pad(w_in[layer, :, 2 * qk + 2 * e:], ((0, 0), (0, LANES - GLA_GATE_RANK)))
    w2p = jnp.pad(gate_w2, ((0, LANES - GLA_GATE_RANK), (0, 0))).astype(BF16)
    proj, glr = _in_proj_gate(x, scale, shift, jnp.swapaxes(w_in, 1, 2), layer, wg)
    y = _gla_core(proj, glr, w2p, gate_b.reshape(1, qk), norm_g.reshape(1, e), e, qk)
    return _out_ln(y, w_out, layer, x, gate, ln_g.reshape(1, d), ln_b.reshape(1, d), alpha)


def _s5_layer(x, scale, shift, gate, ln_g, ln_b, alpha, w_in, layer, a_re, a_im, log_dt, b_re, b_im, c_re, c_im,
              d_skip, w_glu, b_glu, w_out):
    l, d = x.shape
    e = w_out.shape[1]
    t = S5_T
    tc = min(S5_TILE_BLOCKS, l // t)
    uz = _in_proj_perm(x, t, tc, scale, shift, w_in, layer)
    wt, pm, qm, lam_r, lam_i = _s5_operators(a_re, a_im, log_dt, b_re, b_im, c_re, c_im)
    yact = _ssm_core(uz, tc, wt, pm, qm, lam_r, lam_i, d_skip.reshape(1, e), e)
    yglu = _glu(yact, w_glu, layer, b_glu.reshape(1, e), uz)
    return _out_ln(yglu, w_out, layer, x, gate, ln_g.reshape(1, d), ln_b.reshape(1, d), alpha,
                   step_major=(t, tc))


def kernel(x, c, ln_g, ln_b, ada_w, ada_b, gla_w_in, gla_gate_w2, gla_gate_b, gla_norm_g, gla_w_out,
           s5_w_in, s5_a_re, s5_a_im, s5_log_dt, s5_b_re, s5_b_im, s5_c_re, s5_c_im, s5_d,
           s5_w_glu, s5_b_glu, s5_w_out):
    bsz, l, d = x.shape
    assert bsz == 1, "batch 1 only"
    depth = ln_g.shape[0]
    alpha = (2 * depth) ** 0.25
    mod = _adaln(c, ada_w, ada_b)
    h = x.reshape(l, d)
    gla_w_out_b = gla_w_out.astype(BF16)
    s5_w_out_b = s5_w_out.astype(BF16)
    for i in range(depth):
        shift, scale, gate = mod[i, :, :d], mod[i, :, d:2 * d], mod[i, :, 2 * d:]
        j = i // 2
        if i % 2 == 0:
            h = _gla_layer(h, scale, shift, gate, ln_g[i], ln_b[i], alpha, gla_w_in, j, gla_gate_w2[j],
                           gla_gate_b[j], gla_norm_g[j], gla_w_out_b)
        else:
            h = _s5_layer(h, scale, shift, gate, ln_g[i], ln_b[i], alpha, s5_w_in, j, s5_a_re[j], s5_a_im[j],
                          s5_log_dt[j], s5_b_re[j], s5_b_im[j], s5_c_re[j], s5_c_im[j], s5_d[j],
                          s5_w_glu, s5_b_glu[j], s5_w_out_b)
    return h.reshape(bsz, l, d)
```

```python
import functools
import math

import jax
import jax.numpy as jnp
from jax import lax
from jax.experimental import pallas as pl
from jax.experimental.pallas import tpu as pltpu

F32 = jnp.float32
BF16 = jnp.bfloat16
U32 = jnp.uint32

CHUNK = 64
GLA_HEADS = 4
GLA_GATE_RANK = 16
GLA_TAU = 16.0
S5_GROUP = 16
S5_STATE = 64
LN_EPS = 1e-5
RMS_EPS = 1e-6

S5_T = 16
S5_OCT = 16
GLA_STEP_CHUNKS = 4
S5_TILE_BLOCKS = 64
GLU_ROW_CHUNK = 256
OUT_ROW_CHUNK = 128
OPS_WAVE = 4
LANES = 128
V7X_SCOPED_VMEM_CAP = 60000 * 1024


def _cparams(semantics, vmem_bytes):
    limit = min(int(vmem_bytes) + (6 << 20), V7X_SCOPED_VMEM_CAP)
    return pltpu.CompilerParams(dimension_semantics=semantics, vmem_limit_bytes=limit)


def _sigmoid(x):
    return 1.0 / (1.0 + jnp.exp(-x))


def _silu(x):
    return x * _sigmoid(x)


def _gelu_tanh(y):
    cdf = 0.5 * (1.0 + jnp.tanh(math.sqrt(2.0 / math.pi) * (y + 0.044715 * (y * y * y))))
    return y * cdf


def _adaln_kernel(c_ref, w_ref, b_ref, o_ref):
    c = c_ref[...]
    o_ref[...] = jnp.sum(_silu(c) * w_ref[...], axis=0, keepdims=True) + b_ref[...]


def _adaln(c, ada_w, ada_b, tn=1024):
    depth, d, n3 = ada_w.shape
    assert c.shape == (1, d), "batch 1 only"
    c_col = c.reshape(d, 1)
    out = pl.pallas_call(
        _adaln_kernel,
        grid=(depth, n3 // tn),
        in_specs=[
            pl.BlockSpec((d, 1), lambda l, j: (0, 0)),
            pl.BlockSpec((None, d, tn), lambda l, j: (l, 0, j)),
            pl.BlockSpec((None, 1, tn), lambda l, j: (l, 0, j)),
        ],
        out_specs=pl.BlockSpec((None, 1, tn), lambda l, j: (l, 0, j)),
        out_shape=jax.ShapeDtypeStruct((depth, 1, n3), F32),
        compiler_params=_cparams(("arbitrary", "arbitrary"), 2 * d * tn * 4 + d * LANES * 4),
        name="adaln_mod",
    )(c_col, ada_w, ada_b.reshape(depth, 1, n3))
    return out


def _modulate(x_ref, sc_ref, sh_ref):
    return (x_ref[...] * (1.0 + sc_ref[...]) + sh_ref[...]).astype(BF16)


def _proj_gate_kernel(x_ref, sc_ref, sh_ref, w_ref, wg_ref, o_ref, g_ref, u_scr):
    @pl.when(pl.program_id(1) == 0)
    def _():
        u = _modulate(x_ref, sc_ref, sh_ref)
        u_scr[...] = u
        g_ref[...] = jnp.dot(u, wg_ref[...].astype(BF16), preferred_element_type=F32)

    o_ref[...] = lax.dot_general(u_scr[...], w_ref[...].astype(BF16), (((1,), (1,)), ((), ())),
                                 preferred_element_type=F32).astype(o_ref.dtype)


def _sublane_transpose8(arrs):
    sub = lax.broadcasted_iota(jnp.int32, arrs[0].shape, 1)
    for b in range(3):
        dist = 1 << b
        sel = (sub & dist) != 0
        new = list(arrs)
        for j in range(8):
            if not j & dist:
                lo, hi = arrs[j], arrs[j + dist]
                new[j] = jnp.where(sel, pltpu.roll(hi, dist, 1), lo)
                new[j + dist] = jnp.where(sel, hi, pltpu.roll(lo, 8 - dist, 1))
        arrs = new
    return arrs


STEP_VIEW_LANES = 512


def _load_block_residues(x_ref, ls):
    tc = x_ref.shape[0]
    return [jnp.stack([x_ref[8 * cg + j, :, ls] for cg in range(tc // 8)]) for j in range(8)]


def _proj_perm_kernel(xa_ref, xb_ref, sc_ref, sh_ref, w_ref, o_ref, u_scr):
    tc, hs, d = xa_ref.shape
    assert hs == 8 and tc % 8 == 0

    @pl.when(pl.program_id(1) == 0)
    def _():
        for half, x_ref in enumerate((xa_ref, xb_ref)):
            for l0 in range(0, d, STEP_VIEW_LANES):
                ls = slice(l0, l0 + STEP_VIEW_LANES)
                rows = _sublane_transpose8(_load_block_residues(x_ref, ls))
                for sl in range(hs):
                    s = hs * half + sl
                    u = rows[sl].reshape(tc, STEP_VIEW_LANES) * (1.0 + sc_ref[:, ls]) + sh_ref[:, ls]
                    u_scr[s * tc:(s + 1) * tc, ls] = u.astype(BF16)

    o_ref[...] = jnp.dot(u_scr[...], w_ref[...].astype(BF16), preferred_element_type=F32).astype(o_ref.dtype)


def _in_proj_gate(x, scale, shift, wt, layer, wg, tm=1024, tn=1024):
    l, d = x.shape
    n = (wt.shape[1] // tn) * tn
    ng = wg.shape[1]
    tm = min(tm, l)
    vmem = 2 * tm * d * 4 + 2 * d * tn * 4 + 2 * tm * tn * 2 + tm * d * 2 + 2 * d * ng * 4 + 2 * tm * ng * 4
    return pl.pallas_call(
        _proj_gate_kernel,
        grid=(l // tm, n // tn),
        in_specs=[
            pl.BlockSpec((tm, d), lambda i, j: (i, 0)),
            pl.BlockSpec((1, d), lambda i, j: (0, 0)),
            pl.BlockSpec((1, d), lambda i, j: (0, 0)),
            pl.BlockSpec((None, tn, d), lambda i, j: (layer, j, 0)),
            pl.BlockSpec((d, ng), lambda i, j: (0, 0)),
        ],
        out_specs=[pl.BlockSpec((tm, tn), lambda i, j: (i, j)),
                   pl.BlockSpec((tm, ng), lambda i, j: (i, 0))],
        out_shape=[jax.ShapeDtypeStruct((l, n), BF16), jax.ShapeDtypeStruct((l, ng), F32)],
        scratch_shapes=[pltpu.VMEM((tm, d), BF16)],
        compiler_params=_cparams(("arbitrary", "arbitrary"), vmem),
        name="in_proj_gate",
    )(x, scale, shift, wt, wg)


def _step_major_view(x, t, tc):
    l, d = x.shape
    assert l % (t * tc) == 0 and (t // 2) % 8 == 0
    return x.reshape(l // t, 2, t // 2, d)


def _in_proj_perm(x, t, tc, scale, shift, w, layer, tn=1024):
    l, d = x.shape
    x4 = _step_major_view(x, t, tc)
    n = w.shape[2]
    tm = t * tc
    vmem = 2 * tm * d * 4 + 2 * d * tn * 4 + 2 * tm * tn * 2 + tm * d * 2
    return pl.pallas_call(
        _proj_perm_kernel,
        grid=(l // tm, n // tn),
        in_specs=[
            pl.BlockSpec((tc, None, t // 2, d), lambda i, j: (i, 0, 0, 0)),
            pl.BlockSpec((tc, None, t // 2, d), lambda i, j: (i, 1, 0, 0)),
            pl.BlockSpec((1, d), lambda i, j: (0, 0)),
            pl.BlockSpec((1, d), lambda i, j: (0, 0)),
            pl.BlockSpec((None, d, tn), lambda i, j: (layer, 0, j)),
        ],
        out_specs=pl.BlockSpec((tm, tn), lambda i, j: (i, j)),
        out_shape=jax.ShapeDtypeStruct((l, n), BF16),
        scratch_shapes=[pltpu.VMEM((tm, d), BF16)],
        compiler_params=_cparams(("arbitrary", "arbitrary"), vmem),
        name="in_proj_perm",
    )(x4, x4, scale, shift, w)


def _gla_core_kernel(*refs, dk, dv):
    nh = GLA_HEADS
    q_ref, k_ref = refs[0], refs[1]
    v_refs = refs[2:2 + nh]
    z_refs = refs[2 + nh:2 + 2 * nh]
    g_ref, w2_ref, gb_ref, ng_ref, o_ref, st_ref, kd_scr, ko_scr, qd_scr = refs[2 + 2 * nh:]

    @pl.when(pl.program_id(0) == 0)
    def _():
        st_ref[...] = jnp.zeros_like(st_ref)

    c = CHUNK
    r = q_ref.shape[0]
    sb = r // c
    nt = (((1,), (1,)), ((), ()))
    pre = jnp.dot(g_ref[...].astype(BF16), w2_ref[...], preferred_element_type=F32) + gb_ref[...]
    la = (jnp.minimum(pre, 0.0) - jnp.log(1.0 + jnp.exp(-jnp.abs(pre)))) * (1.0 / GLA_TAU)
    row = lax.broadcasted_iota(jnp.int32, (r, r), 0)
    col = lax.broadcasted_iota(jnp.int32, (r, r), 1)
    tri = (col <= row).astype(BF16)
    la_hi = la.astype(BF16)
    la_lo = (la - la_hi.astype(F32)).astype(BF16)
    gc = (jnp.dot(tri, la_hi, preferred_element_type=F32) + jnp.dot(tri, la_lo, preferred_element_type=F32))
    ends = [gc[c * (j + 1) - 1:c * (j + 1), :] for j in range(sb)]
    e_rows = jnp.concatenate([jnp.broadcast_to(ends[j], (c, gc.shape[1])) for j in range(sb)], axis=0)
    kf = k_ref[...].astype(F32) * jnp.exp(e_rows - gc)
    kd_scr[...] = kf.astype(BF16)
    ko_scr[...] = (kf * jnp.exp(ends[-1] - e_rows)).astype(BF16)
    qd_scr[...] = (q_ref[...].astype(F32) * jnp.exp(e_rows)).astype(BF16)
    dec = jnp.exp(ends[-1])
    for h in range(nh):
        ks = slice(h * dk, (h + 1) * dk)
        vs = slice(h * dv, (h + 1) * dv)
        st_in = st_ref[h]
        base = lax.dot_general(qd_scr[:, ks], st_in.astype(BF16), nt, preferred_element_type=F32)
        ams = []
        for m in range(sb):
            qs = jnp.concatenate(
                [(q_ref[c * j:c * (j + 1), ks].astype(F32) * jnp.exp(ends[j][:, ks] - ends[m][:, ks])).astype(BF16)
                 if j > m else q_ref[c * j:c * (j + 1), ks] for j in range(m, sb)], axis=0)
            kpad = jnp.concatenate(
                ([jnp.zeros((c * m, dk), BF16)] if m else []) + [kd_scr[c * m:c * (m + 1), ks]]
                + ([jnp.zeros((c * (sb - 1 - m), dk), BF16)] if m < sb - 1 else []), axis=0)
            ams.append(lax.dot_general(qs, kpad, nt, preferred_element_type=F32))
        a_rows = []
        for j in range(sb):
            aj = ams[0][c * j:c * (j + 1)]
            for m in range(1, j + 1):
                aj = aj + ams[m][c * (j - m):c * (j - m + 1)]
            a_rows.append(aj)
        a_full = jnp.concatenate(a_rows, axis=0).astype(BF16)
        o = base + jnp.dot(a_full, v_refs[h][...], preferred_element_type=F32)
        upd = lax.dot_general(v_refs[h][...], ko_scr[:, ks], (((0,), (0,)), ((), ())),
                              preferred_element_type=F32)
        st_ref[h] = dec[:, ks] * st_in + upd
        o = o * (dk ** -0.5)
        o = o * lax.rsqrt(jnp.mean(o * o, axis=-1, keepdims=True) + RMS_EPS)
        y = o * ng_ref[:, vs] * _silu(z_refs[h][...].astype(F32))
        o_ref[:, vs] = y.astype(o_ref.dtype)


def _gla_core(proj, glr, w2p, gate_b, norm_g, e, qk):
    l = proj.shape[0]
    nh = GLA_HEADS
    dk, dv = qk // nh, e // nh
    assert (2 * qk) % dv == 0
    v0 = 2 * qk // dv
    ng = glr.shape[1]
    c = min(GLA_STEP_CHUNKS * CHUNK, l)
    assert c % CHUNK == 0 and l % c == 0
    kern = functools.partial(_gla_core_kernel, dk=dk, dv=dv)
    vmem = (2 * (2 * c * e * 2 + 2 * c * qk * 2 + c * ng * 4 + ng * qk * 2 + c * e * 2) + nh * dv * dk * 4
            + 12 * c * qk * 4 + 4 * c * dv * 4)
    head_specs = [pl.BlockSpec((c, dv), functools.partial(lambda n, b: (n, b), b=v0 + h)) for h in range(2 * nh)]
    return pl.pallas_call(
        kern,
        grid=(l // c,),
        in_specs=[
            pl.BlockSpec((c, qk), lambda n: (n, 0)),
            pl.BlockSpec((c, qk), lambda n: (n, 1)),
            *head_specs,
            pl.BlockSpec((c, ng), lambda n: (n, 0)),
            pl.BlockSpec((ng, qk), lambda n: (0, 0)),
            pl.BlockSpec((1, qk), lambda n: (0, 0)),
            pl.BlockSpec((1, e), lambda n: (0, 0)),
        ],
        out_specs=pl.BlockSpec((c, e), lambda n: (n, 0)),
        out_shape=jax.ShapeDtypeStruct((l, e), BF16),
        scratch_shapes=[pltpu.VMEM((nh, dv, dk), F32)] + [pltpu.VMEM((c, qk), BF16) for _ in range(3)],
        compiler_params=_cparams(("arbitrary",), vmem),
        name="gla_core",
    )(*([proj] * (2 + 2 * nh)), glr, w2p, gate_b, norm_g)


def _out_ln_kernel(y_ref, w_ref, x_ref, gate_ref, g_ref, b_ref, o_ref, *scratch, alpha):
    tm = y_ref.shape[0]
    rc = min(OUT_ROW_CHUNK, tm)

    def layer_norm(r):
        mu = jnp.mean(r, axis=-1, keepdims=True)
        cen = r - mu
        var = jnp.mean(cen * cen, axis=-1, keepdims=True)
        return cen * lax.rsqrt(var + LN_EPS) * g_ref[...] + b_ref[...]

    def chunk_out(rows, x_rows):
        h = jnp.dot(y_ref[rows, :], w_ref[...], preferred_element_type=F32)
        return layer_norm(alpha * x_rows + (1.0 + gate_ref[...]) * h)

    if len(x_ref.shape) == 2:
        for r0 in range(0, tm, rc):
            rows = slice(r0, r0 + rc)
            o_ref[rows, :] = chunk_out(rows, x_ref[rows, :])
    else:
        (r_scr,) = scratch
        tc, hs, d = x_ref.shape
        assert hs == 8 and tc % 8 == 0 and rc % tc == 0
        chunks = [slice(l0, l0 + STEP_VIEW_LANES) for l0 in range(0, d, STEP_VIEW_LANES)]
        for ls in chunks:
            xs = _sublane_transpose8(_load_block_residues(x_ref, ls))
            for sl in range(hs):
                r_scr[sl * tc:(sl + 1) * tc, ls] = xs[sl].reshape(tc, STEP_VIEW_LANES)
        for r0 in range(0, tm, rc):
            rows = slice(r0, r0 + rc)
            r_scr[rows, :] = chunk_out(rows, r_scr[rows, :])
        for ls in chunks:
            outs = _sublane_transpose8([r_scr[sl * tc:(sl + 1) * tc, ls].reshape(tc // 8, 8, STEP_VIEW_LANES)
                                        for sl in range(hs)])
            for j in range(8):
                for cg in range(tc // 8):
                    o_ref[8 * cg + j, :, ls] = outs[j][cg]


def _out_ln(y, w, layer, x, gate, ln_g, ln_b, alpha, step_major=None, tm=512):
    l, e = y.shape
    d = w.shape[2]
    kern = functools.partial(_out_ln_kernel, alpha=alpha)
    if step_major is None:
        tm = min(tm, l)
        x_in, out_shape = x, (l, d)
        x_spec = pl.BlockSpec((tm, d), lambda i: (i, 0))
        scratch = []
    else:
        t, tc = step_major
        tm = tc * t // 2
        x_in = _step_major_view(x, t, tc)
        out_shape = x_in.shape
        x_spec = pl.BlockSpec((tc, None, t // 2, d), lambda i: (i // 2, i % 2, 0, 0))
        scratch = [pltpu.VMEM((tm, d), F32)]
    vmem = 2 * tm * e * 2 + e * d * 2 + 4 * tm * d * 4 + len(scratch) * tm * d * 4 + 4 * OUT_ROW_CHUNK * d * 4
    out = pl.pallas_call(
        kern,
        grid=(l // tm,),
        in_specs=[
            pl.BlockSpec((tm, e), lambda i: (i, 0)),
            pl.BlockSpec((None, e, d), lambda i: (layer, 0, 0), pipeline_mode=pl.Buffered(1)),
            x_spec,
            pl.BlockSpec((1, d), lambda i: (0, 0)),
            pl.BlockSpec((1, d), lambda i: (0, 0)),
            pl.BlockSpec((1, d), lambda i: (0, 0)),
        ],
        out_specs=x_spec,
        out_shape=jax.ShapeDtypeStruct(out_shape, F32),
        scratch_shapes=scratch,
        compiler_params=_cparams(("arbitrary",), vmem),
        name="out_proj_ln",
    )(y, w, x_in, gate, ln_g, ln_b)
    return out.reshape(l, d)


def _rot_blocks(v, nblk):
    nblk %= 8
    return pltpu.roll(v, S5_GROUP * nblk, 1) if nblk else v


def _skew_select(cols):
    blk = lax.shift_right_logical(lax.broadcasted_iota(jnp.int32, cols[0].shape, 1), 4)
    masks = [blk == j for j in range(8)]
    out = []
    for a in range(8):
        r = cols[(-a) % 8]
        for j in range(1, 8):
            r = jnp.where(masks[j], cols[(j - a) % 8], r)
        out.append(r)
    return out


def _ssm_kernel(x_ref, w_ref, p_ref, q_ref, lr_ref, li_ref, d_ref, o_ref, a_scr, y_scr, vre, vim, sre, sim):
    ntile, t_steps, tc, _ = x_ref.shape
    nc = ntile * tc
    npair = S5_OCT // 2

    def gather_steps(tile, carry):
        r = pl.ds(pl.multiple_of(tile * tc, tc), tc)
        for s_hi in range(2):
            for g_hi in range(2):
                cols = [_rot_blocks(pltpu.bitcast(x_ref[tile, t_steps - 1 - (8 * s_hi + m), :,
                                                        g_hi * LANES:(g_hi + 1) * LANES], U32), m) for m in range(8)]
                res = _skew_select(cols)
                for m in range(8):
                    a_scr[8 * g_hi + m, r, s_hi * LANES:(s_hi + 1) * LANES] = pltpu.bitcast(res[m], BF16)
        return carry

    lax.fori_loop(0, ntile, gather_steps, 0)
    a_bf = a_scr

    nt = (((1,), (1,)), ((), ()))
    for p in range(npair):
        v = (lax.dot_general(a_bf[2 * p], p_ref[2 * p], nt, preferred_element_type=F32)
             + lax.dot_general(a_bf[2 * p + 1], p_ref[2 * p + 1], nt, preferred_element_type=F32))
        vre[pl.ds(p, nc, stride=npair), :] = v[:, :LANES]
        vim[pl.ds(p, nc, stride=npair), :] = v[:, LANES:]

    ar = lr_ref[...]
    ai = li_ref[...]

    def step(c, carry):
        xr, xi = carry
        rows = pl.ds(pl.multiple_of(c * npair, npair), npair)
        sre[rows, :] = xr
        sim[rows, :] = xi
        nxr = ar * xr - ai * xi + vre[rows, :]
        nxi = ar * xi + ai * xr + vim[rows, :]
        return nxr, nxi

    zero = jnp.zeros((npair, LANES), F32)
    lax.fori_loop(0, nc, step, (zero, zero), unroll=8)

    for p in range(npair):
        s = jnp.concatenate([sre[pl.ds(p, nc, stride=npair), :], sim[pl.ds(p, nc, stride=npair), :]],
                            axis=1).astype(BF16)
        for g in (2 * p, 2 * p + 1):
            y_scr[g] = (jnp.dot(a_bf[g], w_ref[g], preferred_element_type=F32)
                        + jnp.dot(s, q_ref[g], preferred_element_type=F32))

    rbo = min(32, tc)
    per_tile = tc // rbo

    def scatter_steps(it, carry):
        r = pl.ds(pl.multiple_of(it * rbo, rbo), rbo)
        tile = it // per_tile
        rt = pl.ds(pl.multiple_of((it % per_tile) * rbo, rbo), rbo)
        for s_hi in range(2):
            for g_hi in range(2):
                cs = slice(g_hi * LANES, (g_hi + 1) * LANES)
                cols = [y_scr[8 * g_hi + m, r, s_hi * LANES:(s_hi + 1) * LANES] for m in range(8)]
                res = _skew_select(cols)
                dsk = d_ref[:, cs]
                for m in range(8):
                    s = 8 * s_hi + m
                    y = _rot_blocks(res[m], -m) + dsk * x_ref[tile, s, rt, cs].astype(F32)
                    o_ref[tile, s, rt, cs] = _gelu_tanh(y).astype(o_ref.dtype)
        return carry

    lax.fori_loop(0, nc // rbo, scatter_steps, 0)


def _ssm_core(uz, tc, wt, pm, qm, lam_r, lam_i, d_skip, e):
    t = S5_T
    nc = uz.shape[0] // t
    kk = t * S5_GROUP
    assert kk == 2 * LANES and S5_OCT * S5_GROUP == kk
    noct = e // kk
    npair = S5_OCT // 2
    ntile = nc // tc
    x4 = uz.reshape(ntile, t, tc, uz.shape[1])
    blk = (ntile, t, tc, kk)
    vmem = (2 * (2 * t * nc * kk * 2 + 3 * S5_OCT * kk * kk * 2) + S5_OCT * nc * kk * (2 + 4)
            + 4 * nc * npair * LANES * 4)
    out = pl.pallas_call(
        _ssm_kernel,
        grid=(noct,),
        in_specs=[
            pl.BlockSpec(blk, lambda i: (0, 0, 0, i)),
            pl.BlockSpec((S5_OCT, kk, kk), lambda i: (i, 0, 0)),
            pl.BlockSpec((S5_OCT, kk, kk), lambda i: (i, 0, 0)),
            pl.BlockSpec((S5_OCT, kk, kk), lambda i: (i, 0, 0)),
            pl.BlockSpec((None, npair, LANES), lambda i: (i, 0, 0)),
            pl.BlockSpec((None, npair, LANES), lambda i: (i, 0, 0)),
            pl.BlockSpec((1, kk), lambda i: (0, i)),
        ],
        out_specs=pl.BlockSpec(blk, lambda i: (0, 0, 0, i)),
        out_shape=jax.ShapeDtypeStruct((ntile, t, tc, e), BF16),
        scratch_shapes=[pltpu.VMEM((S5_OCT, nc, kk), BF16), pltpu.VMEM((S5_OCT, nc, kk), F32)]
        + [pltpu.VMEM((nc * npair, LANES), F32) for _ in range(4)],
        compiler_params=_cparams(("arbitrary",), vmem),
        name="s5_ssm",
    )(x4, wt, pm, qm, lam_r, lam_i, d_skip)
    return out.reshape(t * nc, e)


def _cmul(ar, ai, br, bi):
    return ar * br - ai * bi, ar * bi + ai * br


def _bf16_terms(x, n):
    terms = []
    for _ in range(n):
        p = x.astype(BF16)
        terms.append(p)
        x = x - p.astype(F32)
    return terms


def _zoh(a_re, a_im, dt):
    mag = jnp.exp(a_re * dt)
    return mag * jnp.cos(a_im * dt), mag * jnp.sin(a_im * dt)


def _s5_ops_kernel(are_ref, aim_ref, ldt_ref, bre_ref, bim_ref, btre_ref, btim_ref, ctre_ref, ctim_ref,
                   arp_ref, aip_ref, ldtp_ref, w_ref, pt_ref, q_ref, lamr_ref, lami_ref, rows_scr):
    t, gi = S5_T, S5_GROUP
    ns = are_ref.shape[1]
    kk = t * gi

    pr, pi = _zoh(arp_ref[...], aip_ref[...], jnp.exp(ldtp_ref[...]))
    for _ in range(t.bit_length() - 1):
        pr, pi = _cmul(pr, pi, pr, pi)
    lamr_ref[...] = pr
    lami_ref[...] = pi

    ar = are_ref[...]
    ai = aim_ref[...]
    l1r, l1i = _zoh(ar, ai, jnp.exp(ldt_ref[...]))
    nr = l1r - 1.0
    den = ar * ar + ai * ai
    quantities = [l1r, l1i]
    for _ in range(3):
        quantities += list(_cmul(quantities[-2], quantities[-1], quantities[-2], quantities[-1]))
    quantities += [(nr * ar + l1i * ai) / den, (l1i * ar - nr * ai) / den]
    for k, val in enumerate(quantities):
        rows_scr[k] = val

    eye = lax.broadcasted_iota(jnp.int32, (ns, ns), 0) == lax.broadcasted_iota(jnp.int32, (ns, ns), 1)
    lane = lax.broadcasted_iota(jnp.int32, (ns, kk), 1)
    tau = lax.shift_right_logical(lane, 4)
    bits = [(lax.shift_right_logical(tau, b) & 1) == 1 for b in range(4)]
    expand = (lax.broadcasted_iota(jnp.int32, (gi, kk), 0)
              == (lax.broadcasted_iota(jnp.int32, (gi, kk), 1) & (gi - 1))).astype(BF16)
    lane_w = lax.broadcasted_iota(jnp.int32, (gi, LANES), 1)

    def column(k, g):
        row = rows_scr[k, pl.ds(g, 1), :]
        return jnp.sum(jnp.where(eye, row, 0.0), axis=1, keepdims=True)

    def lane_powers(cols):
        pr = pi = None
        for b in range(4):
            fr = jnp.where(bits[b], cols[2 * b], 1.0)
            fi = jnp.where(bits[b], cols[2 * b + 1], 0.0)
            pr, pi = (fr, fi) if pr is None else _cmul(pr, pi, fr, fi)
        return pr, pi

    def dot3(a, b):
        (a1, a2), (b1, b2) = _bf16_terms(a, 2), _bf16_terms(b, 2)
        return (jnp.dot(a1, b1, preferred_element_type=F32) + jnp.dot(a1, b2, preferred_element_type=F32)
                + jnp.dot(a2, b1, preferred_element_type=F32))

    def tile_channels(x):
        return sum(jnp.dot(p, expand, preferred_element_type=F32) for p in _bf16_terms(x, 3))

    def stage_scalars(g):
        cols = [column(k, g) for k in range(10)]
        cr, ci = cols[8], cols[9]
        bbr = cr * bre_ref[g] - ci * bim_ref[g]
        bbi = cr * bim_ref[g] + ci * bre_ref[g]
        crow = rows_scr[8, pl.ds(g, 1), :]
        cirow = rows_scr[9, pl.ds(g, 1), :]
        bbtr = crow * btre_ref[g] - cirow * btim_ref[g]
        bbti = crow * btim_ref[g] + cirow * btre_ref[g]
        return cols, bbr, bbi, lane_powers(cols), bbtr, bbti

    def stage_tiles(g, st):
        _, bbr, bbi = st[:3]
        return (tile_channels(ctre_ref[g]), tile_channels(ctim_ref[g]),
                tile_channels(bbr), tile_channels(bbi))

    def stage_kernel(st, tiles):
        (pwr, pwi), bbt_r, bbt_i = st[3:]
        clr, cli = _cmul(tiles[0], tiles[1], pwr, pwi)
        return clr, cli, dot3(bbt_r, clr) - dot3(bbt_i, cli)

    def stage_store(g, st, tiles, kern):
        cols, _, _, (pwr, pwi) = st[:4]
        _, _, bbtr, bbti = tiles
        clr, cli, kt = kern
        odd = g % 2
        gm = g % 8

        def rot_halves(v):
            return jnp.concatenate([_rot_blocks(v[:, :LANES], gm), _rot_blocks(v[:, LANES:], gm)], axis=1)

        k0, k1 = kt[:, :LANES], kt[:, LANES:]
        for s in range(t):
            sh = (gi * s) % LANES
            r0 = pltpu.roll(k0, sh, 1) if sh else k0
            if gi * s < LANES:
                r1 = pltpu.roll(k1, sh, 1) if sh else k1
                lo = jnp.where(lane_w >= sh, r0, 0.0)
                hi = jnp.where(lane_w >= sh, r1, r0)
            else:
                lo = jnp.zeros_like(k0)
                hi = jnp.where(lane_w >= sh, r0, 0.0)
            sa = t - 1 - s
            row0 = gi * (8 * (sa // 8) + (sa + gm) % 8)
            w_ref[g, row0:row0 + gi, :] = rot_halves(jnp.concatenate([lo, hi], axis=1)).astype(w_ref.dtype)
        ptr, pti = _cmul(pwr, pwi, bbtr, bbti)
        qr, qi = _cmul(clr, cli, cols[0], cols[1])
        zero = jnp.zeros((ns, kk), pt_ref.dtype)
        for ref, re, im in ((pt_ref, ptr, pti), (q_ref, qr, -qi)):
            ref[g, odd * ns:(odd + 1) * ns, :] = rot_halves(re).astype(ref.dtype)
            ref[g, (1 - odd) * ns:(2 - odd) * ns, :] = zero
            ref[g, (2 + odd) * ns:(3 + odd) * ns, :] = rot_halves(im).astype(ref.dtype)
            ref[g, (3 - odd) * ns:(4 - odd) * ns, :] = zero

    for g0 in range(0, S5_OCT, OPS_WAVE):
        wave = range(g0, g0 + OPS_WAVE)
        sts = [stage_scalars(g) for g in wave]
        tiles = [stage_tiles(g, st) for g, st in zip(wave, sts)]
        kerns = [stage_kernel(st, tl) for st, tl in zip(sts, tiles)]
        for g, st, tl, kn in zip(wave, sts, tiles, kerns):
            stage_store(g, st, tl, kn)


def _s5_operators(a_re, a_im, log_dt, b_re, b_im, c_re, c_im):
    g, ns = a_re.shape
    t, gi = S5_T, S5_GROUP
    kk = t * gi
    assert t == 16 and 4 * ns == kk and 2 * ns == LANES
    noct = g // S5_OCT
    npair = S5_OCT // 2
    a_re_p = a_re.reshape(g // 2, 2 * ns)
    a_im_p = a_im.reshape(g // 2, 2 * ns)
    ldt_p = jnp.broadcast_to(log_dt[:, None], (g, ns)).reshape(g // 2, 2 * ns)
    oct3 = lambda i: (i, 0, 0)
    ops_shape = jax.ShapeDtypeStruct((g, kk, kk), BF16)
    lam_shape = jax.ShapeDtypeStruct((noct, npair, LANES), F32)
    return pl.pallas_call(
        _s5_ops_kernel,
        grid=(noct,),
        in_specs=[
            pl.BlockSpec((S5_OCT, ns), lambda i: (i, 0)),
            pl.BlockSpec((S5_OCT, ns), lambda i: (i, 0)),
            pl.BlockSpec((S5_OCT, 1), lambda i: (i, 0)),
            pl.BlockSpec((S5_OCT, ns, gi), oct3),
            pl.BlockSpec((S5_OCT, ns, gi), oct3),
            pl.BlockSpec((S5_OCT, gi, ns), oct3),
            pl.BlockSpec((S5_OCT, gi, ns), oct3),
            pl.BlockSpec((S5_OCT, ns, gi), oct3),
            pl.BlockSpec((S5_OCT, ns, gi), oct3),
            pl.BlockSpec((npair, LANES), lambda i: (i, 0)),
            pl.BlockSpec((npair, LANES), lambda i: (i, 0)),
            pl.BlockSpec((npair, LANES), lambda i: (i, 0)),
        ],
        out_specs=[pl.BlockSpec((S5_OCT, kk, kk), oct3)] * 3 + [pl.BlockSpec((None, npair, LANES), oct3)] * 2,
        out_shape=[ops_shape] * 3 + [lam_shape] * 2,
        scratch_shapes=[pltpu.VMEM((10, S5_OCT, ns), F32)],
        compiler_params=_cparams(("arbitrary",), 2 * 3 * S5_OCT * kk * kk * 2 + 4 * S5_OCT * ns * LANES * 4 * 2),
        name="s5_ops",
    )(a_re, a_im, log_dt.reshape(g, 1), b_re, b_im, jnp.swapaxes(b_re, 1, 2), jnp.swapaxes(b_im, 1, 2),
      jnp.swapaxes(c_re, 1, 2), jnp.swapaxes(c_im, 1, 2), a_re_p, a_im_p, ldt_p)


def _glu_kernel(ya_ref, w_ref, b_ref, yc_ref, z_ref, o_ref):
    w = w_ref[...].astype(BF16)
    rc = min(GLU_ROW_CHUNK, ya_ref.shape[0])
    for r0 in range(0, ya_ref.shape[0], rc):
        rows = slice(r0, r0 + rc)
        acc = jnp.dot(ya_ref[rows, :], w, preferred_element_type=F32) + b_ref[...]
        o_ref[rows, :] = (yc_ref[rows, :].astype(F32) * _sigmoid(acc)
                          * _silu(z_ref[rows, :].astype(F32))).astype(o_ref.dtype)


def _glu(yact, w_glu, layer, b_glu, uz, tm=1024, tn=512):
    l, e = yact.shape
    tm = min(tm, l)
    zoff = e // tn
    vmem = 2 * (tm * e * 2 + e * tn * 4 + 3 * tm * tn * 2)
    return pl.pallas_call(
        _glu_kernel,
        grid=(l // tm, e // tn),
        in_specs=[pl.BlockSpec((tm, e), lambda i, j: (i, 0)),
                  pl.BlockSpec((None, e, tn), lambda i, j: (layer, 0, j)),
                  pl.BlockSpec((1, tn), lambda i, j: (0, j)),
                  pl.BlockSpec((tm, tn), lambda i, j: (i, j)),
                  pl.BlockSpec((tm, tn), lambda i, j: (i, zoff + j))],
        out_specs=pl.BlockSpec((tm, tn), lambda i, j: (i, j)),
        out_shape=jax.ShapeDtypeStruct((l, e), BF16),
        compiler_params=_cparams(("arbitrary", "arbitrary"), vmem),
        name="s5_glu",
    )(yact, w_glu, b_glu, yact, uz)


def _gla_layer(x, scale, shift, gate, ln_g, ln_b, alpha, w_in, layer, gate_w2, gate_b, norm_g, w_out):
    d = x.shape[1]
    e = w_out.shape[1]
    qk = gate_w2.shape[1]
    wg = jnp.pad(w_in[layer, :, 2 * qk + 2 * e:], ((0, 0), (0, LANES - GLA_GATE_RANK)))
    w2p = jnp.pad(gate_w2, ((0, LANES - GLA_GATE_RANK), (0, 0))).astype(BF16)
    proj, glr = _in_proj_gate(x, scale, shift, jnp.swapaxes(w_in, 1, 2), layer, wg)
    y = _gla_core(proj, glr, w2p, gate_b.reshape(1, qk), norm_g.reshape(1, e), e, qk)
    return _out_ln(y, w_out, layer, x, gate, ln_g.reshape(1, d), ln_b.reshape(1, d), alpha)


def _s5_layer(x, scale, shift, gate, ln_g, ln_b, alpha, w_in, layer, a_re, a_im, log_dt, b_re, b_im, c_re, c_im,
              d_skip, w_glu, b_glu, w_out):
    l, d = x.shape
    e = w_out.shape[1]
    t = S5_T
    tc = min(S5_TILE_BLOCKS, l // t)
    uz = _in_proj_perm(x, t, tc, scale, shift, w_in, layer)
    wt, pm, qm, lam_r, lam_i = _s5_operators(a_re, a_im, log_dt, b_re, b_im, c_re, c_im)
    yact = _ssm_core(uz, tc, wt, pm, qm, lam_r, lam_i, d_skip.reshape(1, e), e)
    yglu = _glu(yact, w_glu, layer, b_glu.reshape(1, e), uz)
    return _out_ln(yglu, w_out, layer, x, gate, ln_g.reshape(1, d), ln_b.reshape(1, d), alpha,
                   step_major=(t, tc))


def kernel(x, c, ln_g, ln_b, ada_w, ada_b, gla_w_in, gla_gate_w2, gla_gate_b, gla_norm_g, gla_w_out,
           s5_w_in, s5_a_re, s5_a_im, s5_log_dt, s5_b_re, s5_b_im, s5_c_re, s5_c_im, s5_d,
           s5_w_glu, s5_b_glu, s5_w_out):
    bsz, l, d = x.shape
    assert bsz == 1, "batch 1 only"
    depth = ln_g.shape[0]
    alpha = (2 * depth) ** 0.25
    mod = _adaln(c, ada_w, ada_b)
    h = x.reshape(l, d)
    gla_w_out_b = gla_w_out.astype(BF16)
    s5_w_out_b = s5_w_out.astype(BF16)
    for i in range(depth):
        shift, scale, gate = mod[i, :, :d], mod[i, :, d:2 * d], mod[i, :, 2 * d:]
        j = i // 2
        if i % 2 == 0:
            h = _gla_layer(h, scale, shift, gate, ln_g[i], ln_b[i], alpha, gla_w_in, j, gla_gate_w2[j],
                           gla_gate_b[j], gla_norm_g[j], gla_w_out_b)
        else:
            h = _s5_layer(h, scale, shift, gate, ln_g[i], ln_b[i], alpha, s5_w_in, j, s5_a_re[j], s5_a_im[j],
                          s5_log_dt[j], s5_b_re[j], s5_b_im[j], s5_c_re[j], s5_c_im[j], s5_d[j],
                          s5_w_glu, s5_b_glu[j], s5_w_out_b)
    return h.reshape(bsz, l, d)
```

```python
import functools
import math

import jax
import jax.numpy as jnp
from jax import lax
from jax.experimental import pallas as pl
from jax.experimental.pallas import tpu as pltpu

F32 = jnp.float32
BF16 = jnp.bfloat16
U32 = jnp.uint32

CHUNK = 64
GLA_HEADS = 4
GLA_GATE_RANK = 16
GLA_TAU = 16.0
S5_GROUP = 16
S5_STATE = 64
LN_EPS = 1e-5
RMS_EPS = 1e-6

S5_T = 16
S5_OCT = 16
GLA_STEP_CHUNKS = 4
S5_TILE_BLOCKS = 64
OUT_ROW_CHUNK = 128
OPS_WAVE = 8
LANES = 128
V7X_SCOPED_VMEM_CAP = 60000 * 1024


def _cparams(semantics, vmem_bytes):
    limit = min(int(vmem_bytes) + (6 << 20), V7X_SCOPED_VMEM_CAP)
    return pltpu.CompilerParams(dimension_semantics=semantics, vmem_limit_bytes=limit)


def _sigmoid(x):
    return 1.0 / (1.0 + jnp.exp(-x))


def _silu(x):
    return x * _sigmoid(x)


def _gelu_tanh(y):
    cdf = 0.5 * (1.0 + jnp.tanh(math.sqrt(2.0 / math.pi) * (y + 0.044715 * (y * y * y))))
    return y * cdf


def _adaln_kernel(c_ref, w_ref, b_ref, o_ref):
    c = c_ref[...]
    o_ref[...] = jnp.sum(_silu(c) * w_ref[...], axis=0, keepdims=True) + b_ref[...]


def _adaln(c, ada_w, ada_b, tn=1024):
    depth, d, n3 = ada_w.shape
    assert c.shape == (1, d), "batch 1 only"
    c_col = c.reshape(d, 1)
    out = pl.pallas_call(
        _adaln_kernel,
        grid=(depth, n3 // tn),
        in_specs=[
            pl.BlockSpec((d, 1), lambda l, j: (0, 0)),
            pl.BlockSpec((None, d, tn), lambda l, j: (l, 0, j)),
            pl.BlockSpec((None, 1, tn), lambda l, j: (l, 0, j)),
        ],
        out_specs=pl.BlockSpec((None, 1, tn), lambda l, j: (l, 0, j)),
        out_shape=jax.ShapeDtypeStruct((depth, 1, n3), F32),
        compiler_params=_cparams(("arbitrary", "arbitrary"), 2 * d * tn * 4 + d * LANES * 4),
        name="adaln_mod",
    )(c_col, ada_w, ada_b.reshape(depth, 1, n3))
    return out


def _modulate(x_ref, sc_ref, sh_ref):
    return (x_ref[...] * (1.0 + sc_ref[...]) + sh_ref[...]).astype(BF16)


def _proj_gate_kernel(x_ref, sc_ref, sh_ref, w_ref, wg_ref, o_ref, g_ref, u_scr):
    @pl.when(pl.program_id(1) == 0)
    def _():
        u = _modulate(x_ref, sc_ref, sh_ref)
        u_scr[...] = u
        g_ref[...] = jnp.dot(u, wg_ref[...].astype(BF16), preferred_element_type=F32)

    o_ref[...] = lax.dot_general(u_scr[...], w_ref[...].astype(BF16), (((1,), (1,)), ((), ())),
                                 preferred_element_type=F32).astype(o_ref.dtype)


def _sublane_transpose8(arrs):
    sub = lax.broadcasted_iota(jnp.int32, arrs[0].shape, 1)
    for b in range(3):
        dist = 1 << b
        sel = (sub & dist) != 0
        new = list(arrs)
        for j in range(8):
            if not j & dist:
                lo, hi = arrs[j], arrs[j + dist]
                new[j] = jnp.where(sel, pltpu.roll(hi, dist, 1), lo)
                new[j + dist] = jnp.where(sel, hi, pltpu.roll(lo, 8 - dist, 1))
        arrs = new
    return arrs


STEP_VIEW_LANES = 512


def _load_block_residues(x_ref, ls):
    tc = x_ref.shape[0]
    return [jnp.stack([x_ref[8 * cg + j, :, ls] for cg in range(tc // 8)]) for j in range(8)]


def _proj_perm_kernel(xa_ref, xb_ref, sc_ref, sh_ref, w_ref, o_ref, u_scr):
    tc, hs, d = xa_ref.shape
    assert hs == 8 and tc % 8 == 0

    @pl.when(pl.program_id(1) == 0)
    def _():
        for half, x_ref in enumerate((xa_ref, xb_ref)):
            for l0 in range(0, d, STEP_VIEW_LANES):
                ls = slice(l0, l0 + STEP_VIEW_LANES)
                rows = _sublane_transpose8(_load_block_residues(x_ref, ls))
                for sl in range(hs):
                    s = hs * half + sl
                    u = rows[sl].reshape(tc, STEP_VIEW_LANES) * (1.0 + sc_ref[:, ls]) + sh_ref[:, ls]
                    u_scr[s * tc:(s + 1) * tc, ls] = u.astype(BF16)

    o_ref[...] = jnp.dot(u_scr[...], w_ref[...].astype(BF16), preferred_element_type=F32).astype(o_ref.dtype)


def _in_proj_gate(x, scale, shift, wt, layer, wg, tm=1024, tn=1024):
    l, d = x.shape
    n = (wt.shape[1] // tn) * tn
    ng = wg.shape[1]
    tm = min(tm, l)
    vmem = 2 * tm * d * 4 + 2 * d * tn * 4 + 2 * tm * tn * 2 + tm * d * 2 + 2 * d * ng * 4 + 2 * tm * ng * 4
    return pl.pallas_call(
        _proj_gate_kernel,
        grid=(l // tm, n // tn),
        in_specs=[
            pl.BlockSpec((tm, d), lambda i, j: (i, 0)),
            pl.BlockSpec((1, d), lambda i, j: (0, 0)),
            pl.BlockSpec((1, d), lambda i, j: (0, 0)),
            pl.BlockSpec((None, tn, d), lambda i, j: (layer, j, 0)),
            pl.BlockSpec((d, ng), lambda i, j: (0, 0)),
        ],
        out_specs=[pl.BlockSpec((tm, tn), lambda i, j: (i, j)),
                   pl.BlockSpec((tm, ng), lambda i, j: (i, 0))],
        out_shape=[jax.ShapeDtypeStruct((l, n), BF16), jax.ShapeDtypeStruct((l, ng), F32)],
        scratch_shapes=[pltpu.VMEM((tm, d), BF16)],
        compiler_params=_cparams(("arbitrary", "arbitrary"), vmem),
        name="in_proj_gate",
    )(x, scale, shift, wt, wg)


def _step_major_view(x, t, tc):
    l, d = x.shape
    assert l % (t * tc) == 0 and (t // 2) % 8 == 0
    return x.reshape(l // t, 2, t // 2, d)


def _in_proj_perm(x, t, tc, scale, shift, w, layer, tn=1024):
    l, d = x.shape
    x4 = _step_major_view(x, t, tc)
    n = w.shape[2]
    tm = t * tc
    vmem = 2 * tm * d * 4 + 2 * d * tn * 4 + 2 * tm * tn * 2 + tm * d * 2
    return pl.pallas_call(
        _proj_perm_kernel,
        grid=(l // tm, n // tn),
        in_specs=[
            pl.BlockSpec((tc, None, t // 2, d), lambda i, j: (i, 0, 0, 0)),
            pl.BlockSpec((tc, None, t // 2, d), lambda i, j: (i, 1, 0, 0)),
            pl.BlockSpec((1, d), lambda i, j: (0, 0)),
            pl.BlockSpec((1, d), lambda i, j: (0, 0)),
            pl.BlockSpec((None, d, tn), lambda i, j: (layer, 0, j)),
        ],
        out_specs=pl.BlockSpec((tm, tn), lambda i, j: (i, j)),
        out_shape=jax.ShapeDtypeStruct((l, n), BF16),
        scratch_shapes=[pltpu.VMEM((tm, d), BF16)],
        compiler_params=_cparams(("arbitrary", "arbitrary"), vmem),
        name="in_proj_perm",
    )(x4, x4, scale, shift, w)


def _gla_core_kernel(*refs, dk, dv):
    nh = GLA_HEADS
    q_ref, k_ref = refs[0], refs[1]
    v_refs = refs[2:2 + nh]
    z_refs = refs[2 + nh:2 + 2 * nh]
    g_ref, w2_ref, gb_ref, ng_ref, o_ref, st_ref, kd_scr, ko_scr, qd_scr = refs[2 + 2 * nh:]

    @pl.when(pl.program_id(0) == 0)
    def _():
        st_ref[...] = jnp.zeros_like(st_ref)

    c = CHUNK
    r = q_ref.shape[0]
    sb = r // c
    nt = (((1,), (1,)), ((), ()))
    pre = jnp.dot(g_ref[...].astype(BF16), w2_ref[...], preferred_element_type=F32) + gb_ref[...]
    la = (jnp.minimum(pre, 0.0) - jnp.log(1.0 + jnp.exp(-jnp.abs(pre)))) * (1.0 / GLA_TAU)
    row = lax.broadcasted_iota(jnp.int32, (r, r), 0)
    col = lax.broadcasted_iota(jnp.int32, (r, r), 1)
    tri = (col <= row).astype(BF16)
    la_hi = la.astype(BF16)
    la_lo = (la - la_hi.astype(F32)).astype(BF16)
    gc = (jnp.dot(tri, la_hi, preferred_element_type=F32) + jnp.dot(tri, la_lo, preferred_element_type=F32))
    ends = [gc[c * (j + 1) - 1:c * (j + 1), :] for j in range(sb)]
    e_rows = jnp.concatenate([jnp.broadcast_to(ends[j], (c, gc.shape[1])) for j in range(sb)], axis=0)
    kf = k_ref[...].astype(F32) * jnp.exp(e_rows - gc)
    kd_scr[...] = kf.astype(BF16)
    ko_scr[...] = (kf * jnp.exp(ends[-1] - e_rows)).astype(BF16)
    qd_scr[...] = (q_ref[...].astype(F32) * jnp.exp(e_rows)).astype(BF16)
    dec = jnp.exp(ends[-1])
    for h in range(nh):
        ks = slice(h * dk, (h + 1) * dk)
        vs = slice(h * dv, (h + 1) * dv)
        st_in = st_ref[h]
        base = lax.dot_general(qd_scr[:, ks], st_in.astype(BF16), nt, preferred_element_type=F32)
        ams = []
        for m in range(sb):
            qs = jnp.concatenate(
                [(q_ref[c * j:c * (j + 1), ks].astype(F32) * jnp.exp(ends[j][:, ks] - ends[m][:, ks])).astype(BF16)
                 if j > m else q_ref[c * j:c * (j + 1), ks] for j in range(m, sb)], axis=0)
            kpad = jnp.concatenate(
                ([jnp.zeros((c * m, dk), BF16)] if m else []) + [kd_scr[c * m:c * (m + 1), ks]]
                + ([jnp.zeros((c * (sb - 1 - m), dk), BF16)] if m < sb - 1 else []), axis=0)
            ams.append(lax.dot_general(qs, kpad, nt, preferred_element_type=F32))
        a_rows = []
        for j in range(sb):
            aj = ams[0][c * j:c * (j + 1)]
            for m in range(1, j + 1):
                aj = aj + ams[m][c * (j - m):c * (j - m + 1)]
            a_rows.append(aj)
        a_full = jnp.concatenate(a_rows, axis=0).astype(BF16)
        o = base + jnp.dot(a_full, v_refs[h][...], preferred_element_type=F32)
        upd = lax.dot_general(v_refs[h][...], ko_scr[:, ks], (((0,), (0,)), ((), ())),
                              preferred_element_type=F32)
        st_ref[h] = dec[:, ks] * st_in + upd
        o = o * (dk ** -0.5)
        o = o * lax.rsqrt(jnp.mean(o * o, axis=-1, keepdims=True) + RMS_EPS)
        y = o * ng_ref[:, vs] * _silu(z_refs[h][...].astype(F32))
        o_ref[:, vs] = y.astype(o_ref.dtype)


def _gla_core(proj, glr, w2p, gate_b, norm_g, e, qk):
    l = proj.shape[0]
    nh = GLA_HEADS
    dk, dv = qk // nh, e // nh
    assert (2 * qk) % dv == 0
    v0 = 2 * qk // dv
    ng = glr.shape[1]
    c = min(GLA_STEP_CHUNKS * CHUNK, l)
    assert c % CHUNK == 0 and l % c == 0
    kern = functools.partial(_gla_core_kernel, dk=dk, dv=dv)
    vmem = (2 * (2 * c * e * 2 + 2 * c * qk * 2 + c * ng * 4 + ng * qk * 2 + c * e * 2) + nh * dv * dk * 4
            + 12 * c * qk * 4 + 4 * c * dv * 4)
    head_specs = [pl.BlockSpec((c, dv), functools.partial(lambda n, b: (n, b), b=v0 + h)) for h in range(2 * nh)]
    return pl.pallas_call(
        kern,
        grid=(l // c,),
        in_specs=[
            pl.BlockSpec((c, qk), lambda n: (n, 0)),
            pl.BlockSpec((c, qk), lambda n: (n, 1)),
            *head_specs,
            pl.BlockSpec((c, ng), lambda n: (n, 0)),
            pl.BlockSpec((ng, qk), lambda n: (0, 0)),
            pl.BlockSpec((1, qk), lambda n: (0, 0)),
            pl.BlockSpec((1, e), lambda n: (0, 0)),
        ],
        out_specs=pl.BlockSpec((c, e), lambda n: (n, 0)),
        out_shape=jax.ShapeDtypeStruct((l, e), BF16),
        scratch_shapes=[pltpu.VMEM((nh, dv, dk), F32)] + [pltpu.VMEM((c, qk), BF16) for _ in range(3)],
        compiler_params=_cparams(("arbitrary",), vmem),
        name="gla_core",
    )(*([proj] * (2 + 2 * nh)), glr, w2p, gate_b, norm_g)


def _out_ln_kernel(y_ref, w_ref, x_ref, gate_ref, g_ref, b_ref, o_ref, *scratch, alpha):
    tm = y_ref.shape[0]
    rc = min(OUT_ROW_CHUNK, tm)

    def layer_norm(r):
        mu = jnp.mean(r, axis=-1, keepdims=True)
        cen = r - mu
        var = jnp.mean(cen * cen, axis=-1, keepdims=True)
        return cen * lax.rsqrt(var + LN_EPS) * g_ref[...] + b_ref[...]

    def chunk_out(rows, x_rows):
        h = jnp.dot(y_ref[rows, :], w_ref[...], preferred_element_type=F32)
        return layer_norm(alpha * x_rows + (1.0 + gate_ref[...]) * h)

    if len(x_ref.shape) == 2:
        for r0 in range(0, tm, rc):
            rows = slice(r0, r0 + rc)
            o_ref[rows, :] = chunk_out(rows, x_ref[rows, :])
    else:
        (r_scr,) = scratch
        tc, hs, d = x_ref.shape
        assert hs == 8 and tc % 8 == 0 and rc % tc == 0
        chunks = [slice(l0, l0 + STEP_VIEW_LANES) for l0 in range(0, d, STEP_VIEW_LANES)]
        for ls in chunks:
            xs = _sublane_transpose8(_load_block_residues(x_ref, ls))
            for sl in range(hs):
                r_scr[sl * tc:(sl + 1) * tc, ls] = xs[sl].reshape(tc, STEP_VIEW_LANES)
        for r0 in range(0, tm, rc):
            rows = slice(r0, r0 + rc)
            r_scr[rows, :] = chunk_out(rows, r_scr[rows, :])
        for ls in chunks:
            outs = _sublane_transpose8([r_scr[sl * tc:(sl + 1) * tc, ls].reshape(tc // 8, 8, STEP_VIEW_LANES)
                                        for sl in range(hs)])
            for j in range(8):
                for cg in range(tc // 8):
                    o_ref[8 * cg + j, :, ls] = outs[j][cg]


def _out_ln(y, w, layer, x, gate, ln_g, ln_b, alpha, step_major=None, tm=512):
    l, e = y.shape
    d = w.shape[2]
    kern = functools.partial(_out_ln_kernel, alpha=alpha)
    if step_major is None:
        tm = min(tm, l)
        x_in, out_shape = x, (l, d)
        x_spec = pl.BlockSpec((tm, d), lambda i: (i, 0))
        scratch = []
    else:
        t, tc = step_major
        tm = tc * t // 2
        x_in = _step_major_view(x, t, tc)
        out_shape = x_in.shape
        x_spec = pl.BlockSpec((tc, None, t // 2, d), lambda i: (i // 2, i % 2, 0, 0))
        scratch = [pltpu.VMEM((tm, d), F32)]
    vmem = 2 * tm * e * 2 + e * d * 2 + 4 * tm * d * 4 + len(scratch) * tm * d * 4 + 4 * OUT_ROW_CHUNK * d * 4
    out = pl.pallas_call(
        kern,
        grid=(l // tm,),
        in_specs=[
            pl.BlockSpec((tm, e), lambda i: (i, 0)),
            pl.BlockSpec((None, e, d), lambda i: (layer, 0, 0), pipeline_mode=pl.Buffered(1)),
            x_spec,
            pl.BlockSpec((1, d), lambda i: (0, 0)),
            pl.BlockSpec((1, d), lambda i: (0, 0)),
            pl.BlockSpec((1, d), lambda i: (0, 0)),
        ],
        out_specs=x_spec,
        out_shape=jax.ShapeDtypeStruct(out_shape, F32),
        scratch_shapes=scratch,
        compiler_params=_cparams(("arbitrary",), vmem),
        name="out_proj_ln",
    )(y, w, x_in, gate, ln_g, ln_b)
    return out.reshape(l, d)


def _rot_blocks(v, nblk):
    nblk %= 8
    return pltpu.roll(v, S5_GROUP * nblk, 1) if nblk else v


def _skew_select(cols):
    blk = lax.shift_right_logical(lax.broadcasted_iota(jnp.int32, cols[0].shape, 1), 4)
    x = [cols[(-m) % 8] for m in range(8)]
    for b in range(3):
        dist = 1 << b
        sel = (blk & dist) != 0
        x = [jnp.where(sel, x[(m - dist) % 8], x[m]) for m in range(8)]
    return x


def _ssm_kernel(x_ref, w_ref, p_ref, q_ref, lr_ref, li_ref, d_ref, o_ref, a_scr, y_scr, vre, vim, sre, sim):
    ntile, t_steps, tc, _ = x_ref.shape
    nc = ntile * tc
    npair = S5_OCT // 2

    def gather_steps(tile, carry):
        r = pl.ds(pl.multiple_of(tile * tc, tc), tc)
        for s_hi in range(2):
            for g_hi in range(2):
                cols = [_rot_blocks(pltpu.bitcast(x_ref[tile, t_steps - 1 - (8 * s_hi + m), :,
                                                        g_hi * LANES:(g_hi + 1) * LANES], U32), m) for m in range(8)]
                res = _skew_select(cols)
                for m in range(8):
                    a_scr[8 * g_hi + m, r, s_hi * LANES:(s_hi + 1) * LANES] = pltpu.bitcast(res[m], BF16)
        return carry

    lax.fori_loop(0, ntile, gather_steps, 0)
    a_bf = a_scr

    nt = (((1,), (1,)), ((), ()))
    for p in range(npair):
        v = (lax.dot_general(a_bf[2 * p], p_ref[2 * p], nt, preferred_element_type=F32)
             + lax.dot_general(a_bf[2 * p + 1], p_ref[2 * p + 1], nt, preferred_element_type=F32))
        vre[pl.ds(p, nc, stride=npair), :] = v[:, :LANES]
        vim[pl.ds(p, nc, stride=npair), :] = v[:, LANES:]

    ar = lr_ref[...]
    ai = li_ref[...]

    def step(c, carry):
        xr, xi = carry
        rows = pl.ds(pl.multiple_of(c * npair, npair), npair)
        sre[rows, :] = xr
        sim[rows, :] = xi
        nxr = ar * xr - ai * xi + vre[rows, :]
        nxi = ar * xi + ai * xr + vim[rows, :]
        return nxr, nxi

    zero = jnp.zeros((npair, LANES), F32)
    lax.fori_loop(0, nc, step, (zero, zero), unroll=8)

    for p in range(npair):
        s = jnp.concatenate([sre[pl.ds(p, nc, stride=npair), :], sim[pl.ds(p, nc, stride=npair), :]],
                            axis=1).astype(BF16)
        for g in (2 * p, 2 * p + 1):
            y_scr[g] = (jnp.dot(a_bf[g], w_ref[g], preferred_element_type=F32)
                        + jnp.dot(s, q_ref[g], preferred_element_type=F32))

    rbo = min(32, tc)
    per_tile = tc // rbo

    def scatter_steps(it, carry):
        r = pl.ds(pl.multiple_of(it * rbo, rbo), rbo)
        tile = it // per_tile
        rt = pl.ds(pl.multiple_of((it % per_tile) * rbo, rbo), rbo)
        for s_hi in range(2):
            for g_hi in range(2):
                cs = slice(g_hi * LANES, (g_hi + 1) * LANES)
                cols = [y_scr[8 * g_hi + m, r, s_hi * LANES:(s_hi + 1) * LANES] for m in range(8)]
                res = _skew_select(cols)
                dsk = d_ref[:, cs]
                for m in range(8):
                    s = 8 * s_hi + m
                    y = _rot_blocks(res[m], -m) + dsk * x_ref[tile, s, rt, cs].astype(F32)
                    o_ref[tile, s, rt, cs] = _gelu_tanh(y).astype(o_ref.dtype)
        return carry

    lax.fori_loop(0, nc // rbo, scatter_steps, 0)


def _ssm_core(uz, tc, wt, pm, qm, lam_r, lam_i, d_skip, e):
    t = S5_T
    nc = uz.shape[0] // t
    kk = t * S5_GROUP
    assert kk == 2 * LANES and S5_OCT * S5_GROUP == kk
    noct = e // kk
    npair = S5_OCT // 2
    ntile = nc // tc
    x4 = uz.reshape(ntile, t, tc, uz.shape[1])
    blk = (ntile, t, tc, kk)
    vmem = (2 * (2 * t * nc * kk * 2 + 3 * S5_OCT * kk * kk * 2) + S5_OCT * nc * kk * (2 + 4)
            + 4 * nc * npair * LANES * 4)
    out = pl.pallas_call(
        _ssm_kernel,
        grid=(noct,),
        in_specs=[
            pl.BlockSpec(blk, lambda i: (0, 0, 0, i)),
            pl.BlockSpec((S5_OCT, kk, kk), lambda i: (i, 0, 0)),
            pl.BlockSpec((S5_OCT, kk, kk), lambda i: (i, 0, 0)),
            pl.BlockSpec((S5_OCT, kk, kk), lambda i: (i, 0, 0)),
            pl.BlockSpec((None, npair, LANES), lambda i: (i, 0, 0)),
            pl.BlockSpec((None, npair, LANES), lambda i: (i, 0, 0)),
            pl.BlockSpec((1, kk), lambda i: (0, i)),
        ],
        out_specs=pl.BlockSpec(blk, lambda i: (0, 0, 0, i)),
        out_shape=jax.ShapeDtypeStruct((ntile, t, tc, e), BF16),
        scratch_shapes=[pltpu.VMEM((S5_OCT, nc, kk), BF16), pltpu.VMEM((S5_OCT, nc, kk), F32)]
        + [pltpu.VMEM((nc * npair, LANES), F32) for _ in range(4)],
        compiler_params=_cparams(("arbitrary",), vmem),
        name="s5_ssm",
    )(x4, wt, pm, qm, lam_r, lam_i, d_skip)
    return out.reshape(t * nc, e)


def _cmul(ar, ai, br, bi):
    return ar * br - ai * bi, ar * bi + ai * br


def _bf16_terms(x, n):
    terms = []
    for _ in range(n):
        p = x.astype(BF16)
        terms.append(p)
        x = x - p.astype(F32)
    return terms


def _zoh(a_re, a_im, dt):
    mag = jnp.exp(a_re * dt)
    return mag * jnp.cos(a_im * dt), mag * jnp.sin(a_im * dt)


def _s5_ops_kernel(are_ref, aim_ref, ldt_ref, bre_ref, bim_ref, btre_ref, btim_ref, ctre_ref, ctim_ref,
                   arp_ref, aip_ref, ldtp_ref, w_ref, pt_ref, q_ref, lamr_ref, lami_ref, rows_scr):
    t, gi = S5_T, S5_GROUP
    ns = are_ref.shape[1]
    kk = t * gi

    pr, pi = _zoh(arp_ref[...], aip_ref[...], jnp.exp(ldtp_ref[...]))
    for _ in range(t.bit_length() - 1):
        pr, pi = _cmul(pr, pi, pr, pi)
    lamr_ref[...] = pr
    lami_ref[...] = pi

    ar = are_ref[...]
    ai = aim_ref[...]
    l1r, l1i = _zoh(ar, ai, jnp.exp(ldt_ref[...]))
    nr = l1r - 1.0
    den = ar * ar + ai * ai
    quantities = [l1r, l1i]
    for _ in range(3):
        quantities += list(_cmul(quantities[-2], quantities[-1], quantities[-2], quantities[-1]))
    quantities += [(nr * ar + l1i * ai) / den, (l1i * ar - nr * ai) / den]
    for k, val in enumerate(quantities):
        rows_scr[k] = val

    eye = lax.broadcasted_iota(jnp.int32, (ns, ns), 0) == lax.broadcasted_iota(jnp.int32, (ns, ns), 1)
    lane = lax.broadcasted_iota(jnp.int32, (ns, kk), 1)
    tau = lax.shift_right_logical(lane, 4)
    bits = [(lax.shift_right_logical(tau, b) & 1) == 1 for b in range(4)]
    expand = (lax.broadcasted_iota(jnp.int32, (gi, kk), 0)
              == (lax.broadcasted_iota(jnp.int32, (gi, kk), 1) & (gi - 1))).astype(BF16)
    lane_w = lax.broadcasted_iota(jnp.int32, (gi, LANES), 1)

    def column(k, g):
        row = rows_scr[k, pl.ds(g, 1), :]
        return jnp.sum(jnp.where(eye, row, 0.0), axis=1, keepdims=True)

    def lane_powers(cols):
        pr = pi = None
        for b in range(4):
            fr = jnp.where(bits[b], cols[2 * b], 1.0)
            fi = jnp.where(bits[b], cols[2 * b + 1], 0.0)
            pr, pi = (fr, fi) if pr is None else _cmul(pr, pi, fr, fi)
        return pr, pi

    def dot3(a, b):
        (a1, a2), (b1, b2) = _bf16_terms(a, 2), _bf16_terms(b, 2)
        return (jnp.dot(a1, b1, preferred_element_type=F32) + jnp.dot(a1, b2, preferred_element_type=F32)
                + jnp.dot(a2, b1, preferred_element_type=F32))

    def tile_channels(x):
        return sum(jnp.dot(p, expand, preferred_element_type=F32) for p in _bf16_terms(x, 3))

    def stage_scalars(g):
        cols = [column(k, g) for k in range(10)]
        cr, ci = cols[8], cols[9]
        bbr = cr * bre_ref[g] - ci * bim_ref[g]
        bbi = cr * bim_ref[g] + ci * bre_ref[g]
        crow = rows_scr[8, pl.ds(g, 1), :]
        cirow = rows_scr[9, pl.ds(g, 1), :]
        bbtr = crow * btre_ref[g] - cirow * btim_ref[g]
        bbti = crow * btim_ref[g] + cirow * btre_ref[g]
        return cols, bbr, bbi, lane_powers(cols), bbtr, bbti

    def stage_tiles(g, st):
        _, bbr, bbi = st[:3]
        return (tile_channels(ctre_ref[g]), tile_channels(ctim_ref[g]),
                tile_channels(bbr), tile_channels(bbi))

    def stage_kernel(st, tiles):
        (pwr, pwi), bbt_r, bbt_i = st[3:]
        clr, cli = _cmul(tiles[0], tiles[1], pwr, pwi)
        return clr, cli, dot3(bbt_r, clr) - dot3(bbt_i, cli)

    def stage_store(g, st, tiles, kern):
        cols, _, _, (pwr, pwi) = st[:4]
        _, _, bbtr, bbti = tiles
        clr, cli, kt = kern
        odd = g % 2
        gm = g % 8

        def rot_halves(v):
            return jnp.concatenate([_rot_blocks(v[:, :LANES], gm), _rot_blocks(v[:, LANES:], gm)], axis=1)

        k0, k1 = kt[:, :LANES], kt[:, LANES:]
        for s in range(t):
            sh = (gi * s) % LANES
            r0 = pltpu.roll(k0, sh, 1) if sh else k0
            if gi * s < LANES:
                r1 = pltpu.roll(k1, sh, 1) if sh else k1
                lo = jnp.where(lane_w >= sh, r0, 0.0)
                hi = jnp.where(lane_w >= sh, r1, r0)
            else:
                lo = jnp.zeros_like(k0)
                hi = jnp.where(lane_w >= sh, r0, 0.0)
            sa = t - 1 - s
            row0 = gi * (8 * (sa // 8) + (sa + gm) % 8)
            w_ref[g, row0:row0 + gi, :] = rot_halves(jnp.concatenate([lo, hi], axis=1)).astype(w_ref.dtype)
        ptr, pti = _cmul(pwr, pwi, bbtr, bbti)
        qr, qi = _cmul(clr, cli, cols[0], cols[1])
        zero = jnp.zeros((ns, kk), pt_ref.dtype)
        for ref, re, im in ((pt_ref, ptr, pti), (q_ref, qr, -qi)):
            ref[g, odd * ns:(odd + 1) * ns, :] = rot_halves(re).astype(ref.dtype)
            ref[g, (1 - odd) * ns:(2 - odd) * ns, :] = zero
            ref[g, (2 + odd) * ns:(3 + odd) * ns, :] = rot_halves(im).astype(ref.dtype)
            ref[g, (3 - odd) * ns:(4 - odd) * ns, :] = zero

    for g0 in range(0, S5_OCT, OPS_WAVE):
        wave = range(g0, g0 + OPS_WAVE)
        sts = [stage_scalars(g) for g in wave]
        tiles = [stage_tiles(g, st) for g, st in zip(wave, sts)]
        kerns = [stage_kernel(st, tl) for st, tl in zip(sts, tiles)]
        for g, st, tl, kn in zip(wave, sts, tiles, kerns):
            stage_store(g, st, tl, kn)


def _s5_operators(a_re, a_im, log_dt, b_re, b_im, c_re, c_im):
    g, ns = a_re.shape
    t, gi = S5_T, S5_GROUP
    kk = t * gi
    assert t == 16 and 4 * ns == kk and 2 * ns == LANES
    noct = g // S5_OCT
    npair = S5_OCT // 2
    a_re_p = a_re.reshape(g // 2, 2 * ns)
    a_im_p = a_im.reshape(g // 2, 2 * ns)
    ldt_p = jnp.broadcast_to(log_dt[:, None], (g, ns)).reshape(g // 2, 2 * ns)
    oct3 = lambda i: (i, 0, 0)
    ops_shape = jax.ShapeDtypeStruct((g, kk, kk), BF16)
    lam_shape = jax.ShapeDtypeStruct((noct, npair, LANES), F32)
    return pl.pallas_call(
        _s5_ops_kernel,
        grid=(noct,),
        in_specs=[
            pl.BlockSpec((S5_OCT, ns), lambda i: (i, 0)),
            pl.BlockSpec((S5_OCT, ns), lambda i: (i, 0)),
            pl.BlockSpec((S5_OCT, 1), lambda i: (i, 0)),
            pl.BlockSpec((S5_OCT, ns, gi), oct3),
            pl.BlockSpec((S5_OCT, ns, gi), oct3),
            pl.BlockSpec((S5_OCT, gi, ns), oct3),
            pl.BlockSpec((S5_OCT, gi, ns), oct3),
            pl.BlockSpec((S5_OCT, ns, gi), oct3),
            pl.BlockSpec((S5_OCT, ns, gi), oct3),
            pl.BlockSpec((npair, LANES), lambda i: (i, 0)),
            pl.BlockSpec((npair, LANES), lambda i: (i, 0)),
            pl.BlockSpec((npair, LANES), lambda i: (i, 0)),
        ],
        out_specs=[pl.BlockSpec((S5_OCT, kk, kk), oct3)] * 3 + [pl.BlockSpec((None, npair, LANES), oct3)] * 2,
        out_shape=[ops_shape] * 3 + [lam_shape] * 2,
        scratch_shapes=[pltpu.VMEM((10, S5_OCT, ns), F32)],
        compiler_params=_cparams(("arbitrary",), 2 * 3 * S5_OCT * kk * kk * 2 + 4 * S5_OCT * ns * LANES * 4 * 2),
        name="s5_ops",
    )(a_re, a_im, log_dt.reshape(g, 1), b_re, b_im, jnp.swapaxes(b_re, 1, 2), jnp.swapaxes(b_im, 1, 2),
      jnp.swapaxes(c_re, 1, 2), jnp.swapaxes(c_im, 1, 2), a_re_p, a_im_p, ldt_p)


def _glu_kernel(ya_ref, w_ref, b_ref, yc_ref, z_ref, o_ref):
    acc = jnp.dot(ya_ref[...], w_ref[...].astype(BF16), preferred_element_type=F32) + b_ref[...]
    o_ref[...] = (yc_ref[...].astype(F32) * _sigmoid(acc) * _silu(z_ref[...].astype(F32))).astype(o_ref.dtype)


def _glu(yact, w_glu, layer, b_glu, uz, tm=1024, tn=512):
    l, e = yact.shape
    tm = min(tm, l)
    zoff = e // tn
    vmem = 2 * (tm * e * 2 + e * tn * 4 + 3 * tm * tn * 2)
    return pl.pallas_call(
        _glu_kernel,
        grid=(l // tm, e // tn),
        in_specs=[pl.BlockSpec((tm, e), lambda i, j: (i, 0)),
                  pl.BlockSpec((None, e, tn), lambda i, j: (layer, 0, j)),
                  pl.BlockSpec((1, tn), lambda i, j: (0, j)),
                  pl.BlockSpec((tm, tn), lambda i, j: (i, j)),
                  pl.BlockSpec((tm, tn), lambda i, j: (i, zoff + j))],
        out_specs=pl.BlockSpec((tm, tn), lambda i, j: (i, j)),
        out_shape=jax.ShapeDtypeStruct((l, e), BF16),
        compiler_params=_cparams(("arbitrary", "arbitrary"), vmem),
        name="s5_glu",
    )(yact, w_glu, b_glu, yact, uz)


def _gla_layer(x, scale, shift, gate, ln_g, ln_b, alpha, w_in, layer, gate_w2, gate_b, norm_g, w_out):
    d = x.shape[1]
    e = w_out.shape[1]
    qk = gate_w2.shape[1]
    wg = jnp.pad(w_in[layer, :, 2 * qk + 2 * e:], ((0, 0), (0, LANES - GLA_GATE_RANK)))
    w2p = jnp.pad(gate_w2, ((0, LANES - GLA_GATE_RANK), (0, 0))).astype(BF16)
    proj, glr = _in_proj_gate(x, scale, shift, jnp.swapaxes(w_in, 1, 2), layer, wg)
    y = _gla_core(proj, glr, w2p, gate_b.reshape(1, qk), norm_g.reshape(1, e), e, qk)
    return _out_ln(y, w_out, layer, x, gate, ln_g.reshape(1, d), ln_b.reshape(1, d), alpha)


def _s5_layer(x, scale, shift, gate, ln_g, ln_b, alpha, w_in, layer, a_re, a_im, log_dt, b_re, b_im, c_re, c_im,
              d_skip, w_glu, b_glu, w_out):
    l, d = x.shape
    e = w_out.shape[1]
    t = S5_T
    tc = min(S5_TILE_BLOCKS, l // t)
    uz = _in_proj_perm(x, t, tc, scale, shift, w_in, layer)
    wt, pm, qm, lam_r, lam_i = _s5_operators(a_re, a_im, log_dt, b_re, b_im, c_re, c_im)
    yact = _ssm_core(uz, tc, wt, pm, qm, lam_r, lam_i, d_skip.reshape(1, e), e)
    yglu = _glu(yact, w_glu, layer, b_glu.reshape(1, e), uz)
    return _out_ln(yglu, w_out, layer, x, gate, ln_g.reshape(1, d), ln_b.reshape(1, d), alpha,
                   step_major=(t, tc))


def kernel(x, c, ln_g, ln_b, ada_w, ada_b, gla_w_in, gla_gate_w2, gla_gate_b, gla_norm_g, gla_w_out,
           s5_w_in, s5_a_re, s5_a_im, s5_log_dt, s5_b_re, s5_b_im, s5_c_re, s5_c_im, s5_d,
           s5_w_glu, s5_b_glu, s5_w_out):
    bsz, l, d = x.shape
    assert bsz == 1, "batch 1 only"
    depth = ln_g.shape[0]
    alpha = (2 * depth) ** 0.25
    mod = _adaln(c, ada_w, ada_b)
    h = x.reshape(l, d)
    gla_w_out_b = gla_w_out.astype(BF16)
    s5_w_out_b = s5_w_out.astype(BF16)
    for i in range(depth):
        shift, scale, gate = mod[i, :, :d], mod[i, :, d:2 * d], mod[i, :, 2 * d:]
        j = i // 2
        if i % 2 == 0:
            h = _gla_layer(h, scale, shift, gate, ln_g[i], ln_b[i], alpha, gla_w_in, j, gla_gate_w2[j],
                           gla_gate_b[j], gla_norm_g[j], gla_w_out_b)
        else:
            h = _s5_layer(h, scale, shift, gate, ln_g[i], ln_b[i], alpha, s5_w_in, j, s5_a_re[j], s5_a_im[j],
                          s5_log_dt[j], s5_b_re[j], s5_b_im[j], s5_c_re[j], s5_c_im[j], s5_d[j],
                          s5_w_glu, s5_b_glu[j], s5_w_out_b)
    return h.reshape(bsz, l, d)
```

```python
import functools
import math

import jax
import jax.numpy as jnp
from jax import lax
from jax.experimental import pallas as pl
from jax.experimental.pallas import tpu as pltpu

F32 = jnp.float32
BF16 = jnp.bfloat16
U32 = jnp.uint32

CHUNK = 64
GLA_HEADS = 4
GLA_GATE_RANK = 16
GLA_TAU = 16.0
S5_GROUP = 16
S5_STATE = 64
LN_EPS = 1e-5
RMS_EPS = 1e-6

S5_T = 16
S5_OCT = 16
GLA_STEP_CHUNKS = 4
S5_TILE_BLOCKS = 64
OUT_ROW_CHUNK = 256
OPS_WAVE = 8
LANES = 128
V7X_SCOPED_VMEM_CAP = 60000 * 1024


def _cparams(semantics, vmem_bytes):
    limit = min(int(vmem_bytes) + (6 << 20), V7X_SCOPED_VMEM_CAP)
    return pltpu.CompilerParams(dimension_semantics=semantics, vmem_limit_bytes=limit)


def _sigmoid(x):
    return 1.0 / (1.0 + jnp.exp(-x))


def _silu(x):
    return x * _sigmoid(x)


def _gelu_tanh(y):
    cdf = 0.5 * (1.0 + jnp.tanh(math.sqrt(2.0 / math.pi) * (y + 0.044715 * (y * y * y))))
    return y * cdf


def _adaln_kernel(c_ref, w_ref, b_ref, o_ref):
    c = c_ref[...]
    o_ref[...] = jnp.sum(_silu(c) * w_ref[...], axis=0, keepdims=True) + b_ref[...]


def _adaln(c, ada_w, ada_b, tn=1024):
    depth, d, n3 = ada_w.shape
    assert c.shape == (1, d), "batch 1 only"
    c_col = c.reshape(d, 1)
    out = pl.pallas_call(
        _adaln_kernel,
        grid=(depth, n3 // tn),
        in_specs=[
            pl.BlockSpec((d, 1), lambda l, j: (0, 0)),
            pl.BlockSpec((None, d, tn), lambda l, j: (l, 0, j)),
            pl.BlockSpec((None, 1, tn), lambda l, j: (l, 0, j)),
        ],
        out_specs=pl.BlockSpec((None, 1, tn), lambda l, j: (l, 0, j)),
        out_shape=jax.ShapeDtypeStruct((depth, 1, n3), F32),
        compiler_params=_cparams(("arbitrary", "arbitrary"), 2 * d * tn * 4 + d * LANES * 4),
        name="adaln_mod",
    )(c_col, ada_w, ada_b.reshape(depth, 1, n3))
    return out


def _modulate(x_ref, sc_ref, sh_ref):
    return (x_ref[...] * (1.0 + sc_ref[...]) + sh_ref[...]).astype(BF16)


def _proj_gate_kernel(x_ref, sc_ref, sh_ref, w_ref, wg_ref, o_ref, g_ref, u_scr):
    @pl.when(pl.program_id(1) == 0)
    def _():
        u = _modulate(x_ref, sc_ref, sh_ref)
        u_scr[...] = u
        g_ref[...] = jnp.dot(u, wg_ref[...].astype(BF16), preferred_element_type=F32)

    o_ref[...] = lax.dot_general(u_scr[...], w_ref[...].astype(BF16), (((1,), (1,)), ((), ())),
                                 preferred_element_type=F32).astype(o_ref.dtype)


def _sublane_transpose8(arrs):
    sub = lax.broadcasted_iota(jnp.int32, arrs[0].shape, 1)
    for b in range(3):
        dist = 1 << b
        sel = (sub & dist) != 0
        new = list(arrs)
        for j in range(8):
            if not j & dist:
                lo, hi = arrs[j], arrs[j + dist]
                new[j] = jnp.where(sel, pltpu.roll(hi, dist, 1), lo)
                new[j + dist] = jnp.where(sel, hi, pltpu.roll(lo, 8 - dist, 1))
        arrs = new
    return arrs


STEP_VIEW_LANES = 512


def _load_block_residues(x_ref, ls):
    tc = x_ref.shape[0]
    return [jnp.stack([x_ref[8 * cg + j, :, ls] for cg in range(tc // 8)]) for j in range(8)]


def _proj_perm_kernel(xa_ref, xb_ref, sc_ref, sh_ref, w_ref, o_ref, u_scr):
    tc, hs, d = xa_ref.shape
    assert hs == 8 and tc % 8 == 0

    @pl.when(pl.program_id(1) == 0)
    def _():
        for half, x_ref in enumerate((xa_ref, xb_ref)):
            for l0 in range(0, d, STEP_VIEW_LANES):
                ls = slice(l0, l0 + STEP_VIEW_LANES)
                rows = _sublane_transpose8(_load_block_residues(x_ref, ls))
                for sl in range(hs):
                    s = hs * half + sl
                    u = rows[sl].reshape(tc, STEP_VIEW_LANES) * (1.0 + sc_ref[:, ls]) + sh_ref[:, ls]
                    u_scr[s * tc:(s + 1) * tc, ls] = u.astype(BF16)

    o_ref[...] = jnp.dot(u_scr[...], w_ref[...].astype(BF16), preferred_element_type=F32).astype(o_ref.dtype)


def _in_proj_gate(x, scale, shift, wt, layer, wg, tm=1024, tn=1024):
    l, d = x.shape
    n = (wt.shape[1] // tn) * tn
    ng = wg.shape[1]
    tm = min(tm, l)
    vmem = 2 * tm * d * 4 + 2 * d * tn * 4 + 2 * tm * tn * 2 + tm * d * 2 + 2 * d * ng * 4 + 2 * tm * ng * 4
    return pl.pallas_call(
        _proj_gate_kernel,
        grid=(l // tm, n // tn),
        in_specs=[
            pl.BlockSpec((tm, d), lambda i, j: (i, 0)),
            pl.BlockSpec((1, d), lambda i, j: (0, 0)),
            pl.BlockSpec((1, d), lambda i, j: (0, 0)),
            pl.BlockSpec((None, tn, d), lambda i, j: (layer, j, 0)),
            pl.BlockSpec((d, ng), lambda i, j: (0, 0)),
        ],
        out_specs=[pl.BlockSpec((tm, tn), lambda i, j: (i, j)),
                   pl.BlockSpec((tm, ng), lambda i, j: (i, 0))],
        out_shape=[jax.ShapeDtypeStruct((l, n), BF16), jax.ShapeDtypeStruct((l, ng), F32)],
        scratch_shapes=[pltpu.VMEM((tm, d), BF16)],
        compiler_params=_cparams(("arbitrary", "arbitrary"), vmem),
        name="in_proj_gate",
    )(x, scale, shift, wt, wg)


def _step_major_view(x, t, tc):
    l, d = x.shape
    assert l % (t * tc) == 0 and (t // 2) % 8 == 0
    return x.reshape(l // t, 2, t // 2, d)


def _in_proj_perm(x, t, tc, scale, shift, w, layer, tn=1024):
    l, d = x.shape
    x4 = _step_major_view(x, t, tc)
    n = w.shape[2]
    tm = t * tc
    vmem = 2 * tm * d * 4 + 2 * d * tn * 4 + 2 * tm * tn * 2 + tm * d * 2
    return pl.pallas_call(
        _proj_perm_kernel,
        grid=(l // tm, n // tn),
        in_specs=[
            pl.BlockSpec((tc, None, t // 2, d), lambda i, j: (i, 0, 0, 0)),
            pl.BlockSpec((tc, None, t // 2, d), lambda i, j: (i, 1, 0, 0)),
            pl.BlockSpec((1, d), lambda i, j: (0, 0)),
            pl.BlockSpec((1, d), lambda i, j: (0, 0)),
            pl.BlockSpec((None, d, tn), lambda i, j: (layer, 0, j)),
        ],
        out_specs=pl.BlockSpec((tm, tn), lambda i, j: (i, j)),
        out_shape=jax.ShapeDtypeStruct((l, n), BF16),
        scratch_shapes=[pltpu.VMEM((tm, d), BF16)],
        compiler_params=_cparams(("arbitrary", "arbitrary"), vmem),
        name="in_proj_perm",
    )(x4, x4, scale, shift, w)


def _gla_core_kernel(*refs, dk, dv):
    nh = GLA_HEADS
    q_ref, k_ref = refs[0], refs[1]
    v_refs = refs[2:2 + nh]
    z_refs = refs[2 + nh:2 + 2 * nh]
    g_ref, w2_ref, gb_ref, ng_ref, o_ref, st_ref, kd_scr, ko_scr, qd_scr = refs[2 + 2 * nh:]

    @pl.when(pl.program_id(0) == 0)
    def _():
        st_ref[...] = jnp.zeros_like(st_ref)

    c = CHUNK
    r = q_ref.shape[0]
    sb = r // c
    nt = (((1,), (1,)), ((), ()))
    pre = jnp.dot(g_ref[...].astype(BF16), w2_ref[...], preferred_element_type=F32) + gb_ref[...]
    la = (jnp.minimum(pre, 0.0) - jnp.log(1.0 + jnp.exp(-jnp.abs(pre)))) * (1.0 / GLA_TAU)
    row = lax.broadcasted_iota(jnp.int32, (r, r), 0)
    col = lax.broadcasted_iota(jnp.int32, (r, r), 1)
    tri = (col <= row).astype(BF16)
    la_hi = la.astype(BF16)
    la_lo = (la - la_hi.astype(F32)).astype(BF16)
    gc = (jnp.dot(tri, la_hi, preferred_element_type=F32) + jnp.dot(tri, la_lo, preferred_element_type=F32))
    ends = [gc[c * (j + 1) - 1:c * (j + 1), :] for j in range(sb)]
    e_rows = jnp.concatenate([jnp.broadcast_to(ends[j], (c, gc.shape[1])) for j in range(sb)], axis=0)
    kf = k_ref[...].astype(F32) * jnp.exp(e_rows - gc)
    kd_scr[...] = kf.astype(BF16)
    ko_scr[...] = (kf * jnp.exp(ends[-1] - e_rows)).astype(BF16)
    qd_scr[...] = (q_ref[...].astype(F32) * jnp.exp(e_rows)).astype(BF16)
    dec = jnp.exp(ends[-1])
    for h in range(nh):
        ks = slice(h * dk, (h + 1) * dk)
        vs = slice(h * dv, (h + 1) * dv)
        st_in = st_ref[h]
        base = lax.dot_general(qd_scr[:, ks], st_in.astype(BF16), nt, preferred_element_type=F32)
        ams = []
        for m in range(sb):
            qs = jnp.concatenate(
                [(q_ref[c * j:c * (j + 1), ks].astype(F32) * jnp.exp(ends[j][:, ks] - ends[m][:, ks])).astype(BF16)
                 if j > m else q_ref[c * j:c * (j + 1), ks] for j in range(m, sb)], axis=0)
            kpad = jnp.concatenate(
                ([jnp.zeros((c * m, dk), BF16)] if m else []) + [kd_scr[c * m:c * (m + 1), ks]]
                + ([jnp.zeros((c * (sb - 1 - m), dk), BF16)] if m < sb - 1 else []), axis=0)
            ams.append(lax.dot_general(qs, kpad, nt, preferred_element_type=F32))
        a_rows = []
        for j in range(sb):
            aj = ams[0][c * j:c * (j + 1)]
            for m in range(1, j + 1):
                aj = aj + ams[m][c * (j - m):c * (j - m + 1)]
            a_rows.append(aj)
        a_full = jnp.concatenate(a_rows, axis=0).astype(BF16)
        o = base + jnp.dot(a_full, v_refs[h][...], preferred_element_type=F32)
        upd = lax.dot_general(v_refs[h][...], ko_scr[:, ks], (((0,), (0,)), ((), ())),
                              preferred_element_type=F32)
        st_ref[h] = dec[:, ks] * st_in + upd
        o = o * (dk ** -0.5)
        o = o * lax.rsqrt(jnp.mean(o * o, axis=-1, keepdims=True) + RMS_EPS)
        y = o * ng_ref[:, vs] * _silu(z_refs[h][...].astype(F32))
        o_ref[:, vs] = y.astype(o_ref.dtype)


def _gla_core(proj, glr, w2p, gate_b, norm_g, e, qk):
    l = proj.shape[0]
    nh = GLA_HEADS
    dk, dv = qk // nh, e // nh
    assert (2 * qk) % dv == 0
    v0 = 2 * qk // dv
    ng = glr.shape[1]
    c = min(GLA_STEP_CHUNKS * CHUNK, l)
    assert c % CHUNK == 0 and l % c == 0
    kern = functools.partial(_gla_core_kernel, dk=dk, dv=dv)
    vmem = (2 * (2 * c * e * 2 + 2 * c * qk * 2 + c * ng * 4 + ng * qk * 2 + c * e * 2) + nh * dv * dk * 4
            + 12 * c * qk * 4 + 4 * c * dv * 4)
    head_specs = [pl.BlockSpec((c, dv), functools.partial(lambda n, b: (n, b), b=v0 + h)) for h in range(2 * nh)]
    return pl.pallas_call(
        kern,
        grid=(l // c,),
        in_specs=[
            pl.BlockSpec((c, qk), lambda n: (n, 0)),
            pl.BlockSpec((c, qk), lambda n: (n, 1)),
            *head_specs,
            pl.BlockSpec((c, ng), lambda n: (n, 0)),
            pl.BlockSpec((ng, qk), lambda n: (0, 0)),
            pl.BlockSpec((1, qk), lambda n: (0, 0)),
            pl.BlockSpec((1, e), lambda n: (0, 0)),
        ],
        out_specs=pl.BlockSpec((c, e), lambda n: (n, 0)),
        out_shape=jax.ShapeDtypeStruct((l, e), BF16),
        scratch_shapes=[pltpu.VMEM((nh, dv, dk), F32)] + [pltpu.VMEM((c, qk), BF16) for _ in range(3)],
        compiler_params=_cparams(("arbitrary",), vmem),
        name="gla_core",
    )(*([proj] * (2 + 2 * nh)), glr, w2p, gate_b, norm_g)


def _out_ln_kernel(y_ref, w_ref, x_ref, gate_ref, g_ref, b_ref, o_ref, *scratch, alpha):
    tm = y_ref.shape[0]
    rc = min(OUT_ROW_CHUNK, tm)

    def layer_norm(r):
        mu = jnp.mean(r, axis=-1, keepdims=True)
        cen = r - mu
        var = jnp.mean(cen * cen, axis=-1, keepdims=True)
        return cen * lax.rsqrt(var + LN_EPS) * g_ref[...] + b_ref[...]

    def chunk_out(rows, x_rows):
        h = jnp.dot(y_ref[rows, :], w_ref[...], preferred_element_type=F32)
        return layer_norm(alpha * x_rows + (1.0 + gate_ref[...]) * h)

    if len(x_ref.shape) == 2:
        for r0 in range(0, tm, rc):
            rows = slice(r0, r0 + rc)
            o_ref[rows, :] = chunk_out(rows, x_ref[rows, :])
    else:
        (r_scr,) = scratch
        tc, hs, d = x_ref.shape
        assert hs == 8 and tc % 8 == 0 and rc % tc == 0
        chunks = [slice(l0, l0 + STEP_VIEW_LANES) for l0 in range(0, d, STEP_VIEW_LANES)]
        for ls in chunks:
            xs = _sublane_transpose8(_load_block_residues(x_ref, ls))
            for sl in range(hs):
                r_scr[sl * tc:(sl + 1) * tc, ls] = xs[sl].reshape(tc, STEP_VIEW_LANES)
        for r0 in range(0, tm, rc):
            rows = slice(r0, r0 + rc)
            r_scr[rows, :] = chunk_out(rows, r_scr[rows, :])
        for ls in chunks:
            outs = _sublane_transpose8([r_scr[sl * tc:(sl + 1) * tc, ls].reshape(tc // 8, 8, STEP_VIEW_LANES)
                                        for sl in range(hs)])
            for j in range(8):
                for cg in range(tc // 8):
                    o_ref[8 * cg + j, :, ls] = outs[j][cg]


def _out_ln(y, w, layer, x, gate, ln_g, ln_b, alpha, step_major=None, tm=512):
    l, e = y.shape
    d = w.shape[2]
    kern = functools.partial(_out_ln_kernel, alpha=alpha)
    if step_major is None:
        tm = min(tm, l)
        x_in, out_shape = x, (l, d)
        x_spec = pl.BlockSpec((tm, d), lambda i: (i, 0))
        scratch = []
    else:
        t, tc = step_major
        tm = tc * t // 2
        x_in = _step_major_view(x, t, tc)
        out_shape = x_in.shape
        x_spec = pl.BlockSpec((tc, None, t // 2, d), lambda i: (i // 2, i % 2, 0, 0))
        scratch = [pltpu.VMEM((tm, d), F32)]
    vmem = 2 * tm * e * 2 + e * d * 2 + 4 * tm * d * 4 + len(scratch) * tm * d * 4 + 4 * OUT_ROW_CHUNK * d * 4
    out = pl.pallas_call(
        kern,
        grid=(l // tm,),
        in_specs=[
            pl.BlockSpec((tm, e), lambda i: (i, 0)),
            pl.BlockSpec((None, e, d), lambda i: (layer, 0, 0), pipeline_mode=pl.Buffered(1)),
            x_spec,
            pl.BlockSpec((1, d), lambda i: (0, 0)),
            pl.BlockSpec((1, d), lambda i: (0, 0)),
            pl.BlockSpec((1, d), lambda i: (0, 0)),
        ],
        out_specs=x_spec,
        out_shape=jax.ShapeDtypeStruct(out_shape, F32),
        scratch_shapes=scratch,
        compiler_params=_cparams(("arbitrary",), vmem),
        name="out_proj_ln",
    )(y, w, x_in, gate, ln_g, ln_b)
    return out.reshape(l, d)


def _rot_blocks(v, nblk):
    nblk %= 8
    return pltpu.roll(v, S5_GROUP * nblk, 1) if nblk else v


def _skew_select(cols):
    blk = lax.shift_right_logical(lax.broadcasted_iota(jnp.int32, cols[0].shape, 1), 4)
    x = [cols[(-m) % 8] for m in range(8)]
    for b in range(3):
        dist = 1 << b
        sel = (blk & dist) != 0
        x = [jnp.where(sel, x[(m - dist) % 8], x[m]) for m in range(8)]
    return x


def _ssm_kernel(x_ref, w_ref, p_ref, q_ref, lr_ref, li_ref, d_ref, o_ref, a_scr, y_scr, vre, vim, sre, sim):
    ntile, t_steps, tc, _ = x_ref.shape
    nc = ntile * tc
    npair = S5_OCT // 2

    def gather_steps(tile, carry):
        r = pl.ds(pl.multiple_of(tile * tc, tc), tc)
        for s_hi in range(2):
            for g_hi in range(2):
                cols = [_rot_blocks(pltpu.bitcast(x_ref[tile, t_steps - 1 - (8 * s_hi + m), :,
                                                        g_hi * LANES:(g_hi + 1) * LANES], U32), m) for m in range(8)]
                res = _skew_select(cols)
                for m in range(8):
                    a_scr[8 * g_hi + m, r, s_hi * LANES:(s_hi + 1) * LANES] = pltpu.bitcast(res[m], BF16)
        return carry

    lax.fori_loop(0, ntile, gather_steps, 0)
    a_bf = a_scr

    nt = (((1,), (1,)), ((), ()))
    for p in range(npair):
        v = (lax.dot_general(a_bf[2 * p], p_ref[2 * p], nt, preferred_element_type=F32)
             + lax.dot_general(a_bf[2 * p + 1], p_ref[2 * p + 1], nt, preferred_element_type=F32))
        vre[pl.ds(p, nc, stride=npair), :] = v[:, :LANES]
        vim[pl.ds(p, nc, stride=npair), :] = v[:, LANES:]

    ar = lr_ref[...]
    ai = li_ref[...]

    def step(c, carry):
        xr, xi = carry
        rows = pl.ds(pl.multiple_of(c * npair, npair), npair)
        sre[rows, :] = xr
        sim[rows, :] = xi
        nxr = ar * xr - ai * xi + vre[rows, :]
        nxi = ar * xi + ai * xr + vim[rows, :]
        return nxr, nxi

    zero = jnp.zeros((npair, LANES), F32)
    lax.fori_loop(0, nc, step, (zero, zero), unroll=8)

    for p in range(npair):
        s = jnp.concatenate([sre[pl.ds(p, nc, stride=npair), :], sim[pl.ds(p, nc, stride=npair), :]],
                            axis=1).astype(BF16)
        for g in (2 * p, 2 * p + 1):
            y_scr[g] = (jnp.dot(a_bf[g], w_ref[g], preferred_element_type=F32)
                        + jnp.dot(s, q_ref[g], preferred_element_type=F32))

    rbo = min(32, tc)
    per_tile = tc // rbo

    def scatter_steps(it, carry):
        r = pl.ds(pl.multiple_of(it * rbo, rbo), rbo)
        tile = it // per_tile
        rt = pl.ds(pl.multiple_of((it % per_tile) * rbo, rbo), rbo)
        for s_hi in range(2):
            for g_hi in range(2):
                cs = slice(g_hi * LANES, (g_hi + 1) * LANES)
                cols = [y_scr[8 * g_hi + m, r, s_hi * LANES:(s_hi + 1) * LANES] for m in range(8)]
                res = _skew_select(cols)
                dsk = d_ref[:, cs]
                for m in range(8):
                    s = 8 * s_hi + m
                    y = _rot_blocks(res[m], -m) + dsk * x_ref[tile, s, rt, cs].astype(F32)
                    o_ref[tile, s, rt, cs] = _gelu_tanh(y).astype(o_ref.dtype)
        return carry

    lax.fori_loop(0, nc // rbo, scatter_steps, 0)


def _ssm_core(uz, tc, wt, pm, qm, lam_r, lam_i, d_skip, e):
    t = S5_T
    nc = uz.shape[0] // t
    kk = t * S5_GROUP
    assert kk == 2 * LANES and S5_OCT * S5_GROUP == kk
    noct = e // kk
    npair = S5_OCT // 2
    ntile = nc // tc
    x4 = uz.reshape(ntile, t, tc, uz.shape[1])
    blk = (ntile, t, tc, kk)
    vmem = (2 * (2 * t * nc * kk * 2 + 3 * S5_OCT * kk * kk * 2) + S5_OCT * nc * kk * (2 + 4)
            + 4 * nc * npair * LANES * 4)
    out = pl.pallas_call(
        _ssm_kernel,
        grid=(noct,),
        in_specs=[
            pl.BlockSpec(blk, lambda i: (0, 0, 0, i)),
            pl.BlockSpec((S5_OCT, kk, kk), lambda i: (i, 0, 0)),
            pl.BlockSpec((S5_OCT, kk, kk), lambda i: (i, 0, 0)),
            pl.BlockSpec((S5_OCT, kk, kk), lambda i: (i, 0, 0)),
            pl.BlockSpec((None, npair, LANES), lambda i: (i, 0, 0)),
            pl.BlockSpec((None, npair, LANES), lambda i: (i, 0, 0)),
            pl.BlockSpec((1, kk), lambda i: (0, i)),
        ],
        out_specs=pl.BlockSpec(blk, lambda i: (0, 0, 0, i)),
        out_shape=jax.ShapeDtypeStruct((ntile, t, tc, e), BF16),
        scratch_shapes=[pltpu.VMEM((S5_OCT, nc, kk), BF16), pltpu.VMEM((S5_OCT, nc, kk), F32)]
        + [pltpu.VMEM((nc * npair, LANES), F32) for _ in range(4)],
        compiler_params=_cparams(("arbitrary",), vmem),
        name="s5_ssm",
    )(x4, wt, pm, qm, lam_r, lam_i, d_skip)
    return out.reshape(t * nc, e)


def _cmul(ar, ai, br, bi):
    return ar * br - ai * bi, ar * bi + ai * br


def _bf16_terms(x, n):
    terms = []
    for _ in range(n):
        p = x.astype(BF16)
        terms.append(p)
        x = x - p.astype(F32)
    return terms


def _zoh(a_re, a_im, dt):
    mag = jnp.exp(a_re * dt)
    return mag * jnp.cos(a_im * dt), mag * jnp.sin(a_im * dt)


def _s5_ops_kernel(are_ref, aim_ref, ldt_ref, bre_ref, bim_ref, btre_ref, btim_ref, ctre_ref, ctim_ref,
                   arp_ref, aip_ref, ldtp_ref, w_ref, pt_ref, q_ref, lamr_ref, lami_ref, rows_scr):
    t, gi = S5_T, S5_GROUP
    ns = are_ref.shape[1]
    kk = t * gi

    pr, pi = _zoh(arp_ref[...], aip_ref[...], jnp.exp(ldtp_ref[...]))
    for _ in range(t.bit_length() - 1):
        pr, pi = _cmul(pr, pi, pr, pi)
    lamr_ref[...] = pr
    lami_ref[...] = pi

    ar = are_ref[...]
    ai = aim_ref[...]
    l1r, l1i = _zoh(ar, ai, jnp.exp(ldt_ref[...]))
    nr = l1r - 1.0
    den = ar * ar + ai * ai
    quantities = [l1r, l1i]
    for _ in range(3):
        quantities += list(_cmul(quantities[-2], quantities[-1], quantities[-2], quantities[-1]))
    quantities += [(nr * ar + l1i * ai) / den, (l1i * ar - nr * ai) / den]
    for k, val in enumerate(quantities):
        rows_scr[k] = val

    eye = lax.broadcasted_iota(jnp.int32, (ns, ns), 0) == lax.broadcasted_iota(jnp.int32, (ns, ns), 1)
    lane = lax.broadcasted_iota(jnp.int32, (ns, kk), 1)
    tau = lax.shift_right_logical(lane, 4)
    bits = [(lax.shift_right_logical(tau, b) & 1) == 1 for b in range(4)]
    expand = (lax.broadcasted_iota(jnp.int32, (gi, kk), 0)
              == (lax.broadcasted_iota(jnp.int32, (gi, kk), 1) & (gi - 1))).astype(BF16)
    lane_w = lax.broadcasted_iota(jnp.int32, (gi, LANES), 1)

    def column(k, g):
        row = rows_scr[k, pl.ds(g, 1), :]
        return jnp.sum(jnp.where(eye, row, 0.0), axis=1, keepdims=True)

    def lane_powers(cols):
        pr = pi = None
        for b in range(4):
            fr = jnp.where(bits[b], cols[2 * b], 1.0)
            fi = jnp.where(bits[b], cols[2 * b + 1], 0.0)
            pr, pi = (fr, fi) if pr is None else _cmul(pr, pi, fr, fi)
        return pr, pi

    def dot3(a, b):
        (a1, a2), (b1, b2) = _bf16_terms(a, 2), _bf16_terms(b, 2)
        return (jnp.dot(a1, b1, preferred_element_type=F32) + jnp.dot(a1, b2, preferred_element_type=F32)
                + jnp.dot(a2, b1, preferred_element_type=F32))

    def tile_channels(x):
        return sum(jnp.dot(p, expand, preferred_element_type=F32) for p in _bf16_terms(x, 3))

    def stage_scalars(g):
        cols = [column(k, g) for k in range(10)]
        cr, ci = cols[8], cols[9]
        bbr = cr * bre_ref[g] - ci * bim_ref[g]
        bbi = cr * bim_ref[g] + ci * bre_ref[g]
        crow = rows_scr[8, pl.ds(g, 1), :]
        cirow = rows_scr[9, pl.ds(g, 1), :]
        bbtr = crow * btre_ref[g] - cirow * btim_ref[g]
        bbti = crow * btim_ref[g] + cirow * btre_ref[g]
        return cols, bbr, bbi, lane_powers(cols), bbtr, bbti

    def stage_tiles(g, st):
        _, bbr, bbi = st[:3]
        return (tile_channels(ctre_ref[g]), tile_channels(ctim_ref[g]),
                tile_channels(bbr), tile_channels(bbi))

    def stage_kernel(st, tiles):
        (pwr, pwi), bbt_r, bbt_i = st[3:]
        clr, cli = _cmul(tiles[0], tiles[1], pwr, pwi)
        return clr, cli, dot3(bbt_r, clr) - dot3(bbt_i, cli)

    def stage_store(g, st, tiles, kern):
        cols, _, _, (pwr, pwi) = st[:4]
        _, _, bbtr, bbti = tiles
        clr, cli, kt = kern
        odd = g % 2
        gm = g % 8

        def rot_halves(v):
            return jnp.concatenate([_rot_blocks(v[:, :LANES], gm), _rot_blocks(v[:, LANES:], gm)], axis=1)

        k0, k1 = kt[:, :LANES], kt[:, LANES:]
        for s in range(t):
            sh = (gi * s) % LANES
            r0 = pltpu.roll(k0, sh, 1) if sh else k0
            if gi * s < LANES:
                r1 = pltpu.roll(k1, sh, 1) if sh else k1
                lo = jnp.where(lane_w >= sh, r0, 0.0)
                hi = jnp.where(lane_w >= sh, r1, r0)
            else:
                lo = jnp.zeros_like(k0)
                hi = jnp.where(lane_w >= sh, r0, 0.0)
            sa = t - 1 - s
            row0 = gi * (8 * (sa // 8) + (sa + gm) % 8)
            w_ref[g, row0:row0 + gi, :] = rot_halves(jnp.concatenate([lo, hi], axis=1)).astype(w_ref.dtype)
        ptr, pti = _cmul(pwr, pwi, bbtr, bbti)
        qr, qi = _cmul(clr, cli, cols[0], cols[1])
        zero = jnp.zeros((ns, kk), pt_ref.dtype)
        for ref, re, im in ((pt_ref, ptr, pti), (q_ref, qr, -qi)):
            ref[g, odd * ns:(odd + 1) * ns, :] = rot_halves(re).astype(ref.dtype)
            ref[g, (1 - odd) * ns:(2 - odd) * ns, :] = zero
            ref[g, (2 + odd) * ns:(3 + odd) * ns, :] = rot_halves(im).astype(ref.dtype)
            ref[g, (3 - odd) * ns:(4 - odd) * ns, :] = zero

    for g0 in range(0, S5_OCT, OPS_WAVE):
        wave = range(g0, g0 + OPS_WAVE)
        sts = [stage_scalars(g) for g in wave]
        tiles = [stage_tiles(g, st) for g, st in zip(wave, sts)]
        kerns = [stage_kernel(st, tl) for st, tl in zip(sts, tiles)]
        for g, st, tl, kn in zip(wave, sts, tiles, kerns):
            stage_store(g, st, tl, kn)


def _s5_operators(a_re, a_im, log_dt, b_re, b_im, c_re, c_im):
    g, ns = a_re.shape
    t, gi = S5_T, S5_GROUP
    kk = t * gi
    assert t == 16 and 4 * ns == kk and 2 * ns == LANES
    noct = g // S5_OCT
    npair = S5_OCT // 2
    a_re_p = a_re.reshape(g // 2, 2 * ns)
    a_im_p = a_im.reshape(g // 2, 2 * ns)
    ldt_p = jnp.broadcast_to(log_dt[:, None], (g, ns)).reshape(g // 2, 2 * ns)
    oct3 = lambda i: (i, 0, 0)
    ops_shape = jax.ShapeDtypeStruct((g, kk, kk), BF16)
    lam_shape = jax.ShapeDtypeStruct((noct, npair, LANES), F32)
    return pl.pallas_call(
        _s5_ops_kernel,
        grid=(noct,),
        in_specs=[
            pl.BlockSpec((S5_OCT, ns), lambda i: (i, 0)),
            pl.BlockSpec((S5_OCT, ns), lambda i: (i, 0)),
            pl.BlockSpec((S5_OCT, 1), lambda i: (i, 0)),
            pl.BlockSpec((S5_OCT, ns, gi), oct3),
            pl.BlockSpec((S5_OCT, ns, gi), oct3),
            pl.BlockSpec((S5_OCT, gi, ns), oct3),
            pl.BlockSpec((S5_OCT, gi, ns), oct3),
            pl.BlockSpec((S5_OCT, ns, gi), oct3),
            pl.BlockSpec((S5_OCT, ns, gi), oct3),
            pl.BlockSpec((npair, LANES), lambda i: (i, 0)),
            pl.BlockSpec((npair, LANES), lambda i: (i, 0)),
            pl.BlockSpec((npair, LANES), lambda i: (i, 0)),
        ],
        out_specs=[pl.BlockSpec((S5_OCT, kk, kk), oct3)] * 3 + [pl.BlockSpec((None, npair, LANES), oct3)] * 2,
        out_shape=[ops_shape] * 3 + [lam_shape] * 2,
        scratch_shapes=[pltpu.VMEM((10, S5_OCT, ns), F32)],
        compiler_params=_cparams(("arbitrary",), 2 * 3 * S5_OCT * kk * kk * 2 + 4 * S5_OCT * ns * LANES * 4 * 2),
        name="s5_ops",
    )(a_re, a_im, log_dt.reshape(g, 1), b_re, b_im, jnp.swapaxes(b_re, 1, 2), jnp.swapaxes(b_im, 1, 2),
      jnp.swapaxes(c_re, 1, 2), jnp.swapaxes(c_im, 1, 2), a_re_p, a_im_p, ldt_p)


def _glu_kernel(ya_ref, w_ref, b_ref, yc_ref, z_ref, o_ref):
    acc = jnp.dot(ya_ref[...], w_ref[...].astype(BF16), preferred_element_type=F32) + b_ref[...]
    o_ref[...] = (yc_ref[...].astype(F32) * _sigmoid(acc) * _silu(z_ref[...].astype(F32))).astype(o_ref.dtype)


def _glu(yact, w_glu, layer, b_glu, uz, tm=1024, tn=512):
    l, e = yact.shape
    tm = min(tm, l)
    zoff = e // tn
    vmem = 2 * (tm * e * 2 + e * tn * 4 + 3 * tm * tn * 2)
    return pl.pallas_call(
        _glu_kernel,
        grid=(l // tm, e // tn),
        in_specs=[pl.BlockSpec((tm, e), lambda i, j: (i, 0)),
                  pl.BlockSpec((None, e, tn), lambda i, j: (layer, 0, j)),
                  pl.BlockSpec((1, tn), lambda i, j: (0, j)),
                  pl.BlockSpec((tm, tn), lambda i, j: (i, j)),
                  pl.BlockSpec((tm, tn), lambda i, j: (i, zoff + j))],
        out_specs=pl.BlockSpec((tm, tn), lambda i, j: (i, j)),
        out_shape=jax.ShapeDtypeStruct((l, e), BF16),
        compiler_params=_cparams(("arbitrary", "arbitrary"), vmem),
        name="s5_glu",
    )(yact, w_glu, b_glu, yact, uz)


def _gla_layer(x, scale, shift, gate, ln_g, ln_b, alpha, w_in, layer, gate_w2, gate_b, norm_g, w_out):
    d = x.shape[1]
    e = w_out.shape[1]
    qk = gate_w2.shape[1]
    wg = jnp.pad(w_in[layer, :, 2 * qk + 2 * e:], ((0, 0), (0, LANES - GLA_GATE_RANK)))
    w2p = jnp.pad(gate_w2, ((0, LANES - GLA_GATE_RANK), (0, 0))).astype(BF16)
    proj, glr = _in_proj_gate(x, scale, shift, jnp.swapaxes(w_in, 1, 2), layer, wg)
    y = _gla_core(proj, glr, w2p, gate_b.reshape(1, qk), norm_g.reshape(1, e), e, qk)
    return _out_ln(y, w_out, layer, x, gate, ln_g.reshape(1, d), ln_b.reshape(1, d), alpha)


def _s5_layer(x, scale, shift, gate, ln_g, ln_b, alpha, w_in, layer, a_re, a_im, log_dt, b_re, b_im, c_re, c_im,
              d_skip, w_glu, b_glu, w_out):
    l, d = x.shape
    e = w_out.shape[1]
    t = S5_T
    tc = min(S5_TILE_BLOCKS, l // t)
    uz = _in_proj_perm(x, t, tc, scale, shift, w_in, layer)
    wt, pm, qm, lam_r, lam_i = _s5_operators(a_re, a_im, log_dt, b_re, b_im, c_re, c_im)
    yact = _ssm_core(uz, tc, wt, pm, qm, lam_r, lam_i, d_skip.reshape(1, e), e)
    yglu = _glu(yact, w_glu, layer, b_glu.reshape(1, e), uz)
    return _out_ln(yglu, w_out, layer, x, gate, ln_g.reshape(1, d), ln_b.reshape(1, d), alpha,
                   step_major=(t, tc))


def kernel(x, c, ln_g, ln_b, ada_w, ada_b, gla_w_in, gla_gate_w2, gla_gate_b, gla_norm_g, gla_w_out,
           s5_w_in, s5_a_re, s5_a_im, s5_log_dt, s5_b_re, s5_b_im, s5_c_re, s5_c_im, s5_d,
           s5_w_glu, s5_b_glu, s5_w_out):
    bsz, l, d = x.shape
    assert bsz == 1, "batch 1 only"
    depth = ln_g.shape[0]
    alpha = (2 * depth) ** 0.25
    mod = _adaln(c, ada_w, ada_b)
    h = x.reshape(l, d)
    gla_w_out_b = gla_w_out.astype(BF16)
    s5_w_out_b = s5_w_out.astype(BF16)
    for i in range(depth):
        shift, scale, gate = mod[i, :, :d], mod[i, :, d:2 * d], mod[i, :, 2 * d:]
        j = i // 2
        if i % 2 == 0:
            h = _gla_layer(h, scale, shift, gate, ln_g[i], ln_b[i], alpha, gla_w_in, j, gla_gate_w2[j],
                           gla_gate_b[j], gla_norm_g[j], gla_w_out_b)
        else:
            h = _s5_layer(h, scale, shift, gate, ln_g[i], ln_b[i], alpha, s5_w_in, j, s5_a_re[j], s5_a_im[j],
                          s5_log_dt[j], s5_b_re[j], s5_b_im[j], s5_c_re[j], s5_c_im[j], s5_d[j],
                          s5_w_glu, s5_b_glu[j], s5_w_out_b)
    return h.reshape(bsz, l, d)
```

```python
import functools
import math

import jax
import jax.numpy as jnp
from jax import lax
from jax.experimental import pallas as pl
from jax.experimental.pallas import tpu as pltpu

F32 = jnp.float32
BF16 = jnp.bfloat16
U32 = jnp.uint32

CHUNK = 64
GLA_HEADS = 4
GLA_GATE_RANK = 16
GLA_TAU = 16.0
S5_GROUP = 16
S5_STATE = 64
LN_EPS = 1e-5
RMS_EPS = 1e-6

S5_T = 16
S5_OCT = 16
GLA_STEP_CHUNKS = 4
S5_TILE_BLOCKS = 64
OUT_ROW_CHUNK = 256
OPS_WAVE = 8
STEP_VIEW_LANES = 512
LANES = 128
V7X_SCOPED_VMEM_CAP = 60000 * 1024


def _cparams(semantics, vmem_bytes):
    limit = min(int(vmem_bytes) + (6 << 20), V7X_SCOPED_VMEM_CAP)
    return pltpu.CompilerParams(dimension_semantics=semantics, vmem_limit_bytes=limit)


def _sigmoid(x):
    return 1.0 / (1.0 + jnp.exp(-x))


def _silu(x):
    return x * _sigmoid(x)


def _gelu_tanh(y):
    cdf = 0.5 * (1.0 + jnp.tanh(math.sqrt(2.0 / math.pi) * (y + 0.044715 * (y * y * y))))
    return y * cdf


def _adaln_kernel(c_ref, w_ref, b_ref, o_ref):
    c = c_ref[...]
    o_ref[...] = jnp.sum(_silu(c) * w_ref[...], axis=0, keepdims=True) + b_ref[...]


def _adaln(c, ada_w, ada_b, tn=1024):
    depth, d, n3 = ada_w.shape
    assert c.shape == (1, d), "batch 1 only"
    c_col = c.reshape(d, 1)
    out = pl.pallas_call(
        _adaln_kernel,
        grid=(depth, n3 // tn),
        in_specs=[
            pl.BlockSpec((d, 1), lambda l, j: (0, 0)),
            pl.BlockSpec((None, d, tn), lambda l, j: (l, 0, j)),
            pl.BlockSpec((None, 1, tn), lambda l, j: (l, 0, j)),
        ],
        out_specs=pl.BlockSpec((None, 1, tn), lambda l, j: (l, 0, j)),
        out_shape=jax.ShapeDtypeStruct((depth, 1, n3), F32),
        compiler_params=_cparams(("arbitrary", "arbitrary"), 2 * d * tn * 4 + d * LANES * 4),
        name="adaln_mod",
    )(c_col, ada_w, ada_b.reshape(depth, 1, n3))
    return out


def _modulate(x_ref, sc_ref, sh_ref):
    return (x_ref[...] * (1.0 + sc_ref[...]) + sh_ref[...]).astype(BF16)


def _proj_gate_kernel(x_ref, sc_ref, sh_ref, w_ref, wg_ref, o_ref, g_ref, u_scr):
    @pl.when(pl.program_id(1) == 0)
    def _():
        u = _modulate(x_ref, sc_ref, sh_ref)
        u_scr[...] = u
        g_ref[...] = jnp.dot(u, wg_ref[...].astype(BF16), preferred_element_type=F32)

    o_ref[...] = lax.dot_general(u_scr[...], w_ref[...].astype(BF16), (((1,), (1,)), ((), ())),
                                 preferred_element_type=F32).astype(o_ref.dtype)


def _sublane_transpose8(arrs):
    sub = lax.broadcasted_iota(jnp.int32, arrs[0].shape, 1)
    for b in range(3):
        dist = 1 << b
        sel = (sub & dist) != 0
        new = list(arrs)
        for j in range(8):
            if not j & dist:
                lo, hi = arrs[j], arrs[j + dist]
                new[j] = jnp.where(sel, pltpu.roll(hi, dist, 1), lo)
                new[j + dist] = jnp.where(sel, hi, pltpu.roll(lo, 8 - dist, 1))
        arrs = new
    return arrs


def _load_block_residues(x_ref, ls):
    tc = x_ref.shape[0]
    return [jnp.stack([x_ref[8 * cg + j, :, ls] for cg in range(tc // 8)]) for j in range(8)]


def _proj_perm_kernel(xa_ref, xb_ref, sc_ref, sh_ref, w_ref, o_ref, u_scr):
    tc, hs, d = xa_ref.shape
    assert hs == 8 and tc % 8 == 0

    @pl.when(pl.program_id(1) == 0)
    def _():
        for half, x_ref in enumerate((xa_ref, xb_ref)):
            for l0 in range(0, d, STEP_VIEW_LANES):
                ls = slice(l0, l0 + STEP_VIEW_LANES)
                rows = _sublane_transpose8(_load_block_residues(x_ref, ls))
                for sl in range(hs):
                    s = hs * half + sl
                    u = rows[sl].reshape(tc, STEP_VIEW_LANES) * (1.0 + sc_ref[:, ls]) + sh_ref[:, ls]
                    u_scr[s * tc:(s + 1) * tc, ls] = u.astype(BF16)

    o_ref[...] = jnp.dot(u_scr[...], w_ref[...].astype(BF16), preferred_element_type=F32).astype(o_ref.dtype)


def _in_proj_gate(x, scale, shift, wt, layer, wg, tm=1024, tn=1024):
    l, d = x.shape
    n = (wt.shape[1] // tn) * tn
    ng = wg.shape[1]
    tm = min(tm, l)
    vmem = 2 * tm * d * 4 + 2 * d * tn * 4 + 2 * tm * tn * 2 + tm * d * 2 + 2 * d * ng * 4 + 2 * tm * ng * 4
    return pl.pallas_call(
        _proj_gate_kernel,
        grid=(l // tm, n // tn),
        in_specs=[
            pl.BlockSpec((tm, d), lambda i, j: (i, 0)),
            pl.BlockSpec((1, d), lambda i, j: (0, 0)),
            pl.BlockSpec((1, d), lambda i, j: (0, 0)),
            pl.BlockSpec((None, tn, d), lambda i, j: (layer, j, 0)),
            pl.BlockSpec((d, ng), lambda i, j: (0, 0)),
        ],
        out_specs=[pl.BlockSpec((tm, tn), lambda i, j: (i, j)),
                   pl.BlockSpec((tm, ng), lambda i, j: (i, 0))],
        out_shape=[jax.ShapeDtypeStruct((l, n), BF16), jax.ShapeDtypeStruct((l, ng), F32)],
        scratch_shapes=[pltpu.VMEM((tm, d), BF16)],
        compiler_params=_cparams(("arbitrary", "arbitrary"), vmem),
        name="in_proj_gate",
    )(x, scale, shift, wt, wg)


def _step_major_view(x, t, tc):
    l, d = x.shape
    assert l % (t * tc) == 0 and (t // 2) % 8 == 0
    return x.reshape(l // t, 2, t // 2, d)


def _in_proj_perm(x, t, tc, scale, shift, w, layer, tn=1024):
    l, d = x.shape
    x4 = _step_major_view(x, t, tc)
    n = w.shape[2]
    tm = t * tc
    vmem = 2 * tm * d * 4 + 2 * d * tn * 4 + 2 * tm * tn * 2 + tm * d * 2
    return pl.pallas_call(
        _proj_perm_kernel,
        grid=(l // tm, n // tn),
        in_specs=[
            pl.BlockSpec((tc, None, t // 2, d), lambda i, j: (i, 0, 0, 0)),
            pl.BlockSpec((tc, None, t // 2, d), lambda i, j: (i, 1, 0, 0)),
            pl.BlockSpec((1, d), lambda i, j: (0, 0)),
            pl.BlockSpec((1, d), lambda i, j: (0, 0)),
            pl.BlockSpec((None, d, tn), lambda i, j: (layer, 0, j)),
        ],
        out_specs=pl.BlockSpec((tm, tn), lambda i, j: (i, j)),
        out_shape=jax.ShapeDtypeStruct((l, n), BF16),
        scratch_shapes=[pltpu.VMEM((tm, d), BF16)],
        compiler_params=_cparams(("arbitrary", "arbitrary"), vmem),
        name="in_proj_perm",
    )(x4, x4, scale, shift, w)


def _gla_core_kernel(*refs, dk, dv):
    nh = GLA_HEADS
    q_ref, k_ref = refs[0], refs[1]
    v_refs = refs[2:2 + nh]
    z_refs = refs[2 + nh:2 + 2 * nh]
    g_ref, w2_ref, gb_ref, ng_ref, o_ref, st_ref, kd_scr, ko_scr, qd_scr = refs[2 + 2 * nh:]

    @pl.when(pl.program_id(0) == 0)
    def _():
        st_ref[...] = jnp.zeros_like(st_ref)

    c = CHUNK
    r = q_ref.shape[0]
    sb = r // c
    nt = (((1,), (1,)), ((), ()))
    pre = jnp.dot(g_ref[...].astype(BF16), w2_ref[...], preferred_element_type=F32) + gb_ref[...]
    la = (jnp.minimum(pre, 0.0) - jnp.log(1.0 + jnp.exp(-jnp.abs(pre)))) * (1.0 / GLA_TAU)
    row = lax.broadcasted_iota(jnp.int32, (r, r), 0)
    col = lax.broadcasted_iota(jnp.int32, (r, r), 1)
    tri = (col <= row).astype(BF16)
    la_hi = la.astype(BF16)
    la_lo = (la - la_hi.astype(F32)).astype(BF16)
    gc = (jnp.dot(tri, la_hi, preferred_element_type=F32) + jnp.dot(tri, la_lo, preferred_element_type=F32))
    ends = [gc[c * (j + 1) - 1:c * (j + 1), :] for j in range(sb)]
    e_rows = jnp.concatenate([jnp.broadcast_to(ends[j], (c, gc.shape[1])) for j in range(sb)], axis=0)
    kf = k_ref[...].astype(F32) * jnp.exp(e_rows - gc)
    kd_scr[...] = kf.astype(BF16)
    ko_scr[...] = (kf * jnp.exp(ends[-1] - e_rows)).astype(BF16)
    qd_scr[...] = (q_ref[...].astype(F32) * jnp.exp(e_rows)).astype(BF16)
    dec = jnp.exp(ends[-1])
    for h in range(nh):
        ks = slice(h * dk, (h + 1) * dk)
        vs = slice(h * dv, (h + 1) * dv)
        st_in = st_ref[h]
        base = lax.dot_general(qd_scr[:, ks], st_in.astype(BF16), nt, preferred_element_type=F32)
        ams = []
        for m in range(sb):
            qs = jnp.concatenate(
                [(q_ref[c * j:c * (j + 1), ks].astype(F32) * jnp.exp(ends[j][:, ks] - ends[m][:, ks])).astype(BF16)
                 if j > m else q_ref[c * j:c * (j + 1), ks] for j in range(m, sb)], axis=0)
            kpad = jnp.concatenate(
                ([jnp.zeros((c * m, dk), BF16)] if m else []) + [kd_scr[c * m:c * (m + 1), ks]]
                + ([jnp.zeros((c * (sb - 1 - m), dk), BF16)] if m < sb - 1 else []), axis=0)
            ams.append(lax.dot_general(qs, kpad, nt, preferred_element_type=F32))
        a_rows = []
        for j in range(sb):
            aj = ams[0][c * j:c * (j + 1)]
            for m in range(1, j + 1):
                aj = aj + ams[m][c * (j - m):c * (j - m + 1)]
            a_rows.append(aj)
        a_full = jnp.concatenate(a_rows, axis=0).astype(BF16)
        o = base + jnp.dot(a_full, v_refs[h][...], preferred_element_type=F32)
        upd = lax.dot_general(v_refs[h][...], ko_scr[:, ks], (((0,), (0,)), ((), ())),
                              preferred_element_type=F32)
        st_ref[h] = dec[:, ks] * st_in + upd
        o = o * (dk ** -0.5)
        o = o * lax.rsqrt(jnp.mean(o * o, axis=-1, keepdims=True) + RMS_EPS)
        y = o * ng_ref[:, vs] * _silu(z_refs[h][...].astype(F32))
        o_ref[:, vs] = y.astype(o_ref.dtype)


def _gla_core(proj, glr, w2p, gate_b, norm_g, e, qk):
    l = proj.shape[0]
    nh = GLA_HEADS
    dk, dv = qk // nh, e // nh
    assert (2 * qk) % dv == 0
    v0 = 2 * qk // dv
    ng = glr.shape[1]
    c = min(GLA_STEP_CHUNKS * CHUNK, l)
    assert c % CHUNK == 0 and l % c == 0
    kern = functools.partial(_gla_core_kernel, dk=dk, dv=dv)
    vmem = (2 * (2 * c * e * 2 + 2 * c * qk * 2 + c * ng * 4 + ng * qk * 2 + c * e * 2) + nh * dv * dk * 4
            + 12 * c * qk * 4 + 4 * c * dv * 4)
    head_specs = [pl.BlockSpec((c, dv), functools.partial(lambda n, b: (n, b), b=v0 + h)) for h in range(2 * nh)]
    return pl.pallas_call(
        kern,
        grid=(l // c,),
        in_specs=[
            pl.BlockSpec((c, qk), lambda n: (n, 0)),
            pl.BlockSpec((c, qk), lambda n: (n, 1)),
            *head_specs,
            pl.BlockSpec((c, ng), lambda n: (n, 0)),
            pl.BlockSpec((ng, qk), lambda n: (0, 0)),
            pl.BlockSpec((1, qk), lambda n: (0, 0)),
            pl.BlockSpec((1, e), lambda n: (0, 0)),
        ],
        out_specs=pl.BlockSpec((c, e), lambda n: (n, 0)),
        out_shape=jax.ShapeDtypeStruct((l, e), BF16),
        scratch_shapes=[pltpu.VMEM((nh, dv, dk), F32)] + [pltpu.VMEM((c, qk), BF16) for _ in range(3)],
        compiler_params=_cparams(("arbitrary",), vmem),
        name="gla_core",
    )(*([proj] * (2 + 2 * nh)), glr, w2p, gate_b, norm_g)


def _out_ln_kernel(y_ref, w_ref, x_ref, gate_ref, g_ref, b_ref, o_ref, *scratch, alpha):
    tm = y_ref.shape[0]
    rc = min(OUT_ROW_CHUNK, tm)

    def layer_norm(r):
        mu = jnp.mean(r, axis=-1, keepdims=True)
        cen = r - mu
        var = jnp.mean(cen * cen, axis=-1, keepdims=True)
        return cen * lax.rsqrt(var + LN_EPS) * g_ref[...] + b_ref[...]

    def chunk_out(rows, x_rows):
        h = jnp.dot(y_ref[rows, :], w_ref[...], preferred_element_type=F32)
        return layer_norm(alpha * x_rows + (1.0 + gate_ref[...]) * h)

    if len(x_ref.shape) == 2:
        for r0 in range(0, tm, rc):
            rows = slice(r0, r0 + rc)
            o_ref[rows, :] = chunk_out(rows, x_ref[rows, :])
    else:
        (r_scr,) = scratch
        tc, hs, d = x_ref.shape
        assert hs == 8 and tc % 8 == 0 and rc % tc == 0
        chunks = [slice(l0, l0 + STEP_VIEW_LANES) for l0 in range(0, d, STEP_VIEW_LANES)]
        for ls in chunks:
            xs = _sublane_transpose8(_load_block_residues(x_ref, ls))
            for sl in range(hs):
                r_scr[sl * tc:(sl + 1) * tc, ls] = xs[sl].reshape(tc, STEP_VIEW_LANES)
        for r0 in range(0, tm, rc):
            rows = slice(r0, r0 + rc)
            r_scr[rows, :] = chunk_out(rows, r_scr[rows, :])
        for ls in chunks:
            outs = _sublane_transpose8([r_scr[sl * tc:(sl + 1) * tc, ls].reshape(tc // 8, 8, STEP_VIEW_LANES)
                                        for sl in range(hs)])
            for j in range(8):
                for cg in range(tc // 8):
                    o_ref[8 * cg + j, :, ls] = outs[j][cg]


def _out_ln(y, w, layer, x, gate, ln_g, ln_b, alpha, step_major=None, tm=512):
    l, e = y.shape
    d = w.shape[2]
    kern = functools.partial(_out_ln_kernel, alpha=alpha)
    if step_major is None:
        tm = min(tm, l)
        x_in, out_shape = x, (l, d)
        x_spec = pl.BlockSpec((tm, d), lambda i: (i, 0))
        scratch = []
    else:
        t, tc = step_major
        tm = tc * t // 2
        x_in = _step_major_view(x, t, tc)
        out_shape = x_in.shape
        x_spec = pl.BlockSpec((tc, None, t // 2, d), lambda i: (i // 2, i % 2, 0, 0))
        scratch = [pltpu.VMEM((tm, d), F32)]
    vmem = 2 * tm * e * 2 + e * d * 2 + 4 * tm * d * 4 + len(scratch) * tm * d * 4 + 4 * OUT_ROW_CHUNK * d * 4
    out = pl.pallas_call(
        kern,
        grid=(l // tm,),
        in_specs=[
            pl.BlockSpec((tm, e), lambda i: (i, 0)),
            pl.BlockSpec((None, e, d), lambda i: (layer, 0, 0), pipeline_mode=pl.Buffered(1)),
            x_spec,
            pl.BlockSpec((1, d), lambda i: (0, 0)),
            pl.BlockSpec((1, d), lambda i: (0, 0)),
            pl.BlockSpec((1, d), lambda i: (0, 0)),
        ],
        out_specs=x_spec,
        out_shape=jax.ShapeDtypeStruct(out_shape, F32),
        scratch_shapes=scratch,
        compiler_params=_cparams(("arbitrary",), vmem),
        name="out_proj_ln",
    )(y, w, x_in, gate, ln_g, ln_b)
    return out.reshape(l, d)


def _rot_blocks(v, nblk):
    nblk %= 8
    return pltpu.roll(v, S5_GROUP * nblk, 1) if nblk else v


def _skew_select(cols):
    blk = lax.shift_right_logical(lax.broadcasted_iota(jnp.int32, cols[0].shape, 1), 4)
    x = [cols[(-m) % 8] for m in range(8)]
    for b in range(3):
        dist = 1 << b
        sel = (blk & dist) != 0
        x = [jnp.where(sel, x[(m - dist) % 8], x[m]) for m in range(8)]
    return x


def _ssm_kernel(x_ref, w_ref, p_ref, q_ref, lr_ref, li_ref, d_ref, o_ref, a_scr, y_scr, vre, vim, sre, sim):
    ntile, t_steps, tc, _ = x_ref.shape
    nc = ntile * tc
    npair = S5_OCT // 2

    def gather_steps(tile, carry):
        r = pl.ds(pl.multiple_of(tile * tc, tc), tc)
        for s_hi in range(2):
            for g_hi in range(2):
                cols = [_rot_blocks(pltpu.bitcast(x_ref[tile, t_steps - 1 - (8 * s_hi + m), :,
                                                        g_hi * LANES:(g_hi + 1) * LANES], U32), m) for m in range(8)]
                res = _skew_select(cols)
                for m in range(8):
                    a_scr[8 * g_hi + m, r, s_hi * LANES:(s_hi + 1) * LANES] = pltpu.bitcast(res[m], BF16)
        return carry

    lax.fori_loop(0, ntile, gather_steps, 0)
    a_bf = a_scr

    nt = (((1,), (1,)), ((), ()))
    for p in range(npair):
        v = (lax.dot_general(a_bf[2 * p], p_ref[2 * p], nt, preferred_element_type=F32)
             + lax.dot_general(a_bf[2 * p + 1], p_ref[2 * p + 1], nt, preferred_element_type=F32))
        vre[pl.ds(p, nc, stride=npair), :] = v[:, :LANES]
        vim[pl.ds(p, nc, stride=npair), :] = v[:, LANES:]

    ar = lr_ref[...]
    ai = li_ref[...]

    def step(c, carry):
        xr, xi = carry
        rows = pl.ds(pl.multiple_of(c * npair, npair), npair)
        sre[rows, :] = xr
        sim[rows, :] = xi
        nxr = ar * xr - ai * xi + vre[rows, :]
        nxi = ar * xi + ai * xr + vim[rows, :]
        return nxr, nxi

    zero = jnp.zeros((npair, LANES), F32)
    lax.fori_loop(0, nc, step, (zero, zero), unroll=8)

    for p in range(npair):
        s = jnp.concatenate([sre[pl.ds(p, nc, stride=npair), :], sim[pl.ds(p, nc, stride=npair), :]],
                            axis=1).astype(BF16)
        for g in (2 * p, 2 * p + 1):
            y_scr[g] = (jnp.dot(a_bf[g], w_ref[g], preferred_element_type=F32)
                        + jnp.dot(s, q_ref[g], preferred_element_type=F32))

    rbo = min(32, tc)
    per_tile = tc // rbo

    def scatter_steps(it, carry):
        r = pl.ds(pl.multiple_of(it * rbo, rbo), rbo)
        tile = it // per_tile
        rt = pl.ds(pl.multiple_of((it % per_tile) * rbo, rbo), rbo)
        for s_hi in range(2):
            for g_hi in range(2):
                cs = slice(g_hi * LANES, (g_hi + 1) * LANES)
                cols = [y_scr[8 * g_hi + m, r, s_hi * LANES:(s_hi + 1) * LANES] for m in range(8)]
                res = _skew_select(cols)
                dsk = d_ref[:, cs]
                for m in range(8):
                    s = 8 * s_hi + m
                    y = _rot_blocks(res[m], -m) + dsk * x_ref[tile, s, rt, cs].astype(F32)
                    o_ref[tile, s, rt, cs] = _gelu_tanh(y).astype(o_ref.dtype)
        return carry

    lax.fori_loop(0, nc // rbo, scatter_steps, 0)


def _ssm_core(uz, tc, wt, pm, qm, lam_r, lam_i, d_skip, e):
    t = S5_T
    nc = uz.shape[0] // t
    kk = t * S5_GROUP
    assert kk == 2 * LANES and S5_OCT * S5_GROUP == kk
    noct = e // kk
    npair = S5_OCT // 2
    ntile = nc // tc
    x4 = uz.reshape(ntile, t, tc, uz.shape[1])
    blk = (ntile, t, tc, kk)
    vmem = (2 * (2 * t * nc * kk * 2 + 3 * S5_OCT * kk * kk * 2) + S5_OCT * nc * kk * (2 + 4)
            + 4 * nc * npair * LANES * 4)
    out = pl.pallas_call(
        _ssm_kernel,
        grid=(noct,),
        in_specs=[
            pl.BlockSpec(blk, lambda i: (0, 0, 0, i)),
            pl.BlockSpec((S5_OCT, kk, kk), lambda i: (i, 0, 0)),
            pl.BlockSpec((S5_OCT, kk, kk), lambda i: (i, 0, 0)),
            pl.BlockSpec((S5_OCT, kk, kk), lambda i: (i, 0, 0)),
            pl.BlockSpec((None, npair, LANES), lambda i: (i, 0, 0)),
            pl.BlockSpec((None, npair, LANES), lambda i: (i, 0, 0)),
            pl.BlockSpec((1, kk), lambda i: (0, i)),
        ],
        out_specs=pl.BlockSpec(blk, lambda i: (0, 0, 0, i)),
        out_shape=jax.ShapeDtypeStruct((ntile, t, tc, e), BF16),
        scratch_shapes=[pltpu.VMEM((S5_OCT, nc, kk), BF16), pltpu.VMEM((S5_OCT, nc, kk), F32)]
        + [pltpu.VMEM((nc * npair, LANES), F32) for _ in range(4)],
        compiler_params=_cparams(("arbitrary",), vmem),
        name="s5_ssm",
    )(x4, wt, pm, qm, lam_r, lam_i, d_skip)
    return out.reshape(t * nc, e)


def _cmul(ar, ai, br, bi):
    return ar * br - ai * bi, ar * bi + ai * br


def _bf16_terms(x, n):
    terms = []
    for _ in range(n):
        p = x.astype(BF16)
        terms.append(p)
        x = x - p.astype(F32)
    return terms


def _zoh(a_re, a_im, dt):
    mag = jnp.exp(a_re * dt)
    return mag * jnp.cos(a_im * dt), mag * jnp.sin(a_im * dt)


def _s5_ops_kernel(are_ref, aim_ref, ldt_ref, bre_ref, bim_ref, btre_ref, btim_ref, ctre_ref, ctim_ref,
                   arp_ref, aip_ref, ldtp_ref, w_ref, pt_ref, q_ref, lamr_ref, lami_ref, rows_scr):
    t, gi = S5_T, S5_GROUP
    ns = are_ref.shape[1]
    kk = t * gi

    pr, pi = _zoh(arp_ref[...], aip_ref[...], jnp.exp(ldtp_ref[...]))
    for _ in range(t.bit_length() - 1):
        pr, pi = _cmul(pr, pi, pr, pi)
    lamr_ref[...] = pr
    lami_ref[...] = pi

    ar = are_ref[...]
    ai = aim_ref[...]
    l1r, l1i = _zoh(ar, ai, jnp.exp(ldt_ref[...]))
    nr = l1r - 1.0
    den = ar * ar + ai * ai
    quantities = [l1r, l1i]
    for _ in range(3):
        quantities += list(_cmul(quantities[-2], quantities[-1], quantities[-2], quantities[-1]))
    quantities += [(nr * ar + l1i * ai) / den, (l1i * ar - nr * ai) / den]
    for k, val in enumerate(quantities):
        rows_scr[k] = val

    eye = lax.broadcasted_iota(jnp.int32, (ns, ns), 0) == lax.broadcasted_iota(jnp.int32, (ns, ns), 1)
    lane = lax.broadcasted_iota(jnp.int32, (ns, kk), 1)
    tau = lax.shift_right_logical(lane, 4)
    bits = [(lax.shift_right_logical(tau, b) & 1) == 1 for b in range(4)]
    expand = (lax.broadcasted_iota(jnp.int32, (gi, kk), 0)
              == (lax.broadcasted_iota(jnp.int32, (gi, kk), 1) & (gi - 1))).astype(BF16)
    lane_w = lax.broadcasted_iota(jnp.int32, (gi, LANES), 1)

    def column(k, g):
        row = rows_scr[k, pl.ds(g, 1), :]
        return jnp.sum(jnp.where(eye, row, 0.0), axis=1, keepdims=True)

    def lane_powers(cols):
        pr = pi = None
        for b in range(4):
            fr = jnp.where(bits[b], cols[2 * b], 1.0)
            fi = jnp.where(bits[b], cols[2 * b + 1], 0.0)
            pr, pi = (fr, fi) if pr is None else _cmul(pr, pi, fr, fi)
        return pr, pi

    def dot3(a, b):
        (a1, a2), (b1, b2) = _bf16_terms(a, 2), _bf16_terms(b, 2)
        return (jnp.dot(a1, b1, preferred_element_type=F32) + jnp.dot(a1, b2, preferred_element_type=F32)
                + jnp.dot(a2, b1, preferred_element_type=F32))

    def tile_channels(x):
        return sum(jnp.dot(p, expand, preferred_element_type=F32) for p in _bf16_terms(x, 3))

    def stage_scalars(g):
        cols = [column(k, g) for k in range(10)]
        cr, ci = cols[8], cols[9]
        bbr = cr * bre_ref[g] - ci * bim_ref[g]
        bbi = cr * bim_ref[g] + ci * bre_ref[g]
        crow = rows_scr[8, pl.ds(g, 1), :]
        cirow = rows_scr[9, pl.ds(g, 1), :]
        bbtr = crow * btre_ref[g] - cirow * btim_ref[g]
        bbti = crow * btim_ref[g] + cirow * btre_ref[g]
        return cols, bbr, bbi, lane_powers(cols), bbtr, bbti

    def stage_tiles(g, st):
        _, bbr, bbi = st[:3]
        return (tile_channels(ctre_ref[g]), tile_channels(ctim_ref[g]),
                tile_channels(bbr), tile_channels(bbi))

    def stage_kernel(st, tiles):
        (pwr, pwi), bbt_r, bbt_i = st[3:]
        clr, cli = _cmul(tiles[0], tiles[1], pwr, pwi)
        return clr, cli, dot3(bbt_r, clr) - dot3(bbt_i, cli)

    def stage_store(g, st, tiles, kern):
        cols, _, _, (pwr, pwi) = st[:4]
        _, _, bbtr, bbti = tiles
        clr, cli, kt = kern
        odd = g % 2
        gm = g % 8

        def rot_halves(v):
            return jnp.concatenate([_rot_blocks(v[:, :LANES], gm), _rot_blocks(v[:, LANES:], gm)], axis=1)

        k0, k1 = kt[:, :LANES], kt[:, LANES:]
        for s in range(t):
            sh = (gi * s) % LANES
            r0 = pltpu.roll(k0, sh, 1) if sh else k0
            if gi * s < LANES:
                r1 = pltpu.roll(k1, sh, 1) if sh else k1
                lo = jnp.where(lane_w >= sh, r0, 0.0)
                hi = jnp.where(lane_w >= sh, r1, r0)
            else:
                lo = jnp.zeros_like(k0)
                hi = jnp.where(lane_w >= sh, r0, 0.0)
            sa = t - 1 - s
            row0 = gi * (8 * (sa // 8) + (sa + gm) % 8)
            w_ref[g, row0:row0 + gi, :] = rot_halves(jnp.concatenate([lo, hi], axis=1)).astype(w_ref.dtype)
        ptr, pti = _cmul(pwr, pwi, bbtr, bbti)
        qr, qi = _cmul(clr, cli, cols[0], cols[1])
        zero = jnp.zeros((ns, kk), pt_ref.dtype)
        for ref, re, im in ((pt_ref, ptr, pti), (q_ref, qr, -qi)):
            ref[g, odd * ns:(odd + 1) * ns, :] = rot_halves(re).astype(ref.dtype)
            ref[g, (1 - odd) * ns:(2 - odd) * ns, :] = zero
            ref[g, (2 + odd) * ns:(3 + odd) * ns, :] = rot_halves(im).astype(ref.dtype)
            ref[g, (3 - odd) * ns:(4 - odd) * ns, :] = zero

    for g0 in range(0, S5_OCT, OPS_WAVE):
        wave = range(g0, g0 + OPS_WAVE)
        sts = [stage_scalars(g) for g in wave]
        tiles = [stage_tiles(g, st) for g, st in zip(wave, sts)]
        kerns = [stage_kernel(st, tl) for st, tl in zip(sts, tiles)]
        for g, st, tl, kn in zip(wave, sts, tiles, kerns):
            stage_store(g, st, tl, kn)


def _s5_operators(a_re, a_im, log_dt, b_re, b_im, c_re, c_im):
    g, ns = a_re.shape
    t, gi = S5_T, S5_GROUP
    kk = t * gi
    assert t == 16 and 4 * ns == kk and 2 * ns == LANES
    noct = g // S5_OCT
    npair = S5_OCT // 2
    a_re_p = a_re.reshape(g // 2, 2 * ns)
    a_im_p = a_im.reshape(g // 2, 2 * ns)
    ldt_p = jnp.broadcast_to(log_dt[:, None], (g, ns)).reshape(g // 2, 2 * ns)
    oct3 = lambda i: (i, 0, 0)
    ops_shape = jax.ShapeDtypeStruct((g, kk, kk), BF16)
    lam_shape = jax.ShapeDtypeStruct((noct, npair, LANES), F32)
    return pl.pallas_call(
        _s5_ops_kernel,
        grid=(noct,),
        in_specs=[
            pl.BlockSpec((S5_OCT, ns), lambda i: (i, 0)),
            pl.BlockSpec((S5_OCT, ns), lambda i: (i, 0)),
            pl.BlockSpec((S5_OCT, 1), lambda i: (i, 0)),
            pl.BlockSpec((S5_OCT, ns, gi), oct3),
            pl.BlockSpec((S5_OCT, ns, gi), oct3),
            pl.BlockSpec((S5_OCT, gi, ns), oct3),
            pl.BlockSpec((S5_OCT, gi, ns), oct3),
            pl.BlockSpec((S5_OCT, ns, gi), oct3),
            pl.BlockSpec((S5_OCT, ns, gi), oct3),
            pl.BlockSpec((npair, LANES), lambda i: (i, 0)),
            pl.BlockSpec((npair, LANES), lambda i: (i, 0)),
            pl.BlockSpec((npair, LANES), lambda i: (i, 0)),
        ],
        out_specs=[pl.BlockSpec((S5_OCT, kk, kk), oct3)] * 3 + [pl.BlockSpec((None, npair, LANES), oct3)] * 2,
        out_shape=[ops_shape] * 3 + [lam_shape] * 2,
        scratch_shapes=[pltpu.VMEM((10, S5_OCT, ns), F32)],
        compiler_params=_cparams(("arbitrary",), 2 * 3 * S5_OCT * kk * kk * 2 + 4 * S5_OCT * ns * LANES * 4 * 2),
        name="s5_ops",
    )(a_re, a_im, log_dt.reshape(g, 1), b_re, b_im, jnp.swapaxes(b_re, 1, 2), jnp.swapaxes(b_im, 1, 2),
      jnp.swapaxes(c_re, 1, 2), jnp.swapaxes(c_im, 1, 2), a_re_p, a_im_p, ldt_p)


def _glu_kernel(ya_ref, w_ref, b_ref, yc_ref, z_ref, o_ref):
    acc = jnp.dot(ya_ref[...], w_ref[...].astype(BF16), preferred_element_type=F32) + b_ref[...]
    o_ref[...] = (yc_ref[...].astype(F32) * _sigmoid(acc) * _silu(z_ref[...].astype(F32))).astype(o_ref.dtype)


def _glu(yact, w_glu, layer, b_glu, uz, tm=1024, tn=512):
    l, e = yact.shape
    tm = min(tm, l)
    zoff = e // tn
    vmem = 2 * (tm * e * 2 + e * tn * 4 + 3 * tm * tn * 2)
    return pl.pallas_call(
        _glu_kernel,
        grid=(l // tm, e // tn),
        in_specs=[pl.BlockSpec((tm, e), lambda i, j: (i, 0)),
                  pl.BlockSpec((None, e, tn), lambda i, j: (layer, 0, j)),
                  pl.BlockSpec((1, tn), lambda i, j: (0, j)),
                  pl.BlockSpec((tm, tn), lambda i, j: (i, j)),
                  pl.BlockSpec((tm, tn), lambda i, j: (i, zoff + j))],
        out_specs=pl.BlockSpec((tm, tn), lambda i, j: (i, j)),
        out_shape=jax.ShapeDtypeStruct((l, e), BF16),
        compiler_params=_cparams(("arbitrary", "arbitrary"), vmem),
        name="s5_glu",
    )(yact, w_glu, b_glu, yact, uz)


def _gla_layer(x, scale, shift, gate, ln_g, ln_b, alpha, w_in, layer, gate_w2, gate_b, norm_g, w_out):
    d = x.shape[1]
    e = w_out.shape[1]
    qk = gate_w2.shape[1]
    wg = jnp.pad(w_in[layer, :, 2 * qk + 2 * e:], ((0, 0), (0, LANES - GLA_GATE_RANK)))
    w2p = jnp.pad(gate_w2, ((0, LANES - GLA_GATE_RANK), (0, 0))).astype(BF16)
    proj, glr = _in_proj_gate(x, scale, shift, jnp.swapaxes(w_in, 1, 2), layer, wg)
    y = _gla_core(proj, glr, w2p, gate_b.reshape(1, qk), norm_g.reshape(1, e), e, qk)
    return _out_ln(y, w_out, layer, x, gate, ln_g.reshape(1, d), ln_b.reshape(1, d), alpha)


def _s5_layer(x, scale, shift, gate, ln_g, ln_b, alpha, w_in, layer, a_re, a_im, log_dt, b_re, b_im, c_re, c_im,
              d_skip, w_glu, b_glu, w_out):
    l, d = x.shape
    e = w_out.shape[1]
    t = S5_T
    tc = min(S5_TILE_BLOCKS, l // t)
    uz = _in_proj_perm(x, t, tc, scale, shift, w_in, layer)
    wt, pm, qm, lam_r, lam_i = _s5_operators(a_re, a_im, log_dt, b_re, b_im, c_re, c_im)
    yact = _ssm_core(uz, tc, wt, pm, qm, lam_r, lam_i, d_skip.reshape(1, e), e)
    yglu = _glu(yact, w_glu, layer, b_glu.reshape(1, e), uz)
    return _out_ln(yglu, w_out, layer, x, gate, ln_g.reshape(1, d), ln_b.reshape(1, d), alpha,
                   step_major=(t, tc))


def kernel(x, c, ln_g, ln_b, ada_w, ada_b, gla_w_in, gla_gate_w2, gla_gate_b, gla_norm_g, gla_w_out,
           s5_w_in, s5_a_re, s5_a_im, s5_log_dt, s5_b_re, s5_b_im, s5_c_re, s5_c_im, s5_d,
           s5_w_glu, s5_b_glu, s5_w_out):
    bsz, l, d = x.shape
    assert bsz == 1, "batch 1 only"
    depth = ln_g.shape[0]
    alpha = (2 * depth) ** 0.25
    mod = _adaln(c, ada_w, ada_b)
    h = x.reshape(l, d)
    gla_w_out_b = gla_w_out.astype(BF16)
    s5_w_out_b = s5_w_out.astype(BF16)
    for i in range(depth):
        shift, scale, gate = mod[i, :, :d], mod[i, :, d:2 * d], mod[i, :, 2 * d:]
        j = i // 2
        if i % 2 == 0:
            h = _gla_layer(h, scale, shift, gate, ln_g[i], ln_b[i], alpha, gla_w_in, j, gla_gate_w2[j],
                           gla_gate_b[j], gla_norm_g[j], gla_w_out_b)
        else:
            h = _s5_layer(h, scale, shift, gate, ln_g[i], ln_b[i], alpha, s5_w_in, j, s5_a_re[j], s5_a_im[j],
                          s5_log_dt[j], s5_b_re[j], s5_b_im[j], s5_c_re[j], s5_c_im[j], s5_d[j],
                          s5_w_glu, s5_b_glu[j], s5_w_out_b)
    return h.reshape(bsz, l, d)
```

```python
import functools
import math

import jax
import jax.numpy as jnp
from jax import lax
from jax.experimental import pallas as pl
from jax.experimental.pallas import tpu as pltpu

F32 = jnp.float32
BF16 = jnp.bfloat16
U32 = jnp.uint32

CHUNK = 64
GLA_HEADS = 4
GLA_GATE_RANK = 16
GLA_TAU = 16.0
S5_GROUP = 16
S5_STATE = 64
LN_EPS = 1e-5
RMS_EPS = 1e-6

S5_T = 16
S5_OCT = 16
GLA_STEP_CHUNKS = 4
S5_TILE_BLOCKS = 64
OUT_ROW_CHUNK = 256
OPS_WAVE = 8
STEP_VIEW_LANES = 512
LANES = 128
V7X_SCOPED_VMEM_CAP = 60000 * 1024


def _cparams(semantics, vmem_bytes):
    limit = min(int(vmem_bytes) + (6 << 20), V7X_SCOPED_VMEM_CAP)
    return pltpu.CompilerParams(dimension_semantics=semantics, vmem_limit_bytes=limit)


def _sigmoid(x):
    return 1.0 / (1.0 + jnp.exp(-x))


def _silu(x):
    return x * _sigmoid(x)


def _gelu_tanh(y):
    cdf = 0.5 * (1.0 + jnp.tanh(math.sqrt(2.0 / math.pi) * (y + 0.044715 * (y * y * y))))
    return y * cdf


def _adaln_kernel(c_ref, w_ref, b_ref, o_ref):
    c = c_ref[...]
    o_ref[...] = jnp.sum(_silu(c) * w_ref[...], axis=0, keepdims=True) + b_ref[...]


def _adaln(c, ada_w, ada_b, tn=1024):
    depth, d, n3 = ada_w.shape
    assert c.shape == (1, d), "batch 1 only"
    c_col = c.reshape(d, 1)
    out = pl.pallas_call(
        _adaln_kernel,
        grid=(depth, n3 // tn),
        in_specs=[
            pl.BlockSpec((d, 1), lambda l, j: (0, 0)),
            pl.BlockSpec((None, d, tn), lambda l, j: (l, 0, j)),
            pl.BlockSpec((None, 1, tn), lambda l, j: (l, 0, j)),
        ],
        out_specs=pl.BlockSpec((None, 1, tn), lambda l, j: (l, 0, j)),
        out_shape=jax.ShapeDtypeStruct((depth, 1, n3), F32),
        compiler_params=_cparams(("arbitrary", "arbitrary"), 2 * d * tn * 4 + d * LANES * 4),
        name="adaln_mod",
    )(c_col, ada_w, ada_b.reshape(depth, 1, n3))
    return out


def _modulate(x_ref, sc_ref, sh_ref):
    return (x_ref[...] * (1.0 + sc_ref[...]) + sh_ref[...]).astype(BF16)


def _proj_gate_kernel(x_ref, sc_ref, sh_ref, w_ref, wg_ref, o_ref, g_ref, u_scr):
    @pl.when(pl.program_id(1) == 0)
    def _():
        u = _modulate(x_ref, sc_ref, sh_ref)
        u_scr[...] = u
        g_ref[...] = jnp.dot(u, wg_ref[...].astype(BF16), preferred_element_type=F32)

    o_ref[...] = lax.dot_general(u_scr[...], w_ref[...].astype(BF16), (((1,), (1,)), ((), ())),
                                 preferred_element_type=F32).astype(o_ref.dtype)


def _sublane_transpose8(arrs):
    sub = lax.broadcasted_iota(jnp.int32, arrs[0].shape, 1)
    for b in range(3):
        dist = 1 << b
        sel = (sub & dist) != 0
        new = list(arrs)
        for j in range(8):
            if not j & dist:
                lo, hi = arrs[j], arrs[j + dist]
                new[j] = jnp.where(sel, pltpu.roll(hi, dist, 1), lo)
                new[j + dist] = jnp.where(sel, hi, pltpu.roll(lo, 8 - dist, 1))
        arrs = new
    return arrs


def _load_block_residues(x_ref, ls):
    tc = x_ref.shape[0]
    return [jnp.stack([x_ref[8 * cg + j, :, ls] for cg in range(tc // 8)]) for j in range(8)]


def _proj_perm_kernel(xa_ref, xb_ref, sc_ref, sh_ref, w_ref, o_ref, u_scr):
    tc, hs, d = xa_ref.shape
    assert hs == 8 and tc % 8 == 0

    @pl.when(pl.program_id(1) == 0)
    def _():
        for half, x_ref in enumerate((xa_ref, xb_ref)):
            for l0 in range(0, d, STEP_VIEW_LANES):
                ls = slice(l0, l0 + STEP_VIEW_LANES)
                rows = _sublane_transpose8(_load_block_residues(x_ref, ls))
                for sl in range(hs):
                    s = hs * half + sl
                    u = rows[sl].reshape(tc, STEP_VIEW_LANES) * (1.0 + sc_ref[:, ls]) + sh_ref[:, ls]
                    u_scr[s * tc:(s + 1) * tc, ls] = u.astype(BF16)

    o_ref[...] = jnp.dot(u_scr[...], w_ref[...].astype(BF16), preferred_element_type=F32).astype(o_ref.dtype)


def _in_proj_gate(x, scale, shift, wt, layer, wg, tm=1024, tn=1024):
    l, d = x.shape
    n = (wt.shape[1] // tn) * tn
    ng = wg.shape[1]
    tm = min(tm, l)
    vmem = 2 * tm * d * 4 + 2 * d * tn * 4 + 2 * tm * tn * 2 + tm * d * 2 + 2 * d * ng * 4 + 2 * tm * ng * 4
    return pl.pallas_call(
        _proj_gate_kernel,
        grid=(l // tm, n // tn),
        in_specs=[
            pl.BlockSpec((tm, d), lambda i, j: (i, 0)),
            pl.BlockSpec((1, d), lambda i, j: (0, 0)),
            pl.BlockSpec((1, d), lambda i, j: (0, 0)),
            pl.BlockSpec((None, tn, d), lambda i, j: (layer, j, 0)),
            pl.BlockSpec((d, ng), lambda i, j: (0, 0)),
        ],
        out_specs=[pl.BlockSpec((tm, tn), lambda i, j: (i, j)),
                   pl.BlockSpec((tm, ng), lambda i, j: (i, 0))],
        out_shape=[jax.ShapeDtypeStruct((l, n), BF16), jax.ShapeDtypeStruct((l, ng), F32)],
        scratch_shapes=[pltpu.VMEM((tm, d), BF16)],
        compiler_params=_cparams(("arbitrary", "arbitrary"), vmem),
        name="in_proj_gate",
    )(x, scale, shift, wt, wg)


def _step_major_view(x, t, tc):
    l, d = x.shape
    assert l % (t * tc) == 0 and (t // 2) % 8 == 0
    return x.reshape(l // t, 2, t // 2, d)


def _in_proj_perm(x, t, tc, scale, shift, w, layer, tn=1024):
    l, d = x.shape
    x4 = _step_major_view(x, t, tc)
    n = w.shape[2]
    tm = t * tc
    vmem = 2 * tm * d * 4 + 2 * d * tn * 4 + 2 * tm * tn * 2 + tm * d * 2
    return pl.pallas_call(
        _proj_perm_kernel,
        grid=(l // tm, n // tn),
        in_specs=[
            pl.BlockSpec((tc, None, t // 2, d), lambda i, j: (i, 0, 0, 0)),
            pl.BlockSpec((tc, None, t // 2, d), lambda i, j: (i, 1, 0, 0)),
            pl.BlockSpec((1, d), lambda i, j: (0, 0)),
            pl.BlockSpec((1, d), lambda i, j: (0, 0)),
            pl.BlockSpec((None, d, tn), lambda i, j: (layer, 0, j)),
        ],
        out_specs=pl.BlockSpec((tm, tn), lambda i, j: (i, j)),
        out_shape=jax.ShapeDtypeStruct((l, n), BF16),
        scratch_shapes=[pltpu.VMEM((tm, d), BF16)],
        compiler_params=_cparams(("arbitrary", "arbitrary"), vmem),
        name="in_proj_perm",
    )(x4, x4, scale, shift, w)


def _gla_core_kernel(*refs, dk, dv):
    nh = GLA_HEADS
    q_ref, k_ref = refs[0], refs[1]
    v_refs = refs[2:2 + nh]
    z_refs = refs[2 + nh:2 + 2 * nh]
    g_ref, w2_ref, gb_ref, ng_ref, o_ref, st_ref, kd_scr, ko_scr, qd_scr = refs[2 + 2 * nh:]

    @pl.when(pl.program_id(0) == 0)
    def _():
        st_ref[...] = jnp.zeros_like(st_ref)

    c = CHUNK
    r = q_ref.shape[0]
    sb = r // c
    nt = (((1,), (1,)), ((), ()))
    pre = jnp.dot(g_ref[...].astype(BF16), w2_ref[...], preferred_element_type=F32) + gb_ref[...]
    la = (jnp.minimum(pre, 0.0) - jnp.log(1.0 + jnp.exp(-jnp.abs(pre)))) * (1.0 / GLA_TAU)
    row = lax.broadcasted_iota(jnp.int32, (r, r), 0)
    col = lax.broadcasted_iota(jnp.int32, (r, r), 1)
    tri = (col <= row).astype(BF16)
    la_hi = la.astype(BF16)
    la_lo = (la - la_hi.astype(F32)).astype(BF16)
    gc = (jnp.dot(tri, la_hi, preferred_element_type=F32) + jnp.dot(tri, la_lo, preferred_element_type=F32))
    ends = [gc[c * (j + 1) - 1:c * (j + 1), :] for j in range(sb)]
    e_rows = jnp.concatenate([jnp.broadcast_to(ends[j], (c, gc.shape[1])) for j in range(sb)], axis=0)
    kf = k_ref[...].astype(F32) * jnp.exp(e_rows - gc)
    kd_scr[...] = kf.astype(BF16)
    ko_scr[...] = (kf * jnp.exp(ends[-1] - e_rows)).astype(BF16)
    qd_scr[...] = (q_ref[...].astype(F32) * jnp.exp(e_rows)).astype(BF16)
    dec = jnp.exp(ends[-1])
    for h in range(nh):
        ks = slice(h * dk, (h + 1) * dk)
        vs = slice(h * dv, (h + 1) * dv)
        st_in = st_ref[h]
        base = lax.dot_general(qd_scr[:, ks], st_in.astype(BF16), nt, preferred_element_type=F32)
        ams = []
        for m in range(sb):
            qs = jnp.concatenate(
                [(q_ref[c * j:c * (j + 1), ks].astype(F32) * jnp.exp(ends[j][:, ks] - ends[m][:, ks])).astype(BF16)
                 if j > m else q_ref[c * j:c * (j + 1), ks] for j in range(m, sb)], axis=0)
            kpad = jnp.concatenate(
                ([jnp.zeros((c * m, dk), BF16)] if m else []) + [kd_scr[c * m:c * (m + 1), ks]]
                + ([jnp.zeros((c * (sb - 1 - m), dk), BF16)] if m < sb - 1 else []), axis=0)
            ams.append(lax.dot_general(qs, kpad, nt, preferred_element_type=F32))
        a_rows = []
        for j in range(sb):
            aj = ams[0][c * j:c * (j + 1)]
            for m in range(1, j + 1):
                aj = aj + ams[m][c * (j - m):c * (j - m + 1)]
            a_rows.append(aj)
        a_full = jnp.concatenate(a_rows, axis=0).astype(BF16)
        o = base + jnp.dot(a_full, v_refs[h][...], preferred_element_type=F32)
        upd = lax.dot_general(v_refs[h][...], ko_scr[:, ks], (((0,), (0,)), ((), ())),
                              preferred_element_type=F32)
        st_ref[h] = dec[:, ks] * st_in + upd
        o = o * (dk ** -0.5)
        o = o * lax.rsqrt(jnp.mean(o * o, axis=-1, keepdims=True) + RMS_EPS)
        y = o * ng_ref[:, vs] * _silu(z_refs[h][...].astype(F32))
        o_ref[:, vs] = y.astype(o_ref.dtype)


def _gla_core(proj, glr, w2p, gate_b, norm_g, e, qk):
    l = proj.shape[0]
    nh = GLA_HEADS
    dk, dv = qk // nh, e // nh
    assert (2 * qk) % dv == 0
    v0 = 2 * qk // dv
    ng = glr.shape[1]
    c = min(GLA_STEP_CHUNKS * CHUNK, l)
    assert c % CHUNK == 0 and l % c == 0
    kern = functools.partial(_gla_core_kernel, dk=dk, dv=dv)
    vmem = (2 * (2 * c * e * 2 + 2 * c * qk * 2 + c * ng * 4 + ng * qk * 2 + c * e * 2) + nh * dv * dk * 4
            + 12 * c * qk * 4 + 4 * c * dv * 4)
    head_specs = [pl.BlockSpec((c, dv), functools.partial(lambda n, b: (n, b), b=v0 + h)) for h in range(2 * nh)]
    return pl.pallas_call(
        kern,
        grid=(l // c,),
        in_specs=[
            pl.BlockSpec((c, qk), lambda n: (n, 0)),
            pl.BlockSpec((c, qk), lambda n: (n, 1)),
            *head_specs,
            pl.BlockSpec((c, ng), lambda n: (n, 0)),
            pl.BlockSpec((ng, qk), lambda n: (0, 0)),
            pl.BlockSpec((1, qk), lambda n: (0, 0)),
            pl.BlockSpec((1, e), lambda n: (0, 0)),
        ],
        out_specs=pl.BlockSpec((c, e), lambda n: (n, 0)),
        out_shape=jax.ShapeDtypeStruct((l, e), BF16),
        scratch_shapes=[pltpu.VMEM((nh, dv, dk), F32)] + [pltpu.VMEM((c, qk), BF16) for _ in range(3)],
        compiler_params=_cparams(("arbitrary",), vmem),
        name="gla_core",
    )(*([proj] * (2 + 2 * nh)), glr, w2p, gate_b, norm_g)


def _out_ln_kernel(y_ref, w_ref, x_ref, gate_ref, g_ref, b_ref, o_ref, *scratch, alpha):
    tm = y_ref.shape[0]
    rc = min(OUT_ROW_CHUNK, tm)

    def layer_norm(r):
        mu = jnp.mean(r, axis=-1, keepdims=True)
        cen = r - mu
        var = jnp.mean(cen * cen, axis=-1, keepdims=True)
        return cen * lax.rsqrt(var + LN_EPS) * g_ref[...] + b_ref[...]

    def chunk_out(rows, x_rows):
        h = jnp.dot(y_ref[rows, :], w_ref[...], preferred_element_type=F32)
        return layer_norm(alpha * x_rows + (1.0 + gate_ref[...]) * h)

    if len(x_ref.shape) == 2:
        for r0 in range(0, tm, rc):
            rows = slice(r0, r0 + rc)
            o_ref[rows, :] = chunk_out(rows, x_ref[rows, :])
    else:
        (r_scr,) = scratch
        tc, hs, d = x_ref.shape
        assert hs == 8 and tc % 8 == 0 and rc % tc == 0
        chunks = [slice(l0, l0 + STEP_VIEW_LANES) for l0 in range(0, d, STEP_VIEW_LANES)]
        for ls in chunks:
            xs = _sublane_transpose8(_load_block_residues(x_ref, ls))
            for sl in range(hs):
                r_scr[sl * tc:(sl + 1) * tc, ls] = xs[sl].reshape(tc, STEP_VIEW_LANES)
        for r0 in range(0, tm, rc):
            rows = slice(r0, r0 + rc)
            r_scr[rows, :] = chunk_out(rows, r_scr[rows, :])
        for ls in chunks:
            outs = _sublane_transpose8([r_scr[sl * tc:(sl + 1) * tc, ls].reshape(tc // 8, 8, STEP_VIEW_LANES)
                                        for sl in range(hs)])
            for j in range(8):
                for cg in range(tc // 8):
                    o_ref[8 * cg + j, :, ls] = outs[j][cg]


def _out_ln(y, w, layer, x, gate, ln_g, ln_b, alpha, step_major=None, tm=512):
    l, e = y.shape
    d = w.shape[2]
    kern = functools.partial(_out_ln_kernel, alpha=alpha)
    if step_major is None:
        tm = min(tm, l)
        x_in, out_shape = x, (l, d)
        x_spec = pl.BlockSpec((tm, d), lambda i: (i, 0))
        scratch = []
    else:
        t, tc = step_major
        tm = tc * t // 2
        x_in = _step_major_view(x, t, tc)
        out_shape = x_in.shape
        x_spec = pl.BlockSpec((tc, None, t // 2, d), lambda i: (i // 2, i % 2, 0, 0))
        scratch = [pltpu.VMEM((tm, d), F32)]
    vmem = 2 * tm * e * 2 + e * d * 2 + 4 * tm * d * 4 + len(scratch) * tm * d * 4 + 4 * OUT_ROW_CHUNK * d * 4
    out = pl.pallas_call(
        kern,
        grid=(l // tm,),
        in_specs=[
            pl.BlockSpec((tm, e), lambda i: (i, 0)),
            pl.BlockSpec((None, e, d), lambda i: (layer, 0, 0), pipeline_mode=pl.Buffered(1)),
            x_spec,
            pl.BlockSpec((1, d), lambda i: (0, 0)),
            pl.BlockSpec((1, d), lambda i: (0, 0)),
            pl.BlockSpec((1, d), lambda i: (0, 0)),
        ],
        out_specs=x_spec,
        out_shape=jax.ShapeDtypeStruct(out_shape, F32),
        scratch_shapes=scratch,
        compiler_params=_cparams(("arbitrary",), vmem),
        name="out_proj_ln",
    )(y, w, x_in, gate, ln_g, ln_b)
    return out.reshape(l, d)


def _rot_blocks(v, nblk):
    nblk %= 8
    return pltpu.roll(v, S5_GROUP * nblk, 1) if nblk else v


def _skew_select(cols):
    blk = lax.shift_right_logical(lax.broadcasted_iota(jnp.int32, cols[0].shape, 1), 4)
    x = [cols[(-m) % 8] for m in range(8)]
    for b in range(3):
        dist = 1 << b
        sel = (blk & dist) != 0
        x = [jnp.where(sel, x[(m - dist) % 8], x[m]) for m in range(8)]
    return x


def _ssm_kernel(x_ref, w_ref, p_ref, q_ref, lr_ref, li_ref, d_ref, o_ref, a_scr, y_scr, vre, vim, sre, sim):
    ntile, t_steps, tc, _ = x_ref.shape
    nc = ntile * tc
    npair = S5_OCT // 2

    def gather_steps(tile, carry):
        r = pl.ds(pl.multiple_of(tile * tc, tc), tc)
        for s_hi in range(2):
            for g_hi in range(2):
                cols = [_rot_blocks(pltpu.bitcast(x_ref[tile, t_steps - 1 - (8 * s_hi + m), :,
                                                        g_hi * LANES:(g_hi + 1) * LANES], U32), m) for m in range(8)]
                res = _skew_select(cols)
                for m in range(8):
                    a_scr[8 * g_hi + m, r, s_hi * LANES:(s_hi + 1) * LANES] = pltpu.bitcast(res[m], BF16)
        return carry

    lax.fori_loop(0, ntile, gather_steps, 0)
    a_bf = a_scr

    nt = (((1,), (1,)), ((), ()))
    for p in range(npair):
        v = (lax.dot_general(a_bf[2 * p], p_ref[2 * p], nt, preferred_element_type=F32)
             + lax.dot_general(a_bf[2 * p + 1], p_ref[2 * p + 1], nt, preferred_element_type=F32))
        vre[pl.ds(p, nc, stride=npair), :] = v[:, :LANES]
        vim[pl.ds(p, nc, stride=npair), :] = v[:, LANES:]

    ar = lr_ref[...]
    ai = li_ref[...]

    def step(c, carry):
        xr, xi = carry
        rows = pl.ds(pl.multiple_of(c * npair, npair), npair)
        sre[rows, :] = xr
        sim[rows, :] = xi
        nxr = ar * xr - ai * xi + vre[rows, :]
        nxi = ar * xi + ai * xr + vim[rows, :]
        return nxr, nxi

    zero = jnp.zeros((npair, LANES), F32)
    lax.fori_loop(0, nc, step, (zero, zero), unroll=8)

    for p in range(npair):
        s = jnp.concatenate([sre[pl.ds(p, nc, stride=npair), :], sim[pl.ds(p, nc, stride=npair), :]],
                            axis=1).astype(BF16)
        for g in (2 * p, 2 * p + 1):
            y_scr[g] = (jnp.dot(a_bf[g], w_ref[g], preferred_element_type=F32)
                        + jnp.dot(s, q_ref[g], preferred_element_type=F32))

    rbo = min(32, tc)
    per_tile = tc // rbo

    def scatter_steps(it, carry):
        r = pl.ds(pl.multiple_of(it * rbo, rbo), rbo)
        tile = it // per_tile
        rt = pl.ds(pl.multiple_of((it % per_tile) * rbo, rbo), rbo)
        for s_hi in range(2):
            for g_hi in range(2):
                cs = slice(g_hi * LANES, (g_hi + 1) * LANES)
                cols = [y_scr[8 * g_hi + m, r, s_hi * LANES:(s_hi + 1) * LANES] for m in range(8)]
                res = _skew_select(cols)
                dsk = d_ref[:, cs]
                for m in range(8):
                    s = 8 * s_hi + m
                    y = _rot_blocks(res[m], -m) + dsk * x_ref[tile, s, rt, cs].astype(F32)
                    o_ref[tile, s, rt, cs] = _gelu_tanh(y).astype(o_ref.dtype)
        return carry

    lax.fori_loop(0, nc // rbo, scatter_steps, 0)


def _ssm_core(uz, tc, wt, pm, qm, lam_r, lam_i, d_skip, e, oct0=0):
    t = S5_T
    nc = uz.shape[0] // t
    kk = t * S5_GROUP
    assert kk == 2 * LANES and S5_OCT * S5_GROUP == kk
    noct = e // kk
    npair = S5_OCT // 2
    ntile = nc // tc
    x4 = uz.reshape(ntile, t, tc, uz.shape[1])
    blk = (ntile, t, tc, kk)
    vmem = (2 * (2 * t * nc * kk * 2 + 3 * S5_OCT * kk * kk * 2) + S5_OCT * nc * kk * (2 + 4)
            + 4 * nc * npair * LANES * 4)
    out = pl.pallas_call(
        _ssm_kernel,
        grid=(noct,),
        in_specs=[
            pl.BlockSpec(blk, lambda i: (0, 0, 0, i)),
            pl.BlockSpec((S5_OCT, kk, kk), lambda i: (oct0 + i, 0, 0)),
            pl.BlockSpec((S5_OCT, kk, kk), lambda i: (oct0 + i, 0, 0)),
            pl.BlockSpec((S5_OCT, kk, kk), lambda i: (oct0 + i, 0, 0)),
            pl.BlockSpec((None, npair, LANES), lambda i: (oct0 + i, 0, 0)),
            pl.BlockSpec((None, npair, LANES), lambda i: (oct0 + i, 0, 0)),
            pl.BlockSpec((1, kk), lambda i: (0, i)),
        ],
        out_specs=pl.BlockSpec(blk, lambda i: (0, 0, 0, i)),
        out_shape=jax.ShapeDtypeStruct((ntile, t, tc, e), BF16),
        scratch_shapes=[pltpu.VMEM((S5_OCT, nc, kk), BF16), pltpu.VMEM((S5_OCT, nc, kk), F32)]
        + [pltpu.VMEM((nc * npair, LANES), F32) for _ in range(4)],
        compiler_params=_cparams(("arbitrary",), vmem),
        name="s5_ssm",
    )(x4, wt, pm, qm, lam_r, lam_i, d_skip)
    return out.reshape(t * nc, e)


def _cmul(ar, ai, br, bi):
    return ar * br - ai * bi, ar * bi + ai * br


def _bf16_terms(x, n):
    terms = []
    for _ in range(n):
        p = x.astype(BF16)
        terms.append(p)
        x = x - p.astype(F32)
    return terms


def _zoh(a_re, a_im, dt):
    mag = jnp.exp(a_re * dt)
    return mag * jnp.cos(a_im * dt), mag * jnp.sin(a_im * dt)


def _s5_ops_kernel(are_ref, aim_ref, ldt_ref, bre_ref, bim_ref, btre_ref, btim_ref, ctre_ref, ctim_ref,
                   arp_ref, aip_ref, ldtp_ref, w_ref, pt_ref, q_ref, lamr_ref, lami_ref, rows_scr):
    t, gi = S5_T, S5_GROUP
    ns = are_ref.shape[1]
    kk = t * gi

    pr, pi = _zoh(arp_ref[...], aip_ref[...], jnp.exp(ldtp_ref[...]))
    for _ in range(t.bit_length() - 1):
        pr, pi = _cmul(pr, pi, pr, pi)
    lamr_ref[...] = pr
    lami_ref[...] = pi

    ar = are_ref[...]
    ai = aim_ref[...]
    l1r, l1i = _zoh(ar, ai, jnp.exp(ldt_ref[...]))
    nr = l1r - 1.0
    den = ar * ar + ai * ai
    quantities = [l1r, l1i]
    for _ in range(3):
        quantities += list(_cmul(quantities[-2], quantities[-1], quantities[-2], quantities[-1]))
    quantities += [(nr * ar + l1i * ai) / den, (l1i * ar - nr * ai) / den]
    for k, val in enumerate(quantities):
        rows_scr[k] = val

    eye = lax.broadcasted_iota(jnp.int32, (ns, ns), 0) == lax.broadcasted_iota(jnp.int32, (ns, ns), 1)
    lane = lax.broadcasted_iota(jnp.int32, (ns, kk), 1)
    tau = lax.shift_right_logical(lane, 4)
    bits = [(lax.shift_right_logical(tau, b) & 1) == 1 for b in range(4)]
    expand = (lax.broadcasted_iota(jnp.int32, (gi, kk), 0)
              == (lax.broadcasted_iota(jnp.int32, (gi, kk), 1) & (gi - 1))).astype(BF16)
    lane_w = lax.broadcasted_iota(jnp.int32, (gi, LANES), 1)

    def column(k, g):
        row = rows_scr[k, pl.ds(g, 1), :]
        return jnp.sum(jnp.where(eye, row, 0.0), axis=1, keepdims=True)

    def lane_powers(cols):
        pr = pi = None
        for b in range(4):
            fr = jnp.where(bits[b], cols[2 * b], 1.0)
            fi = jnp.where(bits[b], cols[2 * b + 1], 0.0)
            pr, pi = (fr, fi) if pr is None else _cmul(pr, pi, fr, fi)
        return pr, pi

    def dot3(a, b):
        (a1, a2), (b1, b2) = _bf16_terms(a, 2), _bf16_terms(b, 2)
        return (jnp.dot(a1, b1, preferred_element_type=F32) + jnp.dot(a1, b2, preferred_element_type=F32)
                + jnp.dot(a2, b1, preferred_element_type=F32))

    def tile_channels(x):
        return sum(jnp.dot(p, expand, preferred_element_type=F32) for p in _bf16_terms(x, 3))

    def stage_scalars(g):
        cols = [column(k, g) for k in range(10)]
        cr, ci = cols[8], cols[9]
        bbr = cr * bre_ref[g] - ci * bim_ref[g]
        bbi = cr * bim_ref[g] + ci * bre_ref[g]
        crow = rows_scr[8, pl.ds(g, 1), :]
        cirow = rows_scr[9, pl.ds(g, 1), :]
        bbtr = crow * btre_ref[g] - cirow * btim_ref[g]
        bbti = crow * btim_ref[g] + cirow * btre_ref[g]
        return cols, bbr, bbi, lane_powers(cols), bbtr, bbti

    def stage_tiles(g, st):
        _, bbr, bbi = st[:3]
        return (tile_channels(ctre_ref[g]), tile_channels(ctim_ref[g]),
                tile_channels(bbr), tile_channels(bbi))

    def stage_kernel(st, tiles):
        (pwr, pwi), bbt_r, bbt_i = st[3:]
        clr, cli = _cmul(tiles[0], tiles[1], pwr, pwi)
        return clr, cli, dot3(bbt_r, clr) - dot3(bbt_i, cli)

    def stage_store(g, st, tiles, kern):
        cols, _, _, (pwr, pwi) = st[:4]
        _, _, bbtr, bbti = tiles
        clr, cli, kt = kern
        odd = g % 2
        gm = g % 8

        def rot_halves(v):
            return jnp.concatenate([_rot_blocks(v[:, :LANES], gm), _rot_blocks(v[:, LANES:], gm)], axis=1)

        k0, k1 = kt[:, :LANES], kt[:, LANES:]
        for s in range(t):
            sh = (gi * s) % LANES
            r0 = pltpu.roll(k0, sh, 1) if sh else k0
            if gi * s < LANES:
                r1 = pltpu.roll(k1, sh, 1) if sh else k1
                lo = jnp.where(lane_w >= sh, r0, 0.0)
                hi = jnp.where(lane_w >= sh, r1, r0)
            else:
                lo = jnp.zeros_like(k0)
                hi = jnp.where(lane_w >= sh, r0, 0.0)
            sa = t - 1 - s
            row0 = gi * (8 * (sa // 8) + (sa + gm) % 8)
            w_ref[g, row0:row0 + gi, :] = rot_halves(jnp.concatenate([lo, hi], axis=1)).astype(w_ref.dtype)
        ptr, pti = _cmul(pwr, pwi, bbtr, bbti)
        qr, qi = _cmul(clr, cli, cols[0], cols[1])
        zero = jnp.zeros((ns, kk), pt_ref.dtype)
        for ref, re, im in ((pt_ref, ptr, pti), (q_ref, qr, -qi)):
            ref[g, odd * ns:(odd + 1) * ns, :] = rot_halves(re).astype(ref.dtype)
            ref[g, (1 - odd) * ns:(2 - odd) * ns, :] = zero
            ref[g, (2 + odd) * ns:(3 + odd) * ns, :] = rot_halves(im).astype(ref.dtype)
            ref[g, (3 - odd) * ns:(4 - odd) * ns, :] = zero

    for g0 in range(0, S5_OCT, OPS_WAVE):
        wave = range(g0, g0 + OPS_WAVE)
        sts = [stage_scalars(g) for g in wave]
        tiles = [stage_tiles(g, st) for g, st in zip(wave, sts)]
        kerns = [stage_kernel(st, tl) for st, tl in zip(sts, tiles)]
        for g, st, tl, kn in zip(wave, sts, tiles, kerns):
            stage_store(g, st, tl, kn)


def _s5_operators(a_re, a_im, log_dt, b_re, b_im, c_re, c_im):
    g, ns = a_re.shape
    t, gi = S5_T, S5_GROUP
    kk = t * gi
    assert t == 16 and 4 * ns == kk and 2 * ns == LANES
    noct = g // S5_OCT
    npair = S5_OCT // 2
    a_re_p = a_re.reshape(g // 2, 2 * ns)
    a_im_p = a_im.reshape(g // 2, 2 * ns)
    ldt_p = jnp.broadcast_to(log_dt[:, None], (g, ns)).reshape(g // 2, 2 * ns)
    oct3 = lambda i: (i, 0, 0)
    ops_shape = jax.ShapeDtypeStruct((g, kk, kk), BF16)
    lam_shape = jax.ShapeDtypeStruct((noct, npair, LANES), F32)
    return pl.pallas_call(
        _s5_ops_kernel,
        grid=(noct,),
        in_specs=[
            pl.BlockSpec((S5_OCT, ns), lambda i: (i, 0)),
            pl.BlockSpec((S5_OCT, ns), lambda i: (i, 0)),
            pl.BlockSpec((S5_OCT, 1), lambda i: (i, 0)),
            pl.BlockSpec((S5_OCT, ns, gi), oct3),
            pl.BlockSpec((S5_OCT, ns, gi), oct3),
            pl.BlockSpec((S5_OCT, gi, ns), oct3),
            pl.BlockSpec((S5_OCT, gi, ns), oct3),
            pl.BlockSpec((S5_OCT, ns, gi), oct3),
            pl.BlockSpec((S5_OCT, ns, gi), oct3),
            pl.BlockSpec((npair, LANES), lambda i: (i, 0)),
            pl.BlockSpec((npair, LANES), lambda i: (i, 0)),
            pl.BlockSpec((npair, LANES), lambda i: (i, 0)),
        ],
        out_specs=[pl.BlockSpec((S5_OCT, kk, kk), oct3)] * 3 + [pl.BlockSpec((None, npair, LANES), oct3)] * 2,
        out_shape=[ops_shape] * 3 + [lam_shape] * 2,
        scratch_shapes=[pltpu.VMEM((10, S5_OCT, ns), F32)],
        compiler_params=_cparams(("arbitrary",), 2 * 3 * S5_OCT * kk * kk * 2 + 4 * S5_OCT * ns * LANES * 4 * 2),
        name="s5_ops",
    )(a_re, a_im, log_dt.reshape(g, 1), b_re, b_im, jnp.swapaxes(b_re, 1, 2), jnp.swapaxes(b_im, 1, 2),
      jnp.swapaxes(c_re, 1, 2), jnp.swapaxes(c_im, 1, 2), a_re_p, a_im_p, ldt_p)


def _glu_kernel(ya_ref, w_ref, b_ref, yc_ref, z_ref, o_ref):
    acc = jnp.dot(ya_ref[...], w_ref[...].astype(BF16), preferred_element_type=F32) + b_ref[...]
    o_ref[...] = (yc_ref[...].astype(F32) * _sigmoid(acc) * _silu(z_ref[...].astype(F32))).astype(o_ref.dtype)


def _glu(yact, w_glu, layer, b_glu, uz, tm=1024, tn=512):
    l, e = yact.shape
    tm = min(tm, l)
    zoff = e // tn
    vmem = 2 * (tm * e * 2 + e * tn * 4 + 3 * tm * tn * 2)
    return pl.pallas_call(
        _glu_kernel,
        grid=(l // tm, e // tn),
        in_specs=[pl.BlockSpec((tm, e), lambda i, j: (i, 0)),
                  pl.BlockSpec((None, e, tn), lambda i, j: (layer, 0, j)),
                  pl.BlockSpec((1, tn), lambda i, j: (0, j)),
                  pl.BlockSpec((tm, tn), lambda i, j: (i, j)),
                  pl.BlockSpec((tm, tn), lambda i, j: (i, zoff + j))],
        out_specs=pl.BlockSpec((tm, tn), lambda i, j: (i, j)),
        out_shape=jax.ShapeDtypeStruct((l, e), BF16),
        compiler_params=_cparams(("arbitrary", "arbitrary"), vmem),
        name="s5_glu",
    )(yact, w_glu, b_glu, yact, uz)


def _gla_layer(x, scale, shift, gate, ln_g, ln_b, alpha, w_in, layer, gate_w2, gate_b, norm_g, w_out):
    d = x.shape[1]
    e = w_out.shape[1]
    qk = gate_w2.shape[1]
    wg = jnp.pad(w_in[layer, :, 2 * qk + 2 * e:], ((0, 0), (0, LANES - GLA_GATE_RANK)))
    w2p = jnp.pad(gate_w2, ((0, LANES - GLA_GATE_RANK), (0, 0))).astype(BF16)
    proj, glr = _in_proj_gate(x, scale, shift, jnp.swapaxes(w_in, 1, 2), layer, wg)
    y = _gla_core(proj, glr, w2p, gate_b.reshape(1, qk), norm_g.reshape(1, e), e, qk)
    return _out_ln(y, w_out, layer, x, gate, ln_g.reshape(1, d), ln_b.reshape(1, d), alpha)


def _s5_layer(x, scale, shift, gate, ln_g, ln_b, alpha, w_in, layer, ops, d_skip, w_glu, b_glu, w_out):
    l, d = x.shape
    e = w_out.shape[1]
    t = S5_T
    tc = min(S5_TILE_BLOCKS, l // t)
    uz = _in_proj_perm(x, t, tc, scale, shift, w_in, layer)
    yact = _ssm_core(uz, tc, *ops, d_skip.reshape(1, e), e, oct0=layer * (e // (S5_OCT * S5_GROUP)))
    yglu = _glu(yact, w_glu, layer, b_glu.reshape(1, e), uz)
    return _out_ln(yglu, w_out, layer, x, gate, ln_g.reshape(1, d), ln_b.reshape(1, d), alpha,
                   step_major=(t, tc))


def kernel(x, c, ln_g, ln_b, ada_w, ada_b, gla_w_in, gla_gate_w2, gla_gate_b, gla_norm_g, gla_w_out,
           s5_w_in, s5_a_re, s5_a_im, s5_log_dt, s5_b_re, s5_b_im, s5_c_re, s5_c_im, s5_d,
           s5_w_glu, s5_b_glu, s5_w_out):
    bsz, l, d = x.shape
    assert bsz == 1, "batch 1 only"
    depth = ln_g.shape[0]
    alpha = (2 * depth) ** 0.25
    mod = _adaln(c, ada_w, ada_b)
    h = x.reshape(l, d)
    gla_w_out_b = gla_w_out.astype(BF16)
    s5_w_out_b = s5_w_out.astype(BF16)
    merge = lambda a: a.reshape((-1,) + a.shape[2:])
    s5_ops = _s5_operators(*(merge(a) for a in (s5_a_re, s5_a_im, s5_log_dt, s5_b_re, s5_b_im, s5_c_re, s5_c_im)))
    for i in range(depth):
        shift, scale, gate = mod[i, :, :d], mod[i, :, d:2 * d], mod[i, :, 2 * d:]
        j = i // 2
        if i % 2 == 0:
            h = _gla_layer(h, scale, shift, gate, ln_g[i], ln_b[i], alpha, gla_w_in, j, gla_gate_w2[j],
                           gla_gate_b[j], gla_norm_g[j], gla_w_out_b)
        else:
            h = _s5_layer(h, scale, shift, gate, ln_g[i], ln_b[i], alpha, s5_w_in, j, s5_ops, s5_d[j],
                          s5_w_glu, s5_b_glu[j], s5_w_out_b)
    return h.reshape(bsz, l, d)
```
